```python
import math
import jax, jax.numpy as jnp
from jax import lax
import numpy as np

D_MODEL = 1024
BATCH = 8
SEQ = 8192
DEPTH = 2

GRID_W = 64
CTX_LEN = 256
HEAD_DIM = 64
Q_BLOCK = 128
ROPE_THETA = 10000.0
EPS = 1e-6
N_MOD = 6

POOL_GROUPS = 4
POOL_WINDOWS = (2, 4, 8, 16)
POOL_GROUP_DIM = 64
POOL_WIDTH = POOL_GROUPS * POOL_GROUP_DIM
DIFF_HEADS = 6
DIFF_V_DIM = 2 * HEAD_DIM
DIFF_QK_WIDTH = DIFF_HEADS * 2 * HEAD_DIM
DIFF_WIDTH = DIFF_HEADS * DIFF_V_DIM
EVEN_IN_SIZES = (POOL_WIDTH, DIFF_QK_WIDTH, DIFF_QK_WIDTH, DIFF_WIDTH)
EVEN_OUT = POOL_WIDTH + DIFF_WIDTH
GQA_HEADS = 8
GQA_KV_HEADS = 2
GQA_REP = GQA_HEADS // GQA_KV_HEADS
NA_HEADS = 8
NA_WIN_ROWS = 8
NA_WIN_COLS = 16
ODD_IN_SIZES = (GQA_HEADS * HEAD_DIM, GQA_KV_HEADS * HEAD_DIM, GQA_KV_HEADS * HEAD_DIM,
                NA_HEADS * HEAD_DIM, NA_HEADS * HEAD_DIM, NA_HEADS * HEAD_DIM)
ODD_OUT = (GQA_HEADS + NA_HEADS) * HEAD_DIM
PEER_HEADS = 8
PEER_N_KEYS = 128
PEER_N_EXPERTS = PEER_N_KEYS * PEER_N_KEYS
PEER_KEY_DIM = 128
PEER_TOPK = 16
PEER_BLOCK = 128

kernel_name = "hybrid_pool_diff_gqa_na_peer_dit"


def rms_norm(x, g):
    xf = x.astype(jnp.float32)
    y = xf * lax.rsqrt(jnp.mean(xf * xf, axis=-1, keepdims=True) + EPS)
    return (y * g.astype(jnp.float32)).astype(x.dtype)


def split_cols(p, sizes):
    return jnp.split(p, [int(s) for s in np.cumsum(sizes)[:-1]], axis=-1)


def to_heads(p, n_heads):
    B, L, _ = p.shape
    return p.reshape(B, L, n_heads, -1).transpose(0, 2, 1, 3)


def merge_heads(o):
    B, H, L, d = o.shape
    return o.transpose(0, 2, 1, 3).reshape(B, L, H * d)


def modulation(cvec, ada_w, ada_b):
    return jnp.split(jax.nn.silu(cvec) @ ada_w + ada_b, N_MOD, axis=-1)


def modulate(x, g, shift, scale):
    return rms_norm(x, g) * (1.0 + scale[:, None]) + shift[:, None]


def axial_rope_tables(T):
    t = jnp.arange(T, dtype=jnp.int32)
    pos = jnp.stack([t // GRID_W, t % GRID_W], axis=-1).astype(jnp.float32)
    n_freq = HEAD_DIM // 4
    inv_freq = ROPE_THETA ** (-jnp.arange(n_freq, dtype=jnp.float32) / n_freq)
    ang = pos[:, :, None] * inv_freq
    return jnp.cos(ang), jnp.sin(ang)


def apply_axial_rope(x, cos, sin):
    shp = x.shape
    x4 = x.astype(jnp.float32).reshape(shp[:-1] + (2, 2, HEAD_DIM // 4))
    x1, x2 = x4[..., 0, :], x4[..., 1, :]
    out = jnp.stack([x1 * cos - x2 * sin, x2 * cos + x1 * sin], axis=-2)
    return out.reshape(shp).astype(x.dtype)


def sweep_query_blocks(fn, q):
    T = q.shape[-2]
    nb = T // Q_BLOCK
    qb = jnp.moveaxis(q.reshape(q.shape[:-2] + (nb, Q_BLOCK, q.shape[-1])), -3, 0)
    out = jnp.moveaxis(lax.map(fn, qb), 0, -3)
    return out.reshape(out.shape[:-3] + (T, out.shape[-1]))


def diff_attend(q, k, v, lam):
    s = jnp.einsum("bhiqd,bhind->bhiqn", q, k, preferred_element_type=jnp.float32) * (HEAD_DIM ** -0.5)
    a = jax.nn.softmax(s, axis=-1)
    w = a[:, :, 0] - lam * a[:, :, 1]
    return jnp.einsum("bhqn,bhnv->bhqv", w.astype(v.dtype), v)


def gqa_attend(q, k, v):
    s = jnp.einsum("bgrqd,bgnd->bgrqn", q, k, preferred_element_type=jnp.float32) * (HEAD_DIM ** -0.5)
    a = jax.nn.softmax(s, axis=-1)
    return jnp.einsum("bgrqn,bgnd->bgrqd", a.astype(v.dtype), v)


def multiscale_pool(u, w, scale):
    B, L, _ = u.shape
    uf = u.astype(jnp.float32)
    cs = jnp.concatenate([jnp.zeros((B, 1, POOL_WIDTH), jnp.float32), jnp.cumsum(uf, axis=1)], axis=1)
    t = jnp.arange(L, dtype=jnp.int32)
    outs = []
    for g, win in enumerate(POOL_WINDOWS):
        lo = jnp.clip(t - win // 2, 0, L)
        hi = jnp.clip(t + win // 2, 0, L)
        sl = slice(g * POOL_GROUP_DIM, (g + 1) * POOL_GROUP_DIM)
        csg = cs[:, :, sl]
        mean = (csg[:, hi] - csg[:, lo]) / (hi - lo).astype(jnp.float32)[:, None]
        outs.append(mean - uf[:, :, sl])
    p = jnp.stack(outs, axis=2).astype(u.dtype)
    y = jnp.einsum("blgc,gcd->blgd", p, w).reshape(B, L, POOL_WIDTH)
    return y * scale


def neighbourhood_attend(q, k, v, k_ctx, v_ctx, rpb):
    B, H, T, d = q.shape
    rows = T // GRID_W
    kr = min(NA_WIN_ROWS, rows)
    kc = NA_WIN_COLS
    kg = k.reshape(B, H, rows, GRID_W, d)
    vg = v.reshape(B, H, rows, GRID_W, d)
    cols = jnp.arange(GRID_W, dtype=jnp.int32)
    col_idx = jnp.clip(cols - kc // 2, 0, GRID_W - kc)[:, None] + jnp.arange(kc, dtype=jnp.int32)[None]
    dc = col_idx - cols[:, None] + (NA_WIN_COLS - 1)
    q_rows = jnp.moveaxis(q.reshape(B, H, rows, GRID_W, d), 2, 0)
    scale = d ** -0.5

    def one_row(args):
        q_r, r = args
        r0 = jnp.clip(r - kr // 2, 0, rows - kr)
        k_nb = lax.dynamic_slice_in_dim(kg, r0, kr, axis=2)[:, :, :, col_idx]
        v_nb = lax.dynamic_slice_in_dim(vg, r0, kr, axis=2)[:, :, :, col_idx]
        dr = r0 + jnp.arange(kr, dtype=jnp.int32) - r + (NA_WIN_ROWS - 1)
        bias = rpb[:, dr[:, None, None], dc[None]].astype(jnp.float32)
        s_nb = jnp.einsum("bhcd,bhrckd->bhcrk", q_r, k_nb, preferred_element_type=jnp.float32) * scale
        s_nb = s_nb + jnp.transpose(bias, (0, 2, 1, 3))[None]
        s_ctx = jnp.einsum("bhcd,bhnd->bhcn", q_r, k_ctx, preferred_element_type=jnp.float32) * scale
        s = jnp.concatenate([s_nb.reshape(B, H, GRID_W, kr * kc), s_ctx], axis=-1)
        a = jax.nn.softmax(s, axis=-1).astype(v.dtype)
        a_nb = a[..., :kr * kc].reshape(B, H, GRID_W, kr, kc)
        return (jnp.einsum("bhcrk,bhrckd->bhcd", a_nb, v_nb)
                + jnp.einsum("bhcn,bhnd->bhcd", a[..., kr * kc:], v_ctx))

    out = lax.map(one_row, (q_rows, jnp.arange(rows, dtype=jnp.int32)))
    return jnp.moveaxis(out, 0, 2).reshape(B, H, T, d)


def mixer_pool_diff(p_lat, p_ctx, cos, sin, lam_init, pool_w, pool_scale, qn, kn,
                    lq1, lk1, lq2, lk2, subln, need_ctx):
    u_l, q_l, k_l, v_l = split_cols(p_lat, EVEN_IN_SIZES)
    u_c, q_c, k_c, v_c = split_cols(p_ctx, EVEN_IN_SIZES)

    def qk_heads(p, g):
        B, L, _ = p.shape
        return rms_norm(p.reshape(B, L, DIFF_HEADS, 2, HEAD_DIM).transpose(0, 2, 3, 1, 4), g)

    q_l = apply_axial_rope(qk_heads(q_l, qn), cos, sin)
    k_l = apply_axial_rope(qk_heads(k_l, kn), cos, sin)
    k_c = qk_heads(k_c, kn)
    v_c = to_heads(v_c, DIFF_HEADS)
    k_all = jnp.concatenate([k_c, k_l], axis=3)
    v_all = jnp.concatenate([v_c, to_heads(v_l, DIFF_HEADS)], axis=2)
    f32 = jnp.float32
    lam = (jnp.exp(jnp.sum(lq1.astype(f32) * lk1.astype(f32)))
           - jnp.exp(jnp.sum(lq2.astype(f32) * lk2.astype(f32))) + lam_init)

    def post(o):
        return merge_heads(rms_norm(o, subln) * (1.0 - lam_init))

    o_l = sweep_query_blocks(lambda qb: diff_attend(qb, k_all, v_all, lam), q_l)
    y_lat = jnp.concatenate([multiscale_pool(u_l, pool_w, pool_scale), post(o_l)], axis=-1)
    if not need_ctx:
        return y_lat, None
    o_c = diff_attend(qk_heads(q_c, qn), k_c, v_c, lam)
    y_ctx = jnp.concatenate([multiscale_pool(u_c, pool_w, pool_scale), post(o_c)], axis=-1)
    return y_lat, y_ctx


def mixer_gqa_na(p_lat, p_ctx, cos, sin, gqa_qn, gqa_kn, na_qn, na_kn, rpb, need_ctx):
    cq_l, ck_l, cv_l, nq_l, nk_l, nv_l = split_cols(p_lat, ODD_IN_SIZES)
    cq_c, ck_c, cv_c, nq_c, nk_c, nv_c = split_cols(p_ctx, ODD_IN_SIZES)
    B, T, _ = p_lat.shape
    C = p_ctx.shape[1]
    q = apply_axial_rope(rms_norm(to_heads(cq_l, GQA_HEADS), gqa_qn), cos, sin)
    q = q.reshape(B, GQA_KV_HEADS, GQA_REP, T, HEAD_DIM)
    kc = rms_norm(to_heads(ck_c, GQA_KV_HEADS), gqa_kn)
    vc = to_heads(cv_c, GQA_KV_HEADS)
    kl = apply_axial_rope(rms_norm(to_heads(ck_l, GQA_KV_HEADS), gqa_kn), cos, sin)
    k_all = jnp.concatenate([kc, kl], axis=2)
    v_all = jnp.concatenate([vc, to_heads(cv_l, GQA_KV_HEADS)], axis=2)
    o_gqa = sweep_query_blocks(lambda qb: gqa_attend(qb, k_all, v_all), q).reshape(B, GQA_HEADS, T, HEAD_DIM)
    nkc = rms_norm(to_heads(nk_c, NA_HEADS), na_kn)
    nvc = to_heads(nv_c, NA_HEADS)
    o_na = neighbourhood_attend(rms_norm(to_heads(nq_l, NA_HEADS), na_qn),
                                rms_norm(to_heads(nk_l, NA_HEADS), na_kn),
                                to_heads(nv_l, NA_HEADS), nkc, nvc, rpb)
    y_lat = jnp.concatenate([merge_heads(o_gqa), merge_heads(o_na)], axis=-1)
    if not need_ctx:
        return y_lat, None
    qcc = rms_norm(to_heads(cq_c, GQA_HEADS), gqa_qn).reshape(B, GQA_KV_HEADS, GQA_REP, C, HEAD_DIM)
    o_gqa_c = gqa_attend(qcc, kc, vc).reshape(B, GQA_HEADS, C, HEAD_DIM)
    o_na_c = gqa_attend(rms_norm(to_heads(nq_c, NA_HEADS), na_qn)[:, :, None], nkc, nvc)[:, :, 0]
    y_ctx = jnp.concatenate([merge_heads(o_gqa_c), merge_heads(o_na_c)], axis=-1)
    return y_lat, y_ctx


def peer_ffn(h, wq, sub_keys, u, v):
    B, L, D = h.shape
    nb = L // PEER_BLOCK
    hb = jnp.moveaxis(h.reshape(B, nb, PEER_BLOCK, D), 1, 0)

    def block(x):
        q = (x @ wq).reshape(B, PEER_BLOCK, PEER_HEADS, 2, PEER_KEY_DIM)
        s = jnp.einsum("bthpd,hpnd->bthpn", q, sub_keys, preferred_element_type=jnp.float32)
        sv, si = lax.top_k(s, PEER_TOPK)
        cand_s = (sv[..., 0, :, None] + sv[..., 1, None, :]).reshape(B, PEER_BLOCK, PEER_HEADS, PEER_TOPK * PEER_TOPK)
        cand_i = (si[..., 0, :, None] * PEER_N_KEYS + si[..., 1, None, :]).reshape(B, PEER_BLOCK, PEER_HEADS, PEER_TOPK * PEER_TOPK)
        top_s, pos = lax.top_k(cand_s, PEER_TOPK)
        idx = jnp.take_along_axis(cand_i, pos, axis=-1)
        g = jax.nn.softmax(top_s, axis=-1).astype(x.dtype)
        act = jax.nn.gelu(jnp.einsum("bthkd,btd->bthk", u[idx], x), approximate=False)
        return jnp.einsum("bthk,bthkd->btd", g * act, v[idx])

    out = lax.map(block, hb)
    return jnp.moveaxis(out, 0, 1).reshape(B, L, D)


def setup_inputs(seed: int = 0) -> dict:
    key = jax.random.key(seed)
    keys = iter(jax.random.split(key, 64))

    def nrm(shape, scale):
        return jax.random.normal(next(keys), shape, jnp.float32) * scale

    def gain(n):
        return 1.0 + nrm((n,), 0.1)

    D = D_MODEL
    inp = {}
    inp["x"] = nrm((BATCH, SEQ, D), 1.0)
    inp["c"] = nrm((BATCH, D), 1.0)
    inp["ctx"] = nrm((BATCH, CTX_LEN, D), 1.0)
    inp["c_ctx"] = nrm((D,), 1.0)
    inp["l0_ada_w"] = nrm((D, N_MOD * D), D ** -0.5)
    inp["l0_ada_b"] = nrm((N_MOD * D,), 0.02)
    inp["l0_norm1"] = gain(D)
    inp["l0_norm2"] = gain(D)
    inp["l0_w_in"] = nrm((D, sum(EVEN_IN_SIZES)), D ** -0.5)
    inp["l0_w_out"] = nrm((EVEN_OUT, D), EVEN_OUT ** -0.5)
    inp["l0_pool_w"] = nrm((POOL_GROUPS, POOL_GROUP_DIM, POOL_GROUP_DIM), POOL_GROUP_DIM ** -0.5)
    inp["l0_pool_scale"] = gain(POOL_WIDTH)
    inp["l0_diff_qnorm"] = gain(HEAD_DIM)
    inp["l0_diff_knorm"] = gain(HEAD_DIM)
    inp["l0_lambda_q1"] = nrm((HEAD_DIM,), 0.1)
    inp["l0_lambda_k1"] = nrm((HEAD_DIM,), 0.1)
    inp["l0_lambda_q2"] = nrm((HEAD_DIM,), 0.1)
    inp["l0_lambda_k2"] = nrm((HEAD_DIM,), 0.1)
    inp["l0_diff_subln"] = gain(DIFF_V_DIM)
    inp["l0_peer_wq"] = nrm((D, PEER_HEADS * 2 * PEER_KEY_DIM), D ** -0.5)
    inp["l0_peer_keys"] = nrm((PEER_HEADS, 2, PEER_N_KEYS, PEER_KEY_DIM), PEER_KEY_DIM ** -0.5)
    inp["l0_peer_u"] = nrm((PEER_N_EXPERTS, D), D ** -0.5)
    inp["l0_peer_v"] = nrm((PEER_N_EXPERTS, D), 0.5)
    inp["l1_ada_w"] = nrm((D, N_MOD * D), D ** -0.5)
    inp["l1_ada_b"] = nrm((N_MOD * D,), 0.02)
    inp["l1_norm1"] = gain(D)
    inp["l1_norm2"] = gain(D)
    inp["l1_w_in"] = nrm((D, sum(ODD_IN_SIZES)), D ** -0.5)
    inp["l1_w_out"] = nrm((ODD_OUT, D), ODD_OUT ** -0.5)
    inp["l1_gqa_qnorm"] = gain(HEAD_DIM)
    inp["l1_gqa_knorm"] = gain(HEAD_DIM)
    inp["l1_na_qnorm"] = gain(HEAD_DIM)
    inp["l1_na_knorm"] = gain(HEAD_DIM)
    inp["l1_na_rpb"] = nrm((NA_HEADS, 2 * NA_WIN_ROWS - 1, 2 * NA_WIN_COLS - 1), 0.1)
    inp["l1_peer_wq"] = nrm((D, PEER_HEADS * 2 * PEER_KEY_DIM), D ** -0.5)
    inp["l1_peer_keys"] = nrm((PEER_HEADS, 2, PEER_N_KEYS, PEER_KEY_DIM), PEER_KEY_DIM ** -0.5)
    inp["l1_peer_u"] = nrm((PEER_N_EXPERTS, D), D ** -0.5)
    inp["l1_peer_v"] = nrm((PEER_N_EXPERTS, D), 0.5)
    return inp


def reference(x, c, ctx, c_ctx,
              l0_ada_w, l0_ada_b, l0_norm1, l0_norm2, l0_w_in, l0_w_out, l0_pool_w, l0_pool_scale,
              l0_diff_qnorm, l0_diff_knorm, l0_lambda_q1, l0_lambda_k1, l0_lambda_q2, l0_lambda_k2,
              l0_diff_subln, l0_peer_wq, l0_peer_keys, l0_peer_u, l0_peer_v,
              l1_ada_w, l1_ada_b, l1_norm1, l1_norm2, l1_w_in, l1_w_out, l1_gqa_qnorm, l1_gqa_knorm,
              l1_na_qnorm, l1_na_knorm, l1_na_rpb, l1_peer_wq, l1_peer_keys, l1_peer_u, l1_peer_v):
    T = x.shape[1]
    cos, sin = axial_rope_tables(T)
    layers = [
        dict(ada_w=l0_ada_w, ada_b=l0_ada_b, norm1=l0_norm1, norm2=l0_norm2, w_in=l0_w_in, w_out=l0_w_out,
             mix=(l0_pool_w, l0_pool_scale, l0_diff_qnorm, l0_diff_knorm, l0_lambda_q1, l0_lambda_k1,
                  l0_lambda_q2, l0_lambda_k2, l0_diff_subln),
             peer=(l0_peer_wq, l0_peer_keys, l0_peer_u, l0_peer_v)),
        dict(ada_w=l1_ada_w, ada_b=l1_ada_b, norm1=l1_norm1, norm2=l1_norm2, w_in=l1_w_in, w_out=l1_w_out,
             mix=(l1_gqa_qnorm, l1_gqa_knorm, l1_na_qnorm, l1_na_knorm, l1_na_rpb),
             peer=(l1_peer_wq, l1_peer_keys, l1_peer_u, l1_peer_v)),
    ]
    xc = ctx
    for i in range(DEPTH):
        lp = layers[i]
        need_ctx = i < DEPTH - 1
        sh1, sc1, g1, sh2, sc2, g2 = modulation(c, lp["ada_w"], lp["ada_b"])
        csh1, csc1, cg1, csh2, csc2, cg2 = modulation(c_ctx[None], lp["ada_w"], lp["ada_b"])
        p_lat = modulate(x, lp["norm1"], sh1, sc1) @ lp["w_in"]
        p_ctx = modulate(xc, lp["norm1"], csh1, csc1) @ lp["w_in"]
        if i % 2 == 0:
            y, yc = mixer_pool_diff(p_lat, p_ctx, cos, sin, 0.8 - 0.6 * math.exp(-0.3 * i), *lp["mix"], need_ctx)
        else:
            y, yc = mixer_gqa_na(p_lat, p_ctx, cos, sin, *lp["mix"], need_ctx)
        x = x + g1[:, None] * (y @ lp["w_out"])
        x = x + g2[:, None] * peer_ffn(modulate(x, lp["norm2"], sh2, sc2), *lp["peer"])
        if need_ctx:
            xc = xc + cg1[:, None] * (yc @ lp["w_out"])
            xc = xc + cg2[:, None] * peer_ffn(modulate(xc, lp["norm2"], csh2, csc2), *lp["peer"])
    return x
```

```python
import functools
import math

import jax
import jax.numpy as jnp
from jax import lax
from jax.experimental import pallas as pl
from jax.experimental.pallas import tpu as pltpu

F32 = jnp.float32
BF16 = jnp.bfloat16

LANES = 128
SUBLANES = 8
VMEM_LIMIT_BYTES = 56 * 1024 * 1024

HEAD_DIM = 64
GRID_W = 64
ROPE_THETA = 10000.0
EPS = 1e-6
N_MOD = 6
POOL_WINDOWS = (2, 4, 8, 16)
POOL_HALO = max(POOL_WINDOWS) // 2
DIFF_HEADS = 6
GQA_KV_HEADS = 2
NA_WIN_ROWS = 8
NA_WIN_COLS = 16
PEER_HEADS = 8
PEER_N_KEYS = 128
PEER_TOPK = 16
TOK_TILE = 256
KV_CHUNK = 512
NEG = -1e30
SQRT_HALF = 0.7071067811865476


def _cparams(sem, vmem=VMEM_LIMIT_BYTES):
    return pltpu.CompilerParams(dimension_semantics=sem, vmem_limit_bytes=vmem)


def _split(x):
    hi = x.astype(BF16)
    lo = (x - hi.astype(F32)).astype(BF16)
    return hi, lo


_NN = (((1,), (0,)), ((), ()))
_NT = (((1,), (1,)), ((), ()))


def _dot(a, b, dims=_NN):
    return lax.dot_general(a, b, dims, preferred_element_type=F32)


def _dot3(a_hi, a_lo, b_hi, b_lo, dims=_NN):
    return _dot(a_hi, b_hi, dims) + _dot(a_hi, b_lo, dims) + _dot(a_lo, b_hi, dims)


def _lane_iota(shape):
    return lax.broadcasted_iota(jnp.int32, shape, len(shape) - 1)


def _rms(x, gain):
    ms = jnp.mean(x * x, axis=-1, keepdims=True)
    return x * lax.rsqrt(ms + EPS) * gain


def _ada_kernel(c_ref, w_ref, b_ref, o_ref):
    c = c_ref[...]
    a = c / (1.0 + jnp.exp(-c))
    a_hi, a_lo = _split(a)
    w_hi, w_lo = _split(w_ref[...])
    o_ref[...] = _dot3(a_hi, a_lo, w_hi, w_lo) + b_ref[...]


def _ada_call(cvec, ada_w, ada_b):
    rows, d = cvec.shape
    n = ada_w.shape[1]
    bn = 768
    return pl.pallas_call(
        _ada_kernel,
        grid=(n // bn,),
        in_specs=[
            pl.BlockSpec((rows, d), lambda j: (0, 0)),
            pl.BlockSpec((d, bn), lambda j: (0, j)),
            pl.BlockSpec((1, bn), lambda j: (0, j)),
        ],
        out_specs=pl.BlockSpec((rows, bn), lambda j: (0, j)),
        out_shape=jax.ShapeDtypeStruct((rows, n), F32),
        compiler_params=_cparams(("parallel",)),
        name="ada_mod",
    )(cvec, ada_w, ada_b.reshape(1, n))


def _modulation(c, c_ctx, ada_w, ada_b):
    b, d = c.shape
    rows = -(-(b + 1) // 16) * 16
    cvec = jnp.zeros((rows, d), F32).at[0].set(c_ctx).at[1:b + 1].set(c)
    m = _ada_call(cvec, ada_w, ada_b).reshape(rows, N_MOD, d)
    return jnp.stack([jnp.broadcast_to(m[0], (b, N_MOD, d)), m[1:b + 1]], axis=1)


def _head_norm(y, gain):
    r = lax.broadcasted_iota(jnp.int32, (LANES, LANES), 0) // HEAD_DIM
    c = lax.broadcasted_iota(jnp.int32, (LANES, LANES), 1) // HEAD_DIM
    ones_bd = jnp.where(r == c, 1.0, 0.0).astype(BF16)
    hi, lo = _split(y * y)
    ss = _dot(hi, ones_bd) + _dot(lo, ones_bd)
    return y * lax.rsqrt(ss * (1.0 / HEAD_DIM) + EPS) * gain


def _rope(y, cos, sin):
    up = pltpu.roll(y, LANES - 16, 1)
    down = pltpu.roll(y, 16, 1)
    partner = jnp.where((_lane_iota(y.shape) & 16) == 0, up, down)
    return y * cos + partner * sin


def _inproj_kernel(roles, fuse_residual, *refs):
    refs = list(refs)
    x_ref = refs.pop(0)
    if fuse_residual:
        y_ref = refs.pop(0)
        pmod_ref = refs.pop(0)
    mod_ref = refs.pop(0)
    norm_ref = refs.pop(0)
    cos_ref = refs.pop(0)
    sin_ref = refs.pop(0)
    w_refs, g_refs = [], []
    for kind, _, _ in roles:
        w_refs.append(refs.pop(0))
        g_refs.append(refs.pop(0) if kind in ("norm", "norm_rope") else None)
    if fuse_residual:
        xo_ref = refs.pop(0)
    out_refs = refs

    x = x_ref[0]
    if fuse_residual:
        x = x + pmod_ref[0, 0, 5:6, :] * y_ref[0]
        xo_ref[0] = x
    xm = _rms(x, norm_ref[...]) * (1.0 + mod_ref[0, 0, 1:2, :]) + mod_ref[0, 0, 0:1, :]
    xm = xm.astype(BF16)
    cos = cos_ref[...]
    sin = sin_ref[...]
    for (kind, width, scale), w_ref, g_ref, o_ref in zip(roles, w_refs, g_refs, out_refs):
        acc = _dot(xm, w_ref[...])
        if kind in ("plain", "plain_f32"):
            o_ref[0] = acc.astype(o_ref.dtype)
            continue
        for j in range(width // LANES):
            y = _head_norm(acc[:, j * LANES:(j + 1) * LANES], g_ref[...])
            if kind == "norm_rope":
                y = _rope(y, cos, sin)
            if scale != 1.0:
                y = y * scale
            o_ref[0, :, j * LANES:(j + 1) * LANES] = y.astype(o_ref.dtype)


def _inproj_call(x, mods, norm_g, cos, sin, roles, weights, gains, residual=None):
    b, s, d = x.shape
    nt = s // TOK_TILE
    tok = lambda w: pl.BlockSpec((1, TOK_TILE, w), lambda i, t: (i, t, 0))
    mod_spec = pl.BlockSpec((1, 1, N_MOD, d), lambda i, t: (i, jnp.minimum(t, 1), 0, 0))
    args, specs = [x], [tok(d)]
    if residual is not None:
        y, pmods = residual
        args += [y, pmods]
        specs += [tok(d), mod_spec]
    args += [mods, norm_g.reshape(1, d), cos, sin]
    specs += [mod_spec, pl.BlockSpec((1, d), lambda i, t: (0, 0)),
              pl.BlockSpec((TOK_TILE, LANES), lambda i, t: (t, 0)),
              pl.BlockSpec((TOK_TILE, LANES), lambda i, t: (t, 0))]
    for (kind, width, _), w, g in zip(roles, weights, gains):
        args.append(w)
        specs.append(pl.BlockSpec((d, width), lambda i, t: (0, 0)))
        if kind in ("norm", "norm_rope"):
            args.append(g)
            specs.append(pl.BlockSpec((1, LANES), lambda i, t: (0, 0)))
    out_shapes, out_specs = [], []
    if residual is not None:
        out_shapes.append(jax.ShapeDtypeStruct((b, s, d), F32))
        out_specs.append(tok(d))
    for kind, width, _ in roles:
        dt = F32 if kind == "plain_f32" else BF16
        out_shapes.append(jax.ShapeDtypeStruct((b, s, width), dt))
        out_specs.append(tok(width))
    return pl.pallas_call(
        functools.partial(_inproj_kernel, tuple(roles), residual is not None),
        grid=(b, nt),
        in_specs=specs,
        out_specs=out_specs,
        out_shape=out_shapes,
        compiler_params=_cparams(("parallel", "parallel")),
        name="in_proj",
    )(*args)


def _rope_tables(s_total, ctx_len):
    t = jnp.arange(s_total - ctx_len, dtype=jnp.int32)
    pos = jnp.stack([t // GRID_W, t % GRID_W], axis=-1).astype(F32)
    n_freq = HEAD_DIM // 4
    inv_freq = ROPE_THETA ** (-jnp.arange(n_freq, dtype=F32) / n_freq)
    ang = pos[:, :, None] * inv_freq
    cos, sin = jnp.cos(ang), jnp.sin(ang)
    cos64 = jnp.concatenate([cos[:, 0], cos[:, 0], cos[:, 1], cos[:, 1]], axis=-1)
    sin64 = jnp.concatenate([-sin[:, 0], sin[:, 0], -sin[:, 1], sin[:, 1]], axis=-1)
    cos128 = jnp.concatenate([jnp.ones((ctx_len, LANES), F32), jnp.tile(cos64, (1, 2))], axis=0)
    sin128 = jnp.concatenate([jnp.zeros((ctx_len, LANES), F32), jnp.tile(sin64, (1, 2))], axis=0)
    return cos128, sin128


def _attend_all(q, k_ref, v_ref, is_latent, n_lat_chunks, m_ref, l_ref, acc_ref):
    m_ref[...] = jnp.full(m_ref.shape, NEG, F32)
    l_ref[...] = jnp.zeros(l_ref.shape, F32)
    acc_ref[...] = jnp.zeros(acc_ref.shape, F32)

    def step(kc, vc):
        s = _dot(q, kc, _NT)
        m_prev = m_ref[...]
        m_new = jnp.maximum(m_prev, jnp.max(s, axis=-1, keepdims=True))
        alpha = jnp.exp(m_prev - m_new)
        p = jnp.exp(s - m_new)
        l_ref[...] = alpha * l_ref[...] + jnp.sum(p, axis=-1, keepdims=True)
        acc_ref[...] = alpha * acc_ref[...] + _dot(p.astype(BF16), vc)
        m_ref[...] = m_new

    step(k_ref[0, 0:TOK_TILE, :], v_ref[0, 0:TOK_TILE, :])

    def body(j, carry):
        off = pl.multiple_of(TOK_TILE + j * KV_CHUNK, TOK_TILE)
        step(k_ref[0, pl.ds(off, KV_CHUNK), :], v_ref[0, pl.ds(off, KV_CHUNK), :])
        return carry

    lax.fori_loop(0, jnp.where(is_latent, n_lat_chunks, 0), body, 0)
    return acc_ref[...] / l_ref[...]


def _half_masks(q):
    lo = _lane_iota(q.shape) < HEAD_DIM
    zero = jnp.zeros_like(q)
    return jnp.where(lo, q, zero), jnp.where(lo, zero, q)


def _diff_attn_kernel(n_lat_chunks, lam_init, q_ref, k_ref, v_ref, lam_ref, subln_ref, o_ref,
                      m_ref, l_ref, acc_ref):
    q1, q2 = _half_masks(q_ref[0])
    q = jnp.concatenate([q1, q2], axis=0)
    o = _attend_all(q, k_ref, v_ref, pl.program_id(2) > 0, n_lat_chunks, m_ref, l_ref, acc_ref)
    lv = lam_ref[...]
    lam = (jnp.exp(jnp.sum(lv[0:1] * lv[1:2], axis=-1, keepdims=True))
           - jnp.exp(jnp.sum(lv[2:3] * lv[3:4], axis=-1, keepdims=True)) + lam_init)
    od = o[:TOK_TILE] - lam * o[TOK_TILE:]
    o_ref[0] = (_rms(od, subln_ref[...]) * (1.0 - lam_init)).astype(o_ref.dtype)


def _diff_attn_call(q, k, v, lam_vecs, subln, lam_init):
    b, s, w = q.shape
    heads = w // LANES
    n_lat_chunks = (s - TOK_TILE) // KV_CHUNK
    seq = pl.BlockSpec((1, s, LANES), lambda i, h, t: (i, 0, h))
    return pl.pallas_call(
        functools.partial(_diff_attn_kernel, n_lat_chunks, lam_init),
        grid=(b, heads, s // TOK_TILE),
        in_specs=[
            pl.BlockSpec((1, TOK_TILE, LANES), lambda i, h, t: (i, t, h)),
            seq, seq,
            pl.BlockSpec((4, HEAD_DIM), lambda i, h, t: (0, 0)),
            pl.BlockSpec((1, LANES), lambda i, h, t: (0, 0)),
        ],
        out_specs=pl.BlockSpec((1, TOK_TILE, LANES), lambda i, h, t: (i, t, h)),
        out_shape=jax.ShapeDtypeStruct((b, s, w), BF16),
        scratch_shapes=[pltpu.VMEM((2 * TOK_TILE, 1), F32), pltpu.VMEM((2 * TOK_TILE, 1), F32),
                        pltpu.VMEM((2 * TOK_TILE, LANES), F32)],
        compiler_params=_cparams(("parallel", "parallel", "parallel")),
        name="diff_attn",
    )(q, k, v, lam_vecs, subln.reshape(1, LANES))


def _gqa_attn_kernel(n_lat_chunks, q_ref, k_ref, v_ref, o_ref, m_ref, l_ref, acc_ref):
    qa = _half_masks(q_ref[0, :, 0:LANES])
    qb = _half_masks(q_ref[0, :, LANES:2 * LANES])
    q = jnp.concatenate([qa[0], qa[1], qb[0], qb[1]], axis=0)
    o = _attend_all(q, k_ref, v_ref, pl.program_id(2) > 0, n_lat_chunks, m_ref, l_ref, acc_ref)
    lo = _lane_iota((TOK_TILE, LANES)) < HEAD_DIM
    t = TOK_TILE
    o_ref[0, :, 0:LANES] = jnp.where(lo, o[0:t], o[t:2 * t]).astype(o_ref.dtype)
    o_ref[0, :, LANES:2 * LANES] = jnp.where(lo, o[2 * t:3 * t], o[3 * t:4 * t]).astype(o_ref.dtype)


def _gqa_attn_call(q, k_dup, v_dup):
    b, s, w = q.shape
    groups = w // (2 * LANES)
    n_lat_chunks = (s - TOK_TILE) // KV_CHUNK
    seq = pl.BlockSpec((1, s, LANES), lambda i, g, t: (i, 0, g))
    return pl.pallas_call(
        functools.partial(_gqa_attn_kernel, n_lat_chunks),
        grid=(b, groups, s // TOK_TILE),
        in_specs=[pl.BlockSpec((1, TOK_TILE, 2 * LANES), lambda i, g, t: (i, t, g)), seq, seq],
        out_specs=pl.BlockSpec((1, TOK_TILE, 2 * LANES), lambda i, g, t: (i, t, g)),
        out_shape=jax.ShapeDtypeStruct((b, s, w), BF16),
        scratch_shapes=[pltpu.VMEM((4 * TOK_TILE, 1), F32), pltpu.VMEM((4 * TOK_TILE, 1), F32),
                        pltpu.VMEM((4 * TOK_TILE, LANES), F32)],
        compiler_params=_cparams(("parallel", "parallel", "parallel")),
        name="gqa_attn",
    )(q, k_dup, v_dup)


def _na_attn_kernel(n_rows, q_ref, k_ref, v_ref, bias_ref, o_ref):
    t = pl.program_id(2)

    @pl.when(t == 0)
    def _():
        o_ref[0] = jnp.zeros(o_ref.shape[1:], o_ref.dtype)

    @pl.when(t > 0)
    def _():
        kctx = k_ref[0, 0:TOK_TILE, :]
        vctx = v_ref[0, 0:TOK_TILE, :]
        win = NA_WIN_ROWS * GRID_W
        rows_per_tile = TOK_TILE // GRID_W
        lo = _lane_iota((GRID_W, LANES)) < HEAD_DIM
        for i in range(rows_per_tile):
            r = (t - 1) * rows_per_tile + i
            r0 = jnp.clip(r - NA_WIN_ROWS // 2, 0, n_rows - NA_WIN_ROWS)
            off = pl.multiple_of(TOK_TILE + r0 * GRID_W, GRID_W)
            q1, q2 = _half_masks(q_ref[0, i * GRID_W:(i + 1) * GRID_W, :])
            q = jnp.concatenate([q1, q2], axis=0)
            s_nb = _dot(q, k_ref[0, pl.ds(off, win), :], _NT) + bias_ref[0, r - r0]
            s_cx = _dot(q, kctx, _NT)
            m = jnp.maximum(jnp.max(s_nb, axis=-1, keepdims=True), jnp.max(s_cx, axis=-1, keepdims=True))
            p_nb = jnp.exp(s_nb - m)
            p_cx = jnp.exp(s_cx - m)
            den = jnp.sum(p_nb, axis=-1, keepdims=True) + jnp.sum(p_cx, axis=-1, keepdims=True)
            o = (_dot(p_nb.astype(BF16), v_ref[0, pl.ds(off, win), :]) + _dot(p_cx.astype(BF16), vctx)) / den
            o_ref[0, i * GRID_W:(i + 1) * GRID_W, :] = jnp.where(lo, o[:GRID_W], o[GRID_W:]).astype(o_ref.dtype)


def _na_bias_table(rpb, n_rows):
    kr = min(NA_WIN_ROWS, n_rows)
    cols = jnp.arange(GRID_W, dtype=jnp.int32)
    c0 = jnp.clip(cols - NA_WIN_COLS // 2, 0, GRID_W - NA_WIN_COLS)
    kc = jnp.arange(GRID_W, dtype=jnp.int32)
    inside = (kc[None, :] >= c0[:, None]) & (kc[None, :] < c0[:, None] + NA_WIN_COLS)
    dc = jnp.clip(kc[None, :] - cols[:, None] + (NA_WIN_COLS - 1), 0, 2 * NA_WIN_COLS - 2)
    var = jnp.arange(NA_WIN_ROWS, dtype=jnp.int32)
    dr = jnp.clip(jnp.arange(kr, dtype=jnp.int32)[None, :] - var[:, None] + (NA_WIN_ROWS - 1),
                  0, 2 * NA_WIN_ROWS - 2)
    tab = rpb.astype(F32)[:, dr[:, :, None, None], dc[None, None, :, :]]
    tab = jnp.where(inside[None, None, None], tab, NEG)
    h = rpb.shape[0]
    tab = tab.transpose(0, 1, 3, 2, 4).reshape(h // 2, 2, NA_WIN_ROWS, GRID_W, kr * GRID_W)
    return tab.transpose(0, 2, 1, 3, 4).reshape(h // 2, NA_WIN_ROWS, 2 * GRID_W, kr * GRID_W)


def _na_attn_call(q, k, v, bias):
    b, s, w = q.shape
    n_rows = (s - TOK_TILE) // GRID_W
    seq = pl.BlockSpec((1, s, LANES), lambda i, h, t: (i, 0, h))
    tile = pl.BlockSpec((1, TOK_TILE, LANES), lambda i, h, t: (i, t, h))
    return pl.pallas_call(
        functools.partial(_na_attn_kernel, n_rows),
        grid=(b, w // LANES, s // TOK_TILE),
        in_specs=[tile, seq, seq,
                  pl.BlockSpec((1,) + bias.shape[1:], lambda i, h, t: (h, 0, 0, 0))],
        out_specs=tile,
        out_shape=jax.ShapeDtypeStruct((b, s, w), BF16),
        compiler_params=_cparams(("parallel", "parallel", "parallel")),
        name="na_attn",
    )(q, k, v, bias)


def _pool_kernel(s_total, u_ref, w_ref, scale_ref, o_ref):
    t = pl.program_id(1)
    t0 = t * TOK_TILE
    seg_lo = jnp.where(t == 0, 0, TOK_TILE)
    seg_hi = jnp.where(t == 0, TOK_TILE, s_total)
    span = TOK_TILE + 2 * POOL_HALO
    start = pl.multiple_of(jnp.clip(t0 - POOL_HALO, 0, s_total - span), SUBLANES)
    hi, lo = _split(u_ref[0, pl.ds(start, span), :])
    own = u_ref[0, pl.ds(pl.multiple_of(t0, TOK_TILE), TOK_TILE), :]
    tok_q = t0 + lax.broadcasted_iota(jnp.int32, (TOK_TILE, span), 0)
    tok_k = start + lax.broadcasted_iota(jnp.int32, (TOK_TILE, span), 1)
    group = _lane_iota(own.shape) // HEAD_DIM
    mean = jnp.zeros(own.shape, F32)
    for g, win in enumerate(POOL_WINDOWS):
        lo_t = jnp.maximum(tok_q - win // 2, seg_lo)
        hi_t = jnp.minimum(tok_q + win // 2, seg_hi)
        band = jnp.where(tok_k >= lo_t, jnp.where(tok_k < hi_t, 1.0, 0.0), 0.0).astype(BF16)
        count = (hi_t - lo_t)[:, 0:1].astype(F32)
        total = _dot(band, hi) + _dot(band, lo)
        mean = jnp.where(group == g, total / count, mean)
    p = (mean - own).astype(BF16)
    o_ref[0] = (_dot(p, w_ref[...]) * scale_ref[...]).astype(o_ref.dtype)


def _pool_call(u, w_blockdiag, scale):
    b, s, w = u.shape
    return pl.pallas_call(
        functools.partial(_pool_kernel, s),
        grid=(b, s // TOK_TILE),
        in_specs=[pl.BlockSpec((1, s, w), lambda i, t: (i, 0, 0)),
                  pl.BlockSpec((w, w), lambda i, t: (0, 0)),
                  pl.BlockSpec((1, w), lambda i, t: (0, 0))],
        out_specs=pl.BlockSpec((1, TOK_TILE, w), lambda i, t: (i, t, 0)),
        out_shape=jax.ShapeDtypeStruct((b, s, w), BF16),
        compiler_params=_cparams(("parallel", "parallel")),
        name="pool_mix",
    )(u, w_blockdiag, scale.reshape(1, w))


def _outproj_kernel(n_parts, *refs):
    y_refs = refs[:n_parts]
    w_refs = refs[n_parts:2 * n_parts]
    x_ref, mod_ref, norm_ref, x1_ref, hi_ref, lo_ref = refs[2 * n_parts:]
    acc = _dot(y_refs[0][0], w_refs[0][...])
    for y_ref, w_ref in zip(y_refs[1:], w_refs[1:]):
        acc = acc + _dot(y_ref[0], w_ref[...])
    x1 = x_ref[0] + mod_ref[0, 0, 2:3, :] * acc
    x1_ref[0] = x1
    xm = _rms(x1, norm_ref[...]) * (1.0 + mod_ref[0, 0, 4:5, :]) + mod_ref[0, 0, 3:4, :]
    hi, lo = _split(xm)
    hi_ref[0] = hi
    lo_ref[0] = lo


def _outproj_call(parts, weights, x, mods, norm_g):
    b, s, d = x.shape
    tok = lambda w: pl.BlockSpec((1, TOK_TILE, w), lambda i, t: (i, t, 0))
    specs = [tok(p.shape[-1]) for p in parts]
    specs += [pl.BlockSpec(w.shape, lambda i, t: (0, 0)) for w in weights]
    specs += [tok(d),
              pl.BlockSpec((1, 1, N_MOD, d), lambda i, t: (i, jnp.minimum(t, 1), 0, 0)),
              pl.BlockSpec((1, d), lambda i, t: (0, 0))]
    return pl.pallas_call(
        functools.partial(_outproj_kernel, len(parts)),
        grid=(b, s // TOK_TILE),
        in_specs=specs,
        out_specs=[tok(d), tok(d), tok(d)],
        out_shape=[jax.ShapeDtypeStruct((b, s, d), F32), jax.ShapeDtypeStruct((b, s, d), BF16),
                   jax.ShapeDtypeStruct((b, s, d), BF16)],
        compiler_params=_cparams(("parallel", "parallel")),
        name="out_proj",
    )(*parts, *weights, x, mods, norm_g.reshape(1, d))


def _peer_fold_kernel(k_ref, wt_ref, hi_ref, lo_ref):
    k_hi, k_lo = _split(k_ref[0])
    w_hi, w_lo = _split(wt_ref[0])
    kw = _dot3(k_hi, k_lo, w_hi, w_lo)
    hi, lo = _split(kw)
    hi_ref[0] = hi
    lo_ref[0] = lo


def _peer_fold_call(keys_ph, wq_t):
    n, nk, kd = keys_ph.shape
    d = wq_t.shape[-1]
    blk = pl.BlockSpec((1, nk, d), lambda i: (i, 0, 0))
    return pl.pallas_call(
        _peer_fold_kernel,
        grid=(n,),
        in_specs=[pl.BlockSpec((1, nk, kd), lambda i: (i, 0, 0)), pl.BlockSpec((1, kd, d), lambda i: (i, 0, 0))],
        out_specs=[blk, blk],
        out_shape=[jax.ShapeDtypeStruct((n, nk, d), BF16)] * 2,
        compiler_params=_cparams(("parallel",)),
        name="peer_fold",
    )(keys_ph, wq_t)


def _peer_scores_kernel(kw_hi_ref, kw_lo_ref, x_hi_ref, x_lo_ref, st_ref):
    st_ref[...] = _dot3(kw_hi_ref[...], kw_lo_ref[...], x_hi_ref[...], x_lo_ref[...], _NT)


def _peer_scores_call(kw_hi, kw_lo, x_hi, x_lo, tm):
    n, d = x_hi.shape
    r = kw_hi.shape[0]
    kw = pl.BlockSpec((r, d), lambda t: (0, 0))
    xs = pl.BlockSpec((tm, d), lambda t: (t, 0))
    return pl.pallas_call(
        _peer_scores_kernel,
        grid=(n // tm,),
        in_specs=[kw, kw, xs, xs],
        out_specs=pl.BlockSpec((r, tm), lambda t: (0, t)),
        out_shape=jax.ShapeDtypeStruct((r, n), F32),
        compiler_params=_cparams(("parallel",)),
        name="peer_scores",
    )(kw_hi, kw_lo, x_hi, x_lo)


def _bitonic_merge_desc(v):
    n = len(v)
    if n == 1:
        return v
    half = n // 2
    top = [jnp.maximum(v[i], v[i + half]) for i in range(half)]
    bot = [jnp.minimum(v[i], v[i + half]) for i in range(half)]
    return _bitonic_merge_desc(top) + _bitonic_merge_desc(bot)


def _sort_desc(v):
    n = len(v)
    if n == 1:
        return v
    return _bitonic_merge_desc(_sort_desc(v[:n // 2]) + _sort_desc(v[n // 2:])[::-1])


def _merge_top(a, b):
    n = len(a)
    return _bitonic_merge_desc([jnp.maximum(a[i], b[n - 1 - i]) for i in range(n)])


def _peer_topk_kernel(st_ref, tau_ref, cz_ref):
    k = PEER_TOPK
    half_rows = PEER_HEADS * PEER_N_KEYS
    tops = []
    for p in range(2):
        groups = []
        for g in range(PEER_N_KEYS // k):
            vals = [st_ref[pl.ds(p * half_rows + g * k + j, PEER_HEADS, stride=PEER_N_KEYS), :]
                    for j in range(k)]
            groups.append(_sort_desc(vals))
        while len(groups) > 1:
            groups = [_merge_top(groups[i], groups[i + 1]) for i in range(0, len(groups), 2)]
        tops.append(groups[0])
    t1, t2 = tops
    neg = jnp.full(t1[0].shape, NEG, F32)
    rows = [[t1[i] + t2[j] for j in range(k // (i + 1))] for i in range(k)]
    first = _merge_top(rows[0], rows[1] + [neg] * (k - len(rows[1])))
    rest = [c for row in rows[2:] for c in row]
    rest = _sort_desc(rest + [neg] * (2 * k - len(rest)))[:k]
    top = [jnp.maximum(first[i], rest[k - 1 - i]) for i in range(k)]
    tau = functools.reduce(jnp.minimum, top)
    m = rows[0][0]
    z = functools.reduce(lambda a, c: a + c, [jnp.exp(c - m) for c in top])
    tau_ref[...] = tau
    cz_ref[...] = m + jnp.log(z)


def _peer_topk_call(st):
    r, n = st.shape
    out = pl.BlockSpec((PEER_HEADS, LANES), lambda t: (0, t))
    return pl.pallas_call(
        _peer_topk_kernel,
        grid=(n // LANES,),
        in_specs=[pl.BlockSpec((r, LANES), lambda t: (0, t))],
        out_specs=[out, out],
        out_shape=[jax.ShapeDtypeStruct((PEER_HEADS, n), F32)] * 2,
        compiler_params=_cparams(("parallel",)),
        name="peer_topk",
    )(st)


def _peer_dense_kernel(rows_per_chunk, x_ref, st_ref, tau_ref, cz_ref, u_ref, vt_ref, y_ref, acc_ref, g_ref):
    c = pl.program_id(1)
    tm = x_ref.shape[0]
    nk = PEER_N_KEYS
    half_rows = PEER_HEADS * nk

    @pl.when(c == 0)
    def _():
        acc_ref[...] = jnp.zeros(acc_ref.shape, F32)

    ht = _dot(u_ref[...], x_ref[...], _NT)
    for lt in range(tm // LANES):
        cols = slice(lt * LANES, (lt + 1) * LANES)
        s1_rows = [st_ref[pl.ds(pl.multiple_of(h * nk + c * SUBLANES, SUBLANES), SUBLANES), cols]
                   for h in range(PEER_HEADS)]
        for r in range(rows_per_chunk):
            w = jnp.zeros((nk, LANES), F32)
            for h in range(PEER_HEADS):
                s2 = st_ref[half_rows + h * nk:half_rows + (h + 1) * nk, cols]
                ssum = s1_rows[h][r:r + 1, :] + s2
                gate = jnp.exp(ssum - cz_ref[h:h + 1, cols])
                w = w + jnp.where(ssum >= tau_ref[h:h + 1, cols], gate, 0.0)
            hv = ht[r * nk:(r + 1) * nk, cols]
            act = 0.5 * hv * (1.0 + lax.erf(hv * SQRT_HALF))
            g_ref[r * nk:(r + 1) * nk, cols] = (w * act).astype(BF16)
    acc_ref[...] += _dot(vt_ref[...], g_ref[...])

    @pl.when(c == pl.num_programs(1) - 1)
    def _():
        y_ref[...] = acc_ref[...].T


def _peer_dense_call(x_hi, st, tau, cz, u_bf, vt_bf, tm, rows_per_chunk):
    n, d = x_hi.shape
    n_exp = u_bf.shape[0]
    ne = rows_per_chunk * PEER_N_KEYS
    return pl.pallas_call(
        functools.partial(_peer_dense_kernel, rows_per_chunk),
        grid=(n // tm, n_exp // ne),
        in_specs=[
            pl.BlockSpec((tm, d), lambda t, c: (t, 0)),
            pl.BlockSpec((st.shape[0], tm), lambda t, c: (0, t)),
            pl.BlockSpec((PEER_HEADS, tm), lambda t, c: (0, t)),
            pl.BlockSpec((PEER_HEADS, tm), lambda t, c: (0, t)),
            pl.BlockSpec((ne, d), lambda t, c: (c, 0)),
            pl.BlockSpec((d, ne), lambda t, c: (0, c)),
        ],
        out_specs=pl.BlockSpec((tm, d), lambda t, c: (t, 0)),
        out_shape=jax.ShapeDtypeStruct((n, d), F32),
        scratch_shapes=[pltpu.VMEM((d, tm), F32), pltpu.VMEM((ne, tm), BF16)],
        compiler_params=_cparams(("parallel", "arbitrary")),
        name="peer_dense",
    )(x_hi, st, tau, cz, u_bf, vt_bf)


def _peer(x_hi, x_lo, wq, keys, u, v):
    b, s, d = x_hi.shape
    n = b * s
    heads, _, nk, kd = keys.shape
    keys_ph = keys.transpose(1, 0, 2, 3).reshape(2 * heads, nk, kd)
    wq_t = wq.T.reshape(heads, 2, kd, d).transpose(1, 0, 2, 3).reshape(2 * heads, kd, d)
    kw_hi, kw_lo = _peer_fold_call(keys_ph, wq_t)
    kw_hi = kw_hi.reshape(2 * heads * nk, d)
    kw_lo = kw_lo.reshape(2 * heads * nk, d)
    tm = 512 if n % 512 == 0 else TOK_TILE
    xh = x_hi.reshape(n, d)
    st = _peer_scores_call(kw_hi, kw_lo, xh, x_lo.reshape(n, d), tm)
    tau, cz = _peer_topk_call(st)
    y = _peer_dense_call(xh, st, tau, cz, u.astype(BF16), v.T.astype(BF16), tm, SUBLANES)
    return y.reshape(b, s, d)


def _final_kernel(x_ref, y_ref, mod_ref, o_ref):
    o_ref[0] = x_ref[0] + mod_ref[0, 0, 5:6, :] * y_ref[0]


def _final_call(x1, y, mods):
    b, s, d = x1.shape
    nt = s // TOK_TILE - 1
    src = pl.BlockSpec((1, TOK_TILE, d), lambda i, t: (i, t + 1, 0))
    return pl.pallas_call(
        _final_kernel,
        grid=(b, nt),
        in_specs=[src, src, pl.BlockSpec((1, 1, N_MOD, d), lambda i, t: (i, 1, 0, 0))],
        out_specs=pl.BlockSpec((1, TOK_TILE, d), lambda i, t: (i, t, 0)),
        out_shape=jax.ShapeDtypeStruct((b, nt * TOK_TILE, d), F32),
        compiler_params=_cparams(("parallel", "parallel")),
        name="final_residual",
    )(x1, y, mods)


def _gain2(g):
    return jnp.concatenate([g, g]).reshape(1, LANES).astype(F32)


def kernel(x, c, ctx, c_ctx, l0_ada_w, l0_ada_b, l0_norm1, l0_norm2, l0_w_in, l0_w_out, l0_pool_w, l0_pool_scale, l0_diff_qnorm, l0_diff_knorm, l0_lambda_q1, l0_lambda_k1, l0_lambda_q2, l0_lambda_k2, l0_diff_subln, l0_peer_wq, l0_peer_keys, l0_peer_u, l0_peer_v, l1_ada_w, l1_ada_b, l1_norm1, l1_norm2, l1_w_in, l1_w_out, l1_gqa_qnorm, l1_gqa_knorm, l1_na_qnorm, l1_na_knorm, l1_na_rpb, l1_peer_wq, l1_peer_keys, l1_peer_u, l1_peer_v):
    b, t_lat, d = x.shape
    ctx_len = ctx.shape[1]
    assert ctx_len == TOK_TILE and t_lat % KV_CHUNK == 0 and t_lat // GRID_W >= NA_WIN_ROWS
    s = ctx_len + t_lat
    xs = jnp.concatenate([ctx, x], axis=1)
    cos, sin = _rope_tables(s, ctx_len)
    qk_scale = HEAD_DIM ** -0.5

    mods0 = _modulation(c, c_ctx, l0_ada_w, l0_ada_b)
    w0 = l0_w_in.astype(BF16)
    pw = l0_pool_scale.shape[0]
    dw = DIFF_HEADS * 2 * HEAD_DIM
    roles0 = [("plain_f32", pw, 1.0), ("norm_rope", dw, qk_scale), ("norm_rope", dw, 1.0), ("plain", dw, 1.0)]
    weights0 = [w0[:, :pw], w0[:, pw:pw + dw], w0[:, pw + dw:pw + 2 * dw], w0[:, pw + 2 * dw:]]
    gains0 = [None, _gain2(l0_diff_qnorm), _gain2(l0_diff_knorm), None]
    u0, q0, k0, v0 = _inproj_call(xs, mods0, l0_norm1, cos, sin, roles0, weights0, gains0)
    lam_init = 0.8 - 0.6 * math.exp(-0.3 * 0)
    lam_vecs = jnp.stack([l0_lambda_q1, l0_lambda_k1, l0_lambda_q2, l0_lambda_k2]).astype(F32)
    o_diff = _diff_attn_call(q0, k0, v0, lam_vecs, l0_diff_subln, lam_init)
    pool_bd = jax.scipy.linalg.block_diag(*[l0_pool_w[g] for g in range(l0_pool_w.shape[0])]).astype(BF16)
    y_pool = _pool_call(u0, pool_bd, l0_pool_scale)
    wo0 = l0_w_out.astype(BF16)
    x1, xh, xl = _outproj_call([y_pool, o_diff], [wo0[:pw], wo0[pw:]], xs, mods0, l0_norm2)
    y_peer0 = _peer(xh, xl, l0_peer_wq, l0_peer_keys, l0_peer_u, l0_peer_v)

    mods1 = _modulation(c, c_ctx, l1_ada_w, l1_ada_b)
    w1 = l1_w_in.astype(BF16)
    n_q = 8 * HEAD_DIM
    n_kv = GQA_KV_HEADS * HEAD_DIM
    o_ck, o_cv, o_nq, o_nk, o_nv = n_q, n_q + n_kv, n_q + 2 * n_kv, 2 * n_q + 2 * n_kv, 3 * n_q + 2 * n_kv
    dup = lambda w: jnp.concatenate([w[:, :HEAD_DIM], w[:, :HEAD_DIM], w[:, HEAD_DIM:], w[:, HEAD_DIM:]], axis=1)
    roles1 = [("norm_rope", n_q, qk_scale), ("norm_rope", 2 * n_kv, 1.0), ("plain", 2 * n_kv, 1.0),
              ("norm", n_q, qk_scale), ("norm", n_q, 1.0), ("plain", n_q, 1.0)]
    weights1 = [w1[:, :o_ck], dup(w1[:, o_ck:o_cv]), dup(w1[:, o_cv:o_nq]),
                w1[:, o_nq:o_nk], w1[:, o_nk:o_nv], w1[:, o_nv:]]
    gains1 = [_gain2(l1_gqa_qnorm), _gain2(l1_gqa_knorm), None, _gain2(l1_na_qnorm), _gain2(l1_na_knorm), None]
    x2, cq, ckd, cvd, nq, nk_, nv = _inproj_call(x1, mods1, l1_norm1, cos, sin, roles1, weights1, gains1,
                                                 residual=(y_peer0, mods0))
    o_gqa = _gqa_attn_call(cq, ckd, cvd)
    bias = _na_bias_table(l1_na_rpb, t_lat // GRID_W)
    o_na = _na_attn_call(nq, nk_, nv, bias)
    wo1 = l1_w_out.astype(BF16)
    x3, xh1, xl1 = _outproj_call([o_gqa, o_na], [wo1[:n_q], wo1[n_q:]], x2, mods1, l1_norm2)
    y_peer1 = _peer(xh1, xl1, l1_peer_wq, l1_peer_keys, l1_peer_u, l1_peer_v)
    return _final_call(x3, y_peer1, mods1)
```

```python
import functools
import math

import jax
import jax.numpy as jnp
from jax import lax
from jax.experimental import pallas as pl
from jax.experimental.pallas import tpu as pltpu

F32 = jnp.float32
BF16 = jnp.bfloat16

LANES = 128
SUBLANES = 8
VMEM_LIMIT_BYTES = 56 * 1024 * 1024

HEAD_DIM = 64
GRID_W = 64
ROPE_THETA = 10000.0
EPS = 1e-6
N_MOD = 6
POOL_WINDOWS = (2, 4, 8, 16)
POOL_HALO = max(POOL_WINDOWS) // 2
DIFF_HEADS = 6
GQA_KV_HEADS = 2
NA_WIN_ROWS = 8
NA_WIN_COLS = 16
PEER_HEADS = 8
PEER_N_KEYS = 128
PEER_TOPK = 16
TOK_TILE = 256
KV_CHUNK = 512
NEG = -1e30
SQRT_HALF = 0.7071067811865476


def _cparams(sem, vmem=VMEM_LIMIT_BYTES):
    return pltpu.CompilerParams(dimension_semantics=sem, vmem_limit_bytes=vmem)


def _split(x):
    hi = x.astype(BF16)
    lo = (x - hi.astype(F32)).astype(BF16)
    return hi, lo


_NN = (((1,), (0,)), ((), ()))
_NT = (((1,), (1,)), ((), ()))


def _dot(a, b, dims=_NN):
    return lax.dot_general(a, b, dims, preferred_element_type=F32)


def _dot3(a_hi, a_lo, b_hi, b_lo, dims=_NN):
    return _dot(a_hi, b_hi, dims) + _dot(a_hi, b_lo, dims) + _dot(a_lo, b_hi, dims)


def _lane_iota(shape):
    return lax.broadcasted_iota(jnp.int32, shape, len(shape) - 1)


def _rms(x, gain):
    ms = jnp.mean(x * x, axis=-1, keepdims=True)
    return x * lax.rsqrt(ms + EPS) * gain


def _ada_kernel(c_ref, w_ref, b_ref, o_ref):
    c = c_ref[...]
    a = c / (1.0 + jnp.exp(-c))
    a_hi, a_lo = _split(a)
    w_hi, w_lo = _split(w_ref[...])
    o_ref[...] = _dot3(a_hi, a_lo, w_hi, w_lo) + b_ref[...]


def _ada_call(cvec, ada_w, ada_b):
    rows, d = cvec.shape
    n = ada_w.shape[1]
    bn = 768
    return pl.pallas_call(
        _ada_kernel,
        grid=(n // bn,),
        in_specs=[
            pl.BlockSpec((rows, d), lambda j: (0, 0)),
            pl.BlockSpec((d, bn), lambda j: (0, j)),
            pl.BlockSpec((1, bn), lambda j: (0, j)),
        ],
        out_specs=pl.BlockSpec((rows, bn), lambda j: (0, j)),
        out_shape=jax.ShapeDtypeStruct((rows, n), F32),
        compiler_params=_cparams(("parallel",)),
        name="ada_mod",
    )(cvec, ada_w, ada_b.reshape(1, n))


def _modulation(c, c_ctx, ada_w, ada_b):
    b, d = c.shape
    rows = -(-(b + 1) // 16) * 16
    cvec = jnp.zeros((rows, d), F32).at[0].set(c_ctx).at[1:b + 1].set(c)
    m = _ada_call(cvec, ada_w, ada_b).reshape(rows, N_MOD, d)
    return jnp.stack([jnp.broadcast_to(m[0], (b, N_MOD, d)), m[1:b + 1]], axis=1)


def _head_norm(y, gain):
    r = lax.broadcasted_iota(jnp.int32, (LANES, LANES), 0) // HEAD_DIM
    c = lax.broadcasted_iota(jnp.int32, (LANES, LANES), 1) // HEAD_DIM
    ones_bd = jnp.where(r == c, 1.0, 0.0).astype(BF16)
    hi, lo = _split(y * y)
    ss = _dot(hi, ones_bd) + _dot(lo, ones_bd)
    return y * lax.rsqrt(ss * (1.0 / HEAD_DIM) + EPS) * gain


def _rope(y, cos, sin):
    up = pltpu.roll(y, LANES - 16, 1)
    down = pltpu.roll(y, 16, 1)
    partner = jnp.where((_lane_iota(y.shape) & 16) == 0, up, down)
    return y * cos + partner * sin


def _inproj_kernel(roles, fuse_residual, *refs):
    refs = list(refs)
    x_ref = refs.pop(0)
    if fuse_residual:
        y_ref = refs.pop(0)
        pmod_ref = refs.pop(0)
    mod_ref = refs.pop(0)
    norm_ref = refs.pop(0)
    cos_ref = refs.pop(0)
    sin_ref = refs.pop(0)
    w_refs, g_refs = [], []
    for kind, _, _, _ in roles:
        w_refs.append(refs.pop(0))
        g_refs.append(refs.pop(0) if kind in ("norm", "norm_rope") else None)
    if fuse_residual:
        xo_ref = refs.pop(0)
    out_refs = refs

    x = x_ref[0]
    if fuse_residual:
        x = x + pmod_ref[0, 0, 5:6, :] * y_ref[0]
        xo_ref[0] = x
    xm = _rms(x, norm_ref[...]) * (1.0 + mod_ref[0, 0, 1:2, :]) + mod_ref[0, 0, 0:1, :]
    xm = xm.astype(BF16)
    cos = cos_ref[...]
    sin = sin_ref[...]
    for (kind, width, scale, transposed), w_ref, g_ref, o_ref in zip(roles, w_refs, g_refs, out_refs):
        acc = _dot(xm, w_ref[...])
        if kind in ("plain", "plain_f32") and not transposed:
            o_ref[0] = acc.astype(o_ref.dtype)
            continue
        for j in range(width // LANES):
            y = acc[:, j * LANES:(j + 1) * LANES]
            if kind in ("norm", "norm_rope"):
                y = _head_norm(y, g_ref[...])
            if kind == "norm_rope":
                y = _rope(y, cos, sin)
            if scale != 1.0:
                y = y * scale
            if transposed:
                o_ref[0, j * LANES:(j + 1) * LANES, :] = y.T.astype(o_ref.dtype)
            else:
                o_ref[0, :, j * LANES:(j + 1) * LANES] = y.astype(o_ref.dtype)


def _inproj_call(x, mods, norm_g, cos, sin, roles, weights, gains, residual=None):
    b, s, d = x.shape
    nt = s // TOK_TILE
    tok = lambda w: pl.BlockSpec((1, TOK_TILE, w), lambda i, t: (i, t, 0))
    mod_spec = pl.BlockSpec((1, 1, N_MOD, d), lambda i, t: (i, jnp.minimum(t, 1), 0, 0))
    args, specs = [x], [tok(d)]
    if residual is not None:
        y, pmods = residual
        args += [y, pmods]
        specs += [tok(d), mod_spec]
    args += [mods, norm_g.reshape(1, d), cos, sin]
    specs += [mod_spec, pl.BlockSpec((1, d), lambda i, t: (0, 0)),
              pl.BlockSpec((TOK_TILE, LANES), lambda i, t: (t, 0)),
              pl.BlockSpec((TOK_TILE, LANES), lambda i, t: (t, 0))]
    for (kind, width, _, _), w, g in zip(roles, weights, gains):
        args.append(w)
        specs.append(pl.BlockSpec((d, width), lambda i, t: (0, 0)))
        if kind in ("norm", "norm_rope"):
            args.append(g)
            specs.append(pl.BlockSpec((1, LANES), lambda i, t: (0, 0)))
    out_shapes, out_specs = [], []
    if residual is not None:
        out_shapes.append(jax.ShapeDtypeStruct((b, s, d), F32))
        out_specs.append(tok(d))
    for kind, width, _, transposed in roles:
        dt = F32 if kind == "plain_f32" else BF16
        if transposed:
            out_shapes.append(jax.ShapeDtypeStruct((b, width, s), dt))
            out_specs.append(pl.BlockSpec((1, width, TOK_TILE), lambda i, t: (i, 0, t)))
        else:
            out_shapes.append(jax.ShapeDtypeStruct((b, s, width), dt))
            out_specs.append(tok(width))
    return pl.pallas_call(
        functools.partial(_inproj_kernel, tuple(roles), residual is not None),
        grid=(b, nt),
        in_specs=specs,
        out_specs=out_specs,
        out_shape=out_shapes,
        compiler_params=_cparams(("parallel", "parallel")),
        name="in_proj",
    )(*args)


def _rope_tables(s_total, ctx_len):
    t = jnp.arange(s_total - ctx_len, dtype=jnp.int32)
    pos = jnp.stack([t // GRID_W, t % GRID_W], axis=-1).astype(F32)
    n_freq = HEAD_DIM // 4
    inv_freq = ROPE_THETA ** (-jnp.arange(n_freq, dtype=F32) / n_freq)
    ang = pos[:, :, None] * inv_freq
    cos, sin = jnp.cos(ang), jnp.sin(ang)
    cos64 = jnp.concatenate([cos[:, 0], cos[:, 0], cos[:, 1], cos[:, 1]], axis=-1)
    sin64 = jnp.concatenate([-sin[:, 0], sin[:, 0], -sin[:, 1], sin[:, 1]], axis=-1)
    cos128 = jnp.concatenate([jnp.ones((ctx_len, LANES), F32), jnp.tile(cos64, (1, 2))], axis=0)
    sin128 = jnp.concatenate([jnp.zeros((ctx_len, LANES), F32), jnp.tile(sin64, (1, 2))], axis=0)
    return cos128, sin128


def _attend_all(qt, k_ref, vt_ref, is_latent, n_lat_chunks, acc_ref):
    r = qt.shape[1]
    acc_ref[...] = jnp.zeros(acc_ref.shape, F32)

    def step(kc, vtc, m_prev, l_prev):
        s = _dot(kc, qt)
        m_new = jnp.maximum(m_prev, jnp.max(s, axis=0, keepdims=True))
        alpha = jnp.exp(m_prev - m_new)
        p = jnp.exp(s - m_new)
        acc_ref[...] = alpha * acc_ref[...] + _dot(vtc, p.astype(BF16))
        return m_new, alpha * l_prev + jnp.sum(p, axis=0, keepdims=True)

    m, l = step(k_ref[0, 0:TOK_TILE, :], vt_ref[0, :, 0:TOK_TILE],
                jnp.full((1, r), NEG, F32), jnp.zeros((1, r), F32))

    def body(j, carry):
        off = pl.multiple_of(TOK_TILE + j * KV_CHUNK, TOK_TILE)
        return step(k_ref[0, pl.ds(off, KV_CHUNK), :], vt_ref[0, :, pl.ds(off, KV_CHUNK)], *carry)

    m, l = lax.fori_loop(0, jnp.where(is_latent, n_lat_chunks, 0), body, (m, l))
    return acc_ref[...] / l


def _half_masks(q):
    lo = _lane_iota(q.shape) < HEAD_DIM
    zero = jnp.zeros_like(q)
    return jnp.where(lo, q, zero), jnp.where(lo, zero, q)


def _row_half_masks(qt):
    top = lax.broadcasted_iota(jnp.int32, qt.shape, 0) < HEAD_DIM
    zero = jnp.zeros_like(qt)
    return jnp.where(top, qt, zero), jnp.where(top, zero, qt)


def _diff_attn_kernel(n_lat_chunks, lam_init, qt_ref, k_ref, vt_ref, lam_ref, subln_ref, o_ref, acc_ref):
    q1, q2 = _row_half_masks(qt_ref[0])
    qt = jnp.concatenate([q1, q2], axis=1)
    o = _attend_all(qt, k_ref, vt_ref, pl.program_id(2) > 0, n_lat_chunks, acc_ref)
    lv = lam_ref[...]
    lam = (jnp.exp(jnp.sum(lv[0:1] * lv[1:2], axis=-1, keepdims=True))
           - jnp.exp(jnp.sum(lv[2:3] * lv[3:4], axis=-1, keepdims=True)) + lam_init)
    od = o[:, :TOK_TILE] - lam * o[:, TOK_TILE:]
    ms = jnp.mean(od * od, axis=0, keepdims=True)
    on = od * lax.rsqrt(ms + EPS) * subln_ref[...] * (1.0 - lam_init)
    o_ref[0] = on.T.astype(o_ref.dtype)


def _diff_attn_call(qt, k, vt, lam_vecs, subln, lam_init):
    b, s, w = k.shape
    heads = w // LANES
    n_lat_chunks = (s - TOK_TILE) // KV_CHUNK
    return pl.pallas_call(
        functools.partial(_diff_attn_kernel, n_lat_chunks, lam_init),
        grid=(b, heads, s // TOK_TILE),
        in_specs=[
            pl.BlockSpec((1, LANES, TOK_TILE), lambda i, h, t: (i, h, t)),
            pl.BlockSpec((1, s, LANES), lambda i, h, t: (i, 0, h)),
            pl.BlockSpec((1, LANES, s), lambda i, h, t: (i, h, 0)),
            pl.BlockSpec((4, HEAD_DIM), lambda i, h, t: (0, 0)),
            pl.BlockSpec((LANES, 1), lambda i, h, t: (0, 0)),
        ],
        out_specs=pl.BlockSpec((1, TOK_TILE, LANES), lambda i, h, t: (i, t, h)),
        out_shape=jax.ShapeDtypeStruct((b, s, w), BF16),
        scratch_shapes=[pltpu.VMEM((LANES, 2 * TOK_TILE), F32)],
        compiler_params=_cparams(("parallel", "parallel", "parallel")),
        name="diff_attn",
    )(qt, k, vt, lam_vecs, subln.reshape(LANES, 1))


def _gqa_attn_kernel(n_lat_chunks, qt_ref, k_ref, vt_ref, o_ref, acc_ref):
    qa = _row_half_masks(qt_ref[0, 0:LANES, :])
    qb = _row_half_masks(qt_ref[0, LANES:2 * LANES, :])
    qt = jnp.concatenate([qa[0], qa[1], qb[0], qb[1]], axis=1)
    o = _attend_all(qt, k_ref, vt_ref, pl.program_id(2) > 0, n_lat_chunks, acc_ref)
    t = TOK_TILE
    ot = jnp.concatenate([o[:, j * t:(j + 1) * t] for j in range(4)], axis=0)
    o_ref[0] = ot.T.astype(o_ref.dtype)


def _gqa_attn_call(qt, k_dup, vt):
    b, w, s = qt.shape
    groups = w // (2 * LANES)
    n_lat_chunks = (s - TOK_TILE) // KV_CHUNK
    return pl.pallas_call(
        functools.partial(_gqa_attn_kernel, n_lat_chunks),
        grid=(b, groups, s // TOK_TILE),
        in_specs=[pl.BlockSpec((1, 2 * LANES, TOK_TILE), lambda i, g, t: (i, g, t)),
                  pl.BlockSpec((1, s, LANES), lambda i, g, t: (i, 0, g)),
                  pl.BlockSpec((1, HEAD_DIM, s), lambda i, g, t: (i, g, 0))],
        out_specs=pl.BlockSpec((1, TOK_TILE, 2 * LANES), lambda i, g, t: (i, t, g)),
        out_shape=jax.ShapeDtypeStruct((b, s, w), BF16),
        scratch_shapes=[pltpu.VMEM((HEAD_DIM, 4 * TOK_TILE), F32)],
        compiler_params=_cparams(("parallel", "parallel", "parallel")),
        name="gqa_attn",
    )(qt, k_dup, vt)


def _na_attn_kernel(n_rows, q_ref, k_ref, v_ref, bias_ref, o_ref):
    t = pl.program_id(2)

    @pl.when(t == 0)
    def _():
        o_ref[0] = jnp.zeros(o_ref.shape[1:], o_ref.dtype)

    @pl.when(t > 0)
    def _():
        kctx = k_ref[0, 0:TOK_TILE, :]
        vctx = v_ref[0, 0:TOK_TILE, :]
        win = NA_WIN_ROWS * GRID_W
        rows_per_tile = TOK_TILE // GRID_W
        lo = _lane_iota((GRID_W, LANES)) < HEAD_DIM
        for i in range(rows_per_tile):
            r = (t - 1) * rows_per_tile + i
            r0 = jnp.clip(r - NA_WIN_ROWS // 2, 0, n_rows - NA_WIN_ROWS)
            off = pl.multiple_of(TOK_TILE + r0 * GRID_W, GRID_W)
            q1, q2 = _half_masks(q_ref[0, i * GRID_W:(i + 1) * GRID_W, :])
            q = jnp.concatenate([q1, q2], axis=0)
            s_nb = _dot(q, k_ref[0, pl.ds(off, win), :], _NT) + bias_ref[0, r - r0]
            s_cx = _dot(q, kctx, _NT)
            m = jnp.maximum(jnp.max(s_nb, axis=-1, keepdims=True), jnp.max(s_cx, axis=-1, keepdims=True))
            p_nb = jnp.exp(s_nb - m)
            p_cx = jnp.exp(s_cx - m)
            den = jnp.sum(p_nb, axis=-1, keepdims=True) + jnp.sum(p_cx, axis=-1, keepdims=True)
            o = (_dot(p_nb.astype(BF16), v_ref[0, pl.ds(off, win), :]) + _dot(p_cx.astype(BF16), vctx)) / den
            o_ref[0, i * GRID_W:(i + 1) * GRID_W, :] = jnp.where(lo, o[:GRID_W], o[GRID_W:]).astype(o_ref.dtype)


def _na_bias_table(rpb, n_rows):
    kr = NA_WIN_ROWS
    cols = jnp.arange(GRID_W, dtype=jnp.int32)
    c0 = jnp.clip(cols - NA_WIN_COLS // 2, 0, GRID_W - NA_WIN_COLS)
    kc = jnp.arange(GRID_W, dtype=jnp.int32)
    inside = (kc[None, :] >= c0[:, None]) & (kc[None, :] < c0[:, None] + NA_WIN_COLS)
    dc = kc[None, :] - cols[:, None] + (NA_WIN_COLS - 1)
    onehot = ((dc[:, :, None] == jnp.arange(2 * NA_WIN_COLS - 1)) & inside[:, :, None]).astype(F32)
    by_col = jnp.einsum("hrd,ckd->hrck", rpb.astype(F32), onehot, precision=lax.Precision.HIGHEST)
    by_col = jnp.where(inside[None, None], by_col, NEG)
    tab = jnp.stack([by_col[:, NA_WIN_ROWS - 1 - var:2 * NA_WIN_ROWS - 1 - var]
                     for var in range(NA_WIN_ROWS)], axis=1)
    h = rpb.shape[0]
    tab = tab.transpose(0, 1, 3, 2, 4).reshape(h // 2, 2, NA_WIN_ROWS, GRID_W, kr * GRID_W)
    return tab.transpose(0, 2, 1, 3, 4).reshape(h // 2, NA_WIN_ROWS, 2 * GRID_W, kr * GRID_W)


def _na_attn_call(q, k, v, bias):
    b, s, w = q.shape
    n_rows = (s - TOK_TILE) // GRID_W
    seq = pl.BlockSpec((1, s, LANES), lambda i, h, t: (i, 0, h))
    tile = pl.BlockSpec((1, TOK_TILE, LANES), lambda i, h, t: (i, t, h))
    return pl.pallas_call(
        functools.partial(_na_attn_kernel, n_rows),
        grid=(b, w // LANES, s // TOK_TILE),
        in_specs=[tile, seq, seq,
                  pl.BlockSpec((1,) + bias.shape[1:], lambda i, h, t: (h, 0, 0, 0))],
        out_specs=tile,
        out_shape=jax.ShapeDtypeStruct((b, s, w), BF16),
        compiler_params=_cparams(("parallel", "parallel", "parallel")),
        name="na_attn",
    )(q, k, v, bias)


def _pool_kernel(s_total, u_ref, w_ref, scale_ref, o_ref):
    t = pl.program_id(1)
    t0 = t * TOK_TILE
    seg_lo = jnp.where(t == 0, 0, TOK_TILE)
    seg_hi = jnp.where(t == 0, TOK_TILE, s_total)
    span = TOK_TILE + 2 * POOL_HALO
    start = pl.multiple_of(jnp.clip(t0 - POOL_HALO, 0, s_total - span), SUBLANES)
    hi, lo = _split(u_ref[0, pl.ds(start, span), :])
    own = u_ref[0, pl.ds(pl.multiple_of(t0, TOK_TILE), TOK_TILE), :]
    tok_q = t0 + lax.broadcasted_iota(jnp.int32, (TOK_TILE, span), 0)
    tok_k = start + lax.broadcasted_iota(jnp.int32, (TOK_TILE, span), 1)
    group = _lane_iota(own.shape) // HEAD_DIM
    mean = jnp.zeros(own.shape, F32)
    for g, win in enumerate(POOL_WINDOWS):
        lo_t = jnp.maximum(tok_q - win // 2, seg_lo)
        hi_t = jnp.minimum(tok_q + win // 2, seg_hi)
        band = jnp.where(tok_k >= lo_t, jnp.where(tok_k < hi_t, 1.0, 0.0), 0.0).astype(BF16)
        count = (hi_t - lo_t)[:, 0:1].astype(F32)
        total = _dot(band, hi) + _dot(band, lo)
        mean = jnp.where(group == g, total / count, mean)
    p = (mean - own).astype(BF16)
    o_ref[0] = (_dot(p, w_ref[...]) * scale_ref[...]).astype(o_ref.dtype)


def _pool_call(u, w_blockdiag, scale):
    b, s, w = u.shape
    return pl.pallas_call(
        functools.partial(_pool_kernel, s),
        grid=(b, s // TOK_TILE),
        in_specs=[pl.BlockSpec((1, s, w), lambda i, t: (i, 0, 0)),
                  pl.BlockSpec((w, w), lambda i, t: (0, 0)),
                  pl.BlockSpec((1, w), lambda i, t: (0, 0))],
        out_specs=pl.BlockSpec((1, TOK_TILE, w), lambda i, t: (i, t, 0)),
        out_shape=jax.ShapeDtypeStruct((b, s, w), BF16),
        compiler_params=_cparams(("parallel", "parallel")),
        name="pool_mix",
    )(u, w_blockdiag, scale.reshape(1, w))


def _outproj_kernel(n_parts, *refs):
    y_refs = refs[:n_parts]
    w_refs = refs[n_parts:2 * n_parts]
    x_ref, mod_ref, norm_ref, x1_ref, hi_ref, lo_ref = refs[2 * n_parts:]
    acc = _dot(y_refs[0][0], w_refs[0][...])
    for y_ref, w_ref in zip(y_refs[1:], w_refs[1:]):
        acc = acc + _dot(y_ref[0], w_ref[...])
    x1 = x_ref[0] + mod_ref[0, 0, 2:3, :] * acc
    x1_ref[0] = x1
    xm = _rms(x1, norm_ref[...]) * (1.0 + mod_ref[0, 0, 4:5, :]) + mod_ref[0, 0, 3:4, :]
    hi, lo = _split(xm)
    hi_ref[0] = hi
    lo_ref[0] = lo


def _outproj_call(parts, weights, x, mods, norm_g):
    b, s, d = x.shape
    tok = lambda w: pl.BlockSpec((1, TOK_TILE, w), lambda i, t: (i, t, 0))
    specs = [tok(p.shape[-1]) for p in parts]
    specs += [pl.BlockSpec(w.shape, lambda i, t: (0, 0)) for w in weights]
    specs += [tok(d),
              pl.BlockSpec((1, 1, N_MOD, d), lambda i, t: (i, jnp.minimum(t, 1), 0, 0)),
              pl.BlockSpec((1, d), lambda i, t: (0, 0))]
    return pl.pallas_call(
        functools.partial(_outproj_kernel, len(parts)),
        grid=(b, s // TOK_TILE),
        in_specs=specs,
        out_specs=[tok(d), tok(d), tok(d)],
        out_shape=[jax.ShapeDtypeStruct((b, s, d), F32), jax.ShapeDtypeStruct((b, s, d), BF16),
                   jax.ShapeDtypeStruct((b, s, d), BF16)],
        compiler_params=_cparams(("parallel", "parallel")),
        name="out_proj",
    )(*parts, *weights, x, mods, norm_g.reshape(1, d))


def _peer_fold_kernel(k_ref, wt_ref, hi_ref, lo_ref):
    k_hi, k_lo = _split(k_ref[0])
    w_hi, w_lo = _split(wt_ref[0])
    kw = _dot3(k_hi, k_lo, w_hi, w_lo)
    hi, lo = _split(kw)
    hi_ref[0] = hi
    lo_ref[0] = lo


def _peer_fold_call(keys_ph, wq_t):
    n, nk, kd = keys_ph.shape
    d = wq_t.shape[-1]
    blk = pl.BlockSpec((1, nk, d), lambda i: (i, 0, 0))
    return pl.pallas_call(
        _peer_fold_kernel,
        grid=(n,),
        in_specs=[pl.BlockSpec((1, nk, kd), lambda i: (i, 0, 0)), pl.BlockSpec((1, kd, d), lambda i: (i, 0, 0))],
        out_specs=[blk, blk],
        out_shape=[jax.ShapeDtypeStruct((n, nk, d), BF16)] * 2,
        compiler_params=_cparams(("parallel",)),
        name="peer_fold",
    )(keys_ph, wq_t)


def _peer_scores_kernel(kw_hi_ref, kw_lo_ref, x_hi_ref, x_lo_ref, st_ref):
    st_ref[...] = _dot3(kw_hi_ref[...], kw_lo_ref[...], x_hi_ref[...], x_lo_ref[...], _NT)


def _peer_scores_call(kw_hi, kw_lo, x_hi, x_lo, tm):
    n, d = x_hi.shape
    r = kw_hi.shape[0]
    kw = pl.BlockSpec((r, d), lambda t: (0, 0))
    xs = pl.BlockSpec((tm, d), lambda t: (t, 0))
    return pl.pallas_call(
        _peer_scores_kernel,
        grid=(n // tm,),
        in_specs=[kw, kw, xs, xs],
        out_specs=pl.BlockSpec((r, tm), lambda t: (0, t)),
        out_shape=jax.ShapeDtypeStruct((r, n), F32),
        compiler_params=_cparams(("parallel",)),
        name="peer_scores",
    )(kw_hi, kw_lo, x_hi, x_lo)


def _bitonic_merge_desc(v):
    n = len(v)
    if n == 1:
        return v
    half = n // 2
    top = [jnp.maximum(v[i], v[i + half]) for i in range(half)]
    bot = [jnp.minimum(v[i], v[i + half]) for i in range(half)]
    return _bitonic_merge_desc(top) + _bitonic_merge_desc(bot)


def _sort_desc(v):
    n = len(v)
    if n == 1:
        return v
    return _bitonic_merge_desc(_sort_desc(v[:n // 2]) + _sort_desc(v[n // 2:])[::-1])


def _merge_top(a, b):
    n = len(a)
    return _bitonic_merge_desc([jnp.maximum(a[i], b[n - 1 - i]) for i in range(n)])


def _peer_topk_kernel(st_ref, tau_ref, cz_ref):
    k = PEER_TOPK
    half_rows = PEER_HEADS * PEER_N_KEYS
    tops = []
    for p in range(2):
        groups = []
        for g in range(PEER_N_KEYS // k):
            vals = [st_ref[pl.ds(p * half_rows + g * k + j, PEER_HEADS, stride=PEER_N_KEYS), :]
                    for j in range(k)]
            groups.append(_sort_desc(vals))
        while len(groups) > 1:
            groups = [_merge_top(groups[i], groups[i + 1]) for i in range(0, len(groups), 2)]
        tops.append(groups[0])
    t1, t2 = tops
    neg = jnp.full(t1[0].shape, NEG, F32)
    rows = [[t1[i] + t2[j] for j in range(k // (i + 1))] for i in range(k)]
    first = _merge_top(rows[0], rows[1] + [neg] * (k - len(rows[1])))
    rest = [c for row in rows[2:] for c in row]
    rest = _sort_desc(rest + [neg] * (2 * k - len(rest)))[:k]
    top = [jnp.maximum(first[i], rest[k - 1 - i]) for i in range(k)]
    tau = functools.reduce(jnp.minimum, top)
    m = rows[0][0]
    z = functools.reduce(lambda a, c: a + c, [jnp.exp(c - m) for c in top])
    tau_ref[...] = tau
    cz_ref[...] = m + jnp.log(z)


def _peer_topk_call(st):
    r, n = st.shape
    out = pl.BlockSpec((PEER_HEADS, LANES), lambda t: (0, t))
    return pl.pallas_call(
        _peer_topk_kernel,
        grid=(n // LANES,),
        in_specs=[pl.BlockSpec((r, LANES), lambda t: (0, t))],
        out_specs=[out, out],
        out_shape=[jax.ShapeDtypeStruct((PEER_HEADS, n), F32)] * 2,
        compiler_params=_cparams(("parallel",)),
        name="peer_topk",
    )(st)


def _peer_dense_kernel(rows_per_chunk, x_ref, st_ref, tau_ref, cz_ref, u_ref, vt_ref, y_ref, acc_ref, g_ref):
    c = pl.program_id(1)
    tm = x_ref.shape[0]
    nk = PEER_N_KEYS
    half_rows = PEER_HEADS * nk

    @pl.when(c == 0)
    def _():
        acc_ref[...] = jnp.zeros(acc_ref.shape, F32)

    ht = _dot(u_ref[...], x_ref[...], _NT)
    for lt in range(tm // LANES):
        cols = slice(lt * LANES, (lt + 1) * LANES)
        s1_rows = [st_ref[pl.ds(pl.multiple_of(h * nk + c * SUBLANES, SUBLANES), SUBLANES), cols]
                   for h in range(PEER_HEADS)]
        for r in range(rows_per_chunk):
            w = jnp.zeros((nk, LANES), F32)
            for h in range(PEER_HEADS):
                s2 = st_ref[half_rows + h * nk:half_rows + (h + 1) * nk, cols]
                ssum = s1_rows[h][r:r + 1, :] + s2
                gate = jnp.exp(ssum - cz_ref[h:h + 1, cols])
                w = w + jnp.where(ssum >= tau_ref[h:h + 1, cols], gate, 0.0)
            hv = ht[r * nk:(r + 1) * nk, cols]
            act = 0.5 * hv * (1.0 + lax.erf(hv * SQRT_HALF))
            g_ref[r * nk:(r + 1) * nk, cols] = (w * act).astype(BF16)
    acc_ref[...] += _dot(vt_ref[...], g_ref[...])

    @pl.when(c == pl.num_programs(1) - 1)
    def _():
        y_ref[...] = acc_ref[...].T


def _peer_dense_call(x_hi, st, tau, cz, u_bf, vt_bf, tm, rows_per_chunk):
    n, d = x_hi.shape
    n_exp = u_bf.shape[0]
    ne = rows_per_chunk * PEER_N_KEYS
    return pl.pallas_call(
        functools.partial(_peer_dense_kernel, rows_per_chunk),
        grid=(n // tm, n_exp // ne),
        in_specs=[
            pl.BlockSpec((tm, d), lambda t, c: (t, 0)),
            pl.BlockSpec((st.shape[0], tm), lambda t, c: (0, t)),
            pl.BlockSpec((PEER_HEADS, tm), lambda t, c: (0, t)),
            pl.BlockSpec((PEER_HEADS, tm), lambda t, c: (0, t)),
            pl.BlockSpec((ne, d), lambda t, c: (c, 0)),
            pl.BlockSpec((d, ne), lambda t, c: (0, c)),
        ],
        out_specs=pl.BlockSpec((tm, d), lambda t, c: (t, 0)),
        out_shape=jax.ShapeDtypeStruct((n, d), F32),
        scratch_shapes=[pltpu.VMEM((d, tm), F32), pltpu.VMEM((ne, tm), BF16)],
        compiler_params=_cparams(("parallel", "arbitrary")),
        name="peer_dense",
    )(x_hi, st, tau, cz, u_bf, vt_bf)


def _peer(x_hi, x_lo, wq, keys, u, v):
    b, s, d = x_hi.shape
    n = b * s
    heads, _, nk, kd = keys.shape
    keys_ph = keys.transpose(1, 0, 2, 3).reshape(2 * heads, nk, kd)
    wq_t = wq.T.reshape(heads, 2, kd, d).transpose(1, 0, 2, 3).reshape(2 * heads, kd, d)
    kw_hi, kw_lo = _peer_fold_call(keys_ph, wq_t)
    kw_hi = kw_hi.reshape(2 * heads * nk, d)
    kw_lo = kw_lo.reshape(2 * heads * nk, d)
    tm = 512 if n % 512 == 0 else TOK_TILE
    xh = x_hi.reshape(n, d)
    st = _peer_scores_call(kw_hi, kw_lo, xh, x_lo.reshape(n, d), tm)
    tau, cz = _peer_topk_call(st)
    y = _peer_dense_call(xh, st, tau, cz, u.astype(BF16), v.T.astype(BF16), tm, SUBLANES)
    return y.reshape(b, s, d)


def _final_kernel(x_ref, y_ref, mod_ref, o_ref):
    o_ref[0] = x_ref[0] + mod_ref[0, 0, 5:6, :] * y_ref[0]


def _final_call(x1, y, mods):
    b, s, d = x1.shape
    nt = s // TOK_TILE - 1
    src = pl.BlockSpec((1, TOK_TILE, d), lambda i, t: (i, t + 1, 0))
    return pl.pallas_call(
        _final_kernel,
        grid=(b, nt),
        in_specs=[src, src, pl.BlockSpec((1, 1, N_MOD, d), lambda i, t: (i, 1, 0, 0))],
        out_specs=pl.BlockSpec((1, TOK_TILE, d), lambda i, t: (i, t, 0)),
        out_shape=jax.ShapeDtypeStruct((b, nt * TOK_TILE, d), F32),
        compiler_params=_cparams(("parallel", "parallel")),
        name="final_residual",
    )(x1, y, mods)


def _gain2(g):
    return jnp.concatenate([g, g]).reshape(1, LANES).astype(F32)


def kernel(x, c, ctx, c_ctx, l0_ada_w, l0_ada_b, l0_norm1, l0_norm2, l0_w_in, l0_w_out, l0_pool_w, l0_pool_scale, l0_diff_qnorm, l0_diff_knorm, l0_lambda_q1, l0_lambda_k1, l0_lambda_q2, l0_lambda_k2, l0_diff_subln, l0_peer_wq, l0_peer_keys, l0_peer_u, l0_peer_v, l1_ada_w, l1_ada_b, l1_norm1, l1_norm2, l1_w_in, l1_w_out, l1_gqa_qnorm, l1_gqa_knorm, l1_na_qnorm, l1_na_knorm, l1_na_rpb, l1_peer_wq, l1_peer_keys, l1_peer_u, l1_peer_v):
    b, t_lat, d = x.shape
    ctx_len = ctx.shape[1]
    assert ctx_len == TOK_TILE and t_lat % KV_CHUNK == 0 and t_lat // GRID_W >= NA_WIN_ROWS
    s = ctx_len + t_lat
    xs = jnp.concatenate([ctx, x], axis=1)
    cos, sin = _rope_tables(s, ctx_len)
    qk_scale = HEAD_DIM ** -0.5

    mods0 = _modulation(c, c_ctx, l0_ada_w, l0_ada_b)
    w0 = l0_w_in.astype(BF16)
    pw = l0_pool_scale.shape[0]
    dw = DIFF_HEADS * 2 * HEAD_DIM
    roles0 = [("plain_f32", pw, 1.0, False), ("norm_rope", dw, qk_scale, True), ("norm_rope", dw, 1.0, False),
              ("plain", dw, 1.0, True)]
    weights0 = [w0[:, :pw], w0[:, pw:pw + dw], w0[:, pw + dw:pw + 2 * dw], w0[:, pw + 2 * dw:]]
    gains0 = [None, _gain2(l0_diff_qnorm), _gain2(l0_diff_knorm), None]
    u0, q0, k0, v0 = _inproj_call(xs, mods0, l0_norm1, cos, sin, roles0, weights0, gains0)
    lam_init = 0.8 - 0.6 * math.exp(-0.3 * 0)
    lam_vecs = jnp.stack([l0_lambda_q1, l0_lambda_k1, l0_lambda_q2, l0_lambda_k2]).astype(F32)
    o_diff = _diff_attn_call(q0, k0, v0, lam_vecs, l0_diff_subln, lam_init)
    pool_bd = jax.scipy.linalg.block_diag(*[l0_pool_w[g] for g in range(l0_pool_w.shape[0])]).astype(BF16)
    y_pool = _pool_call(u0, pool_bd, l0_pool_scale)
    wo0 = l0_w_out.astype(BF16)
    x1, xh, xl = _outproj_call([y_pool, o_diff], [wo0[:pw], wo0[pw:]], xs, mods0, l0_norm2)
    y_peer0 = _peer(xh, xl, l0_peer_wq, l0_peer_keys, l0_peer_u, l0_peer_v)

    mods1 = _modulation(c, c_ctx, l1_ada_w, l1_ada_b)
    w1 = l1_w_in.astype(BF16)
    n_q = 8 * HEAD_DIM
    n_kv = GQA_KV_HEADS * HEAD_DIM
    o_ck, o_cv, o_nq, o_nk, o_nv = n_q, n_q + n_kv, n_q + 2 * n_kv, 2 * n_q + 2 * n_kv, 3 * n_q + 2 * n_kv
    dup = lambda w: jnp.concatenate([w[:, :HEAD_DIM], w[:, :HEAD_DIM], w[:, HEAD_DIM:], w[:, HEAD_DIM:]], axis=1)
    roles1 = [("norm_rope", n_q, qk_scale, True), ("norm_rope", 2 * n_kv, 1.0, False), ("plain", n_kv, 1.0, True),
              ("norm", n_q, qk_scale, False), ("norm", n_q, 1.0, False), ("plain", n_q, 1.0, False)]
    weights1 = [w1[:, :o_ck], dup(w1[:, o_ck:o_cv]), w1[:, o_cv:o_nq],
                w1[:, o_nq:o_nk], w1[:, o_nk:o_nv], w1[:, o_nv:]]
    gains1 = [_gain2(l1_gqa_qnorm), _gain2(l1_gqa_knorm), None, _gain2(l1_na_qnorm), _gain2(l1_na_knorm), None]
    x2, cq, ckd, cvd, nq, nk_, nv = _inproj_call(x1, mods1, l1_norm1, cos, sin, roles1, weights1, gains1,
                                                 residual=(y_peer0, mods0))
    o_gqa = _gqa_attn_call(cq, ckd, cvd)
    bias = _na_bias_table(l1_na_rpb, t_lat // GRID_W)
    o_na = _na_attn_call(nq, nk_, nv, bias)
    wo1 = l1_w_out.astype(BF16)
    x3, xh1, xl1 = _outproj_call([o_gqa, o_na], [wo1[:n_q], wo1[n_q:]], x2, mods1, l1_norm2)
    y_peer1 = _peer(xh1, xl1, l1_peer_wq, l1_peer_keys, l1_peer_u, l1_peer_v)
    return _final_call(x3, y_peer1, mods1)
```

```python
import functools
import math

import jax
import jax.numpy as jnp
from jax import lax
from jax.experimental import pallas as pl
from jax.experimental.pallas import tpu as pltpu

F32 = jnp.float32
BF16 = jnp.bfloat16

LANES = 128
SUBLANES = 8
VMEM_LIMIT_BYTES = 56 * 1024 * 1024

HEAD_DIM = 64
GRID_W = 64
ROPE_THETA = 10000.0
EPS = 1e-6
N_MOD = 6
POOL_WINDOWS = (2, 4, 8, 16)
POOL_HALO = max(POOL_WINDOWS) // 2
DIFF_HEADS = 6
GQA_KV_HEADS = 2
NA_WIN_ROWS = 8
NA_WIN_COLS = 16
PEER_HEADS = 8
PEER_N_KEYS = 128
PEER_TOPK = 16
PEER_SUB_KEYS = 32
TOK_TILE = 256
KV_CHUNK = 2048
NEG = -1e30
SQRT_HALF = 0.7071067811865476


def _cparams(sem, vmem=VMEM_LIMIT_BYTES):
    return pltpu.CompilerParams(dimension_semantics=sem, vmem_limit_bytes=vmem)


def _split(x):
    hi = x.astype(BF16)
    lo = (x - hi.astype(F32)).astype(BF16)
    return hi, lo


_NN = (((1,), (0,)), ((), ()))
_NT = (((1,), (1,)), ((), ()))


def _dot(a, b, dims=_NN):
    return lax.dot_general(a, b, dims, preferred_element_type=F32)


def _dot3(a_hi, a_lo, b_hi, b_lo, dims=_NN):
    return _dot(a_hi, b_hi, dims) + _dot(a_hi, b_lo, dims) + _dot(a_lo, b_hi, dims)


def _lane_iota(shape):
    return lax.broadcasted_iota(jnp.int32, shape, len(shape) - 1)


def _rms(x, gain):
    ms = jnp.mean(x * x, axis=-1, keepdims=True)
    return x * lax.rsqrt(ms + EPS) * gain


def _ada_kernel(c_ref, w_ref, b_ref, o_ref):
    c = c_ref[...]
    a = c / (1.0 + jnp.exp(-c))
    a_hi, a_lo = _split(a)
    w_hi, w_lo = _split(w_ref[...])
    o_ref[...] = _dot3(a_hi, a_lo, w_hi, w_lo) + b_ref[...]


def _ada_call(cvec, ada_w, ada_b):
    rows, d = cvec.shape
    n = ada_w.shape[1]
    bn = 768
    return pl.pallas_call(
        _ada_kernel,
        grid=(n // bn,),
        in_specs=[
            pl.BlockSpec((rows, d), lambda j: (0, 0)),
            pl.BlockSpec((d, bn), lambda j: (0, j)),
            pl.BlockSpec((1, bn), lambda j: (0, j)),
        ],
        out_specs=pl.BlockSpec((rows, bn), lambda j: (0, j)),
        out_shape=jax.ShapeDtypeStruct((rows, n), F32),
        compiler_params=_cparams(("parallel",)),
        name="ada_mod",
    )(cvec, ada_w, ada_b.reshape(1, n))


def _modulation(c, c_ctx, ada_w, ada_b):
    b, d = c.shape
    rows = -(-(b + 1) // 16) * 16
    cvec = jnp.zeros((rows, d), F32).at[0].set(c_ctx).at[1:b + 1].set(c)
    m = _ada_call(cvec, ada_w, ada_b).reshape(rows, N_MOD, d)
    return jnp.stack([jnp.broadcast_to(m[0], (b, N_MOD, d)), m[1:b + 1]], axis=1)


def _head_norm(y, gain):
    r = lax.broadcasted_iota(jnp.int32, (LANES, LANES), 0) // HEAD_DIM
    c = lax.broadcasted_iota(jnp.int32, (LANES, LANES), 1) // HEAD_DIM
    ones_bd = jnp.where(r == c, 1.0, 0.0).astype(BF16)
    hi, lo = _split(y * y)
    ss = _dot(hi, ones_bd) + _dot(lo, ones_bd)
    return y * lax.rsqrt(ss * (1.0 / HEAD_DIM) + EPS) * gain


def _rope(y, cos, sin):
    up = pltpu.roll(y, LANES - 16, 1)
    down = pltpu.roll(y, 16, 1)
    partner = jnp.where((_lane_iota(y.shape) & 16) == 0, up, down)
    return y * cos + partner * sin


def _inproj_kernel(roles, fuse_residual, *refs):
    refs = list(refs)
    x_ref = refs.pop(0)
    if fuse_residual:
        y_ref = refs.pop(0)
        pmod_ref = refs.pop(0)
    mod_ref = refs.pop(0)
    norm_ref = refs.pop(0)
    cos_ref = refs.pop(0)
    sin_ref = refs.pop(0)
    w_refs, g_refs = [], []
    for kind, _, _, _ in roles:
        w_refs.append(refs.pop(0))
        g_refs.append(refs.pop(0) if kind in ("norm", "norm_rope") else None)
    if fuse_residual:
        xo_ref = refs.pop(0)
    out_refs = refs

    x = x_ref[0]
    if fuse_residual:
        x = x + pmod_ref[0, 0, 5:6, :] * y_ref[0]
        xo_ref[0] = x
    xm = _rms(x, norm_ref[...]) * (1.0 + mod_ref[0, 0, 1:2, :]) + mod_ref[0, 0, 0:1, :]
    xm = xm.astype(BF16)
    cos = cos_ref[...]
    sin = sin_ref[...]
    for (kind, width, scale, transposed), w_ref, g_ref, o_ref in zip(roles, w_refs, g_refs, out_refs):
        acc = _dot(xm, w_ref[...])
        if kind in ("plain", "plain_f32") and not transposed:
            o_ref[0] = acc.astype(o_ref.dtype)
            continue
        for j in range(width // LANES):
            y = acc[:, j * LANES:(j + 1) * LANES]
            if kind in ("norm", "norm_rope"):
                y = _head_norm(y, g_ref[...])
            if kind == "norm_rope":
                y = _rope(y, cos, sin)
            if scale != 1.0:
                y = y * scale
            if transposed:
                o_ref[0, j * LANES:(j + 1) * LANES, :] = y.T.astype(o_ref.dtype)
            else:
                o_ref[0, :, j * LANES:(j + 1) * LANES] = y.astype(o_ref.dtype)


def _inproj_call(x, mods, norm_g, cos, sin, roles, weights, gains, residual=None):
    b, s, d = x.shape
    nt = s // TOK_TILE
    tok = lambda w: pl.BlockSpec((1, TOK_TILE, w), lambda i, t: (i, t, 0))
    mod_spec = pl.BlockSpec((1, 1, N_MOD, d), lambda i, t: (i, jnp.minimum(t, 1), 0, 0))
    args, specs = [x], [tok(d)]
    if residual is not None:
        y, pmods = residual
        args += [y, pmods]
        specs += [tok(d), mod_spec]
    args += [mods, norm_g.reshape(1, d), cos, sin]
    specs += [mod_spec, pl.BlockSpec((1, d), lambda i, t: (0, 0)),
              pl.BlockSpec((TOK_TILE, LANES), lambda i, t: (t, 0)),
              pl.BlockSpec((TOK_TILE, LANES), lambda i, t: (t, 0))]
    for (kind, width, _, _), w, g in zip(roles, weights, gains):
        args.append(w)
        specs.append(pl.BlockSpec((d, width), lambda i, t: (0, 0)))
        if kind in ("norm", "norm_rope"):
            args.append(g)
            specs.append(pl.BlockSpec((1, LANES), lambda i, t: (0, 0)))
    out_shapes, out_specs = [], []
    if residual is not None:
        out_shapes.append(jax.ShapeDtypeStruct((b, s, d), F32))
        out_specs.append(tok(d))
    for kind, width, _, transposed in roles:
        dt = F32 if kind == "plain_f32" else BF16
        if transposed:
            out_shapes.append(jax.ShapeDtypeStruct((b, width, s), dt))
            out_specs.append(pl.BlockSpec((1, width, TOK_TILE), lambda i, t: (i, 0, t)))
        else:
            out_shapes.append(jax.ShapeDtypeStruct((b, s, width), dt))
            out_specs.append(tok(width))
    return pl.pallas_call(
        functools.partial(_inproj_kernel, tuple(roles), residual is not None),
        grid=(b, nt),
        in_specs=specs,
        out_specs=out_specs,
        out_shape=out_shapes,
        compiler_params=_cparams(("parallel", "parallel")),
        name="in_proj",
    )(*args)


def _rope_tables(s_total, ctx_len):
    t = jnp.arange(s_total - ctx_len, dtype=jnp.int32)
    pos = jnp.stack([t // GRID_W, t % GRID_W], axis=-1).astype(F32)
    n_freq = HEAD_DIM // 4
    inv_freq = ROPE_THETA ** (-jnp.arange(n_freq, dtype=F32) / n_freq)
    ang = pos[:, :, None] * inv_freq
    cos, sin = jnp.cos(ang), jnp.sin(ang)
    cos64 = jnp.concatenate([cos[:, 0], cos[:, 0], cos[:, 1], cos[:, 1]], axis=-1)
    sin64 = jnp.concatenate([-sin[:, 0], sin[:, 0], -sin[:, 1], sin[:, 1]], axis=-1)
    cos128 = jnp.concatenate([jnp.ones((ctx_len, LANES), F32), jnp.tile(cos64, (1, 2))], axis=0)
    sin128 = jnp.concatenate([jnp.zeros((ctx_len, LANES), F32), jnp.tile(sin64, (1, 2))], axis=0)
    return cos128, sin128


def _attend_all(qt, k_ref, vt_ref, is_latent, acc_ref):
    r = qt.shape[1]
    t_lat = k_ref.shape[1] - TOK_TILE
    chunk = math.gcd(t_lat, KV_CHUNK)
    acc_ref[...] = jnp.zeros(acc_ref.shape, F32)

    def step(kc, vtc, m_prev, l_prev):
        s = _dot(kc, qt)
        m_new = jnp.maximum(m_prev, jnp.max(s, axis=0, keepdims=True))
        alpha = jnp.exp(m_prev - m_new)
        p = jnp.exp(s - m_new)
        acc_ref[...] = alpha * acc_ref[...] + _dot(vtc, p.astype(BF16))
        return m_new, alpha * l_prev + jnp.sum(p, axis=0, keepdims=True)

    m, l = step(k_ref[0, 0:TOK_TILE, :], vt_ref[0, :, 0:TOK_TILE],
                jnp.full((1, r), NEG, F32), jnp.zeros((1, r), F32))

    def body(j, carry):
        off = pl.multiple_of(TOK_TILE + j * chunk, TOK_TILE)
        return step(k_ref[0, pl.ds(off, chunk), :], vt_ref[0, :, pl.ds(off, chunk)], *carry)

    m, l = lax.fori_loop(0, jnp.where(is_latent, t_lat // chunk, 0), body, (m, l))
    return acc_ref[...] / l


def _half_masks(q):
    lo = _lane_iota(q.shape) < HEAD_DIM
    zero = jnp.zeros_like(q)
    return jnp.where(lo, q, zero), jnp.where(lo, zero, q)


def _row_half_masks(qt):
    top = lax.broadcasted_iota(jnp.int32, qt.shape, 0) < HEAD_DIM
    zero = jnp.zeros_like(qt)
    return jnp.where(top, qt, zero), jnp.where(top, zero, qt)


def _diff_attn_kernel(lam_init, qt_ref, k_ref, vt_ref, lam_ref, subln_ref, o_ref, acc_ref):
    q1, q2 = _row_half_masks(qt_ref[0])
    qt = jnp.concatenate([q1, q2], axis=1)
    o = _attend_all(qt, k_ref, vt_ref, pl.program_id(2) > 0, acc_ref)
    lv = lam_ref[...]
    lam = (jnp.exp(jnp.sum(lv[0:1] * lv[1:2], axis=-1, keepdims=True))
           - jnp.exp(jnp.sum(lv[2:3] * lv[3:4], axis=-1, keepdims=True)) + lam_init)
    od = o[:, :TOK_TILE] - lam * o[:, TOK_TILE:]
    ms = jnp.mean(od * od, axis=0, keepdims=True)
    on = od * lax.rsqrt(ms + EPS) * subln_ref[...] * (1.0 - lam_init)
    o_ref[0] = on.T.astype(o_ref.dtype)


def _diff_attn_call(qt, k, vt, lam_vecs, subln, lam_init):
    b, s, w = k.shape
    heads = w // LANES
    return pl.pallas_call(
        functools.partial(_diff_attn_kernel, lam_init),
        grid=(b, heads, s // TOK_TILE),
        in_specs=[
            pl.BlockSpec((1, LANES, TOK_TILE), lambda i, h, t: (i, h, t)),
            pl.BlockSpec((1, s, LANES), lambda i, h, t: (i, 0, h)),
            pl.BlockSpec((1, LANES, s), lambda i, h, t: (i, h, 0)),
            pl.BlockSpec((4, HEAD_DIM), lambda i, h, t: (0, 0)),
            pl.BlockSpec((LANES, 1), lambda i, h, t: (0, 0)),
        ],
        out_specs=pl.BlockSpec((1, TOK_TILE, LANES), lambda i, h, t: (i, t, h)),
        out_shape=jax.ShapeDtypeStruct((b, s, w), BF16),
        scratch_shapes=[pltpu.VMEM((LANES, 2 * TOK_TILE), F32)],
        compiler_params=_cparams(("parallel", "parallel", "parallel")),
        name="diff_attn",
    )(qt, k, vt, lam_vecs, subln.reshape(LANES, 1))


def _gqa_attn_kernel(qt_ref, k_ref, vt_ref, o_ref, acc_ref):
    qa = _row_half_masks(qt_ref[0, 0:LANES, :])
    qb = _row_half_masks(qt_ref[0, LANES:2 * LANES, :])
    qt = jnp.concatenate([qa[0], qa[1], qb[0], qb[1]], axis=1)
    o = _attend_all(qt, k_ref, vt_ref, pl.program_id(2) > 0, acc_ref)
    t = TOK_TILE
    ot = jnp.concatenate([o[:, j * t:(j + 1) * t] for j in range(4)], axis=0)
    o_ref[0] = ot.T.astype(o_ref.dtype)


def _gqa_attn_call(qt, k_dup, vt):
    b, w, s = qt.shape
    groups = w // (2 * LANES)
    return pl.pallas_call(
        _gqa_attn_kernel,
        grid=(b, groups, s // TOK_TILE),
        in_specs=[pl.BlockSpec((1, 2 * LANES, TOK_TILE), lambda i, g, t: (i, g, t)),
                  pl.BlockSpec((1, s, LANES), lambda i, g, t: (i, 0, g)),
                  pl.BlockSpec((1, HEAD_DIM, s), lambda i, g, t: (i, g, 0))],
        out_specs=pl.BlockSpec((1, TOK_TILE, 2 * LANES), lambda i, g, t: (i, t, g)),
        out_shape=jax.ShapeDtypeStruct((b, s, w), BF16),
        scratch_shapes=[pltpu.VMEM((HEAD_DIM, 4 * TOK_TILE), F32)],
        compiler_params=_cparams(("parallel", "parallel", "parallel")),
        name="gqa_attn",
    )(qt, k_dup, vt)


def _na_attn_kernel(n_rows, q_ref, k_ref, v_ref, bias_ref, o_ref):
    t = pl.program_id(2)

    @pl.when(t == 0)
    def _():
        o_ref[0] = jnp.zeros(o_ref.shape[1:], o_ref.dtype)

    @pl.when(t > 0)
    def _():
        kctx = k_ref[0, 0:TOK_TILE, :]
        vctx = v_ref[0, 0:TOK_TILE, :]
        win = NA_WIN_ROWS * GRID_W
        rows_per_tile = TOK_TILE // GRID_W
        lo = _lane_iota((GRID_W, LANES)) < HEAD_DIM
        for i in range(rows_per_tile):
            r = (t - 1) * rows_per_tile + i
            r0 = jnp.clip(r - NA_WIN_ROWS // 2, 0, n_rows - NA_WIN_ROWS)
            off = pl.multiple_of(TOK_TILE + r0 * GRID_W, GRID_W)
            q1, q2 = _half_masks(q_ref[0, i * GRID_W:(i + 1) * GRID_W, :])
            q = jnp.concatenate([q1, q2], axis=0)
            s_nb = _dot(q, k_ref[0, pl.ds(off, win), :], _NT) + bias_ref[0, r - r0]
            s_cx = _dot(q, kctx, _NT)
            m = jnp.maximum(jnp.max(s_nb, axis=-1, keepdims=True), jnp.max(s_cx, axis=-1, keepdims=True))
            p_nb = jnp.exp(s_nb - m)
            p_cx = jnp.exp(s_cx - m)
            den = jnp.sum(p_nb, axis=-1, keepdims=True) + jnp.sum(p_cx, axis=-1, keepdims=True)
            o = (_dot(p_nb.astype(BF16), v_ref[0, pl.ds(off, win), :]) + _dot(p_cx.astype(BF16), vctx)) / den
            o_ref[0, i * GRID_W:(i + 1) * GRID_W, :] = jnp.where(lo, o[:GRID_W], o[GRID_W:]).astype(o_ref.dtype)


def _na_bias_table(rpb, n_rows):
    kr = NA_WIN_ROWS
    cols = jnp.arange(GRID_W, dtype=jnp.int32)
    c0 = jnp.clip(cols - NA_WIN_COLS // 2, 0, GRID_W - NA_WIN_COLS)
    kc = jnp.arange(GRID_W, dtype=jnp.int32)
    inside = (kc[None, :] >= c0[:, None]) & (kc[None, :] < c0[:, None] + NA_WIN_COLS)
    dc = kc[None, :] - cols[:, None] + (NA_WIN_COLS - 1)
    onehot = ((dc[:, :, None] == jnp.arange(2 * NA_WIN_COLS - 1)) & inside[:, :, None]).astype(F32)
    by_col = jnp.einsum("hrd,ckd->hrck", rpb.astype(F32), onehot, precision=lax.Precision.HIGHEST)
    by_col = jnp.where(inside[None, None], by_col, NEG)
    tab = jnp.stack([by_col[:, NA_WIN_ROWS - 1 - var:2 * NA_WIN_ROWS - 1 - var]
                     for var in range(NA_WIN_ROWS)], axis=1)
    h = rpb.shape[0]
    tab = tab.transpose(0, 1, 3, 2, 4).reshape(h // 2, 2, NA_WIN_ROWS, GRID_W, kr * GRID_W)
    return tab.transpose(0, 2, 1, 3, 4).reshape(h // 2, NA_WIN_ROWS, 2 * GRID_W, kr * GRID_W)


def _na_attn_call(q, k, v, bias):
    b, s, w = q.shape
    n_rows = (s - TOK_TILE) // GRID_W
    seq = pl.BlockSpec((1, s, LANES), lambda i, h, t: (i, 0, h))
    tile = pl.BlockSpec((1, TOK_TILE, LANES), lambda i, h, t: (i, t, h))
    return pl.pallas_call(
        functools.partial(_na_attn_kernel, n_rows),
        grid=(b, w // LANES, s // TOK_TILE),
        in_specs=[tile, seq, seq,
                  pl.BlockSpec((1,) + bias.shape[1:], lambda i, h, t: (h, 0, 0, 0))],
        out_specs=tile,
        out_shape=jax.ShapeDtypeStruct((b, s, w), BF16),
        compiler_params=_cparams(("parallel", "parallel", "parallel")),
        name="na_attn",
    )(q, k, v, bias)


def _pool_kernel(s_total, u_ref, w_ref, scale_ref, o_ref):
    t = pl.program_id(1)
    t0 = t * TOK_TILE
    seg_lo = jnp.where(t == 0, 0, TOK_TILE)
    seg_hi = jnp.where(t == 0, TOK_TILE, s_total)
    span = TOK_TILE + 2 * POOL_HALO
    start = pl.multiple_of(jnp.clip(t0 - POOL_HALO, 0, s_total - span), SUBLANES)
    hi, lo = _split(u_ref[0, pl.ds(start, span), :])
    own = u_ref[0, pl.ds(pl.multiple_of(t0, TOK_TILE), TOK_TILE), :]
    tok_q = t0 + lax.broadcasted_iota(jnp.int32, (TOK_TILE, span), 0)
    tok_k = start + lax.broadcasted_iota(jnp.int32, (TOK_TILE, span), 1)
    group = _lane_iota(own.shape) // HEAD_DIM
    mean = jnp.zeros(own.shape, F32)
    for g, win in enumerate(POOL_WINDOWS):
        lo_t = jnp.maximum(tok_q - win // 2, seg_lo)
        hi_t = jnp.minimum(tok_q + win // 2, seg_hi)
        band = jnp.where(tok_k >= lo_t, jnp.where(tok_k < hi_t, 1.0, 0.0), 0.0).astype(BF16)
        count = (hi_t - lo_t)[:, 0:1].astype(F32)
        total = _dot(band, hi) + _dot(band, lo)
        mean = jnp.where(group == g, total / count, mean)
    p = (mean - own).astype(BF16)
    o_ref[0] = (_dot(p, w_ref[...]) * scale_ref[...]).astype(o_ref.dtype)


def _pool_call(u, w_blockdiag, scale):
    b, s, w = u.shape
    return pl.pallas_call(
        functools.partial(_pool_kernel, s),
        grid=(b, s // TOK_TILE),
        in_specs=[pl.BlockSpec((1, s, w), lambda i, t: (i, 0, 0)),
                  pl.BlockSpec((w, w), lambda i, t: (0, 0)),
                  pl.BlockSpec((1, w), lambda i, t: (0, 0))],
        out_specs=pl.BlockSpec((1, TOK_TILE, w), lambda i, t: (i, t, 0)),
        out_shape=jax.ShapeDtypeStruct((b, s, w), BF16),
        compiler_params=_cparams(("parallel", "parallel")),
        name="pool_mix",
    )(u, w_blockdiag, scale.reshape(1, w))


def _outproj_kernel(n_parts, *refs):
    y_refs = refs[:n_parts]
    w_refs = refs[n_parts:2 * n_parts]
    x_ref, mod_ref, norm_ref, x1_ref, hi_ref, lo_ref = refs[2 * n_parts:]
    acc = _dot(y_refs[0][0], w_refs[0][...])
    for y_ref, w_ref in zip(y_refs[1:], w_refs[1:]):
        acc = acc + _dot(y_ref[0], w_ref[...])
    x1 = x_ref[0] + mod_ref[0, 0, 2:3, :] * acc
    x1_ref[0] = x1
    xm = _rms(x1, norm_ref[...]) * (1.0 + mod_ref[0, 0, 4:5, :]) + mod_ref[0, 0, 3:4, :]
    hi, lo = _split(xm)
    hi_ref[0] = hi
    lo_ref[0] = lo


def _outproj_call(parts, weights, x, mods, norm_g):
    b, s, d = x.shape
    tok = lambda w: pl.BlockSpec((1, TOK_TILE, w), lambda i, t: (i, t, 0))
    specs = [tok(p.shape[-1]) for p in parts]
    specs += [pl.BlockSpec(w.shape, lambda i, t: (0, 0)) for w in weights]
    specs += [tok(d),
              pl.BlockSpec((1, 1, N_MOD, d), lambda i, t: (i, jnp.minimum(t, 1), 0, 0)),
              pl.BlockSpec((1, d), lambda i, t: (0, 0))]
    return pl.pallas_call(
        functools.partial(_outproj_kernel, len(parts)),
        grid=(b, s // TOK_TILE),
        in_specs=specs,
        out_specs=[tok(d), tok(d), tok(d)],
        out_shape=[jax.ShapeDtypeStruct((b, s, d), F32), jax.ShapeDtypeStruct((b, s, d), BF16),
                   jax.ShapeDtypeStruct((b, s, d), BF16)],
        compiler_params=_cparams(("parallel", "parallel")),
        name="out_proj",
    )(*parts, *weights, x, mods, norm_g.reshape(1, d))


def _peer_fold_kernel(k_ref, wt_ref, hi_ref, lo_ref):
    k_hi, k_lo = _split(k_ref[0])
    w_hi, w_lo = _split(wt_ref[0])
    kw = _dot3(k_hi, k_lo, w_hi, w_lo)
    hi, lo = _split(kw)
    hi_ref[0] = hi
    lo_ref[0] = lo


def _peer_fold_call(keys_ph, wq_t):
    n, nk, kd = keys_ph.shape
    d = wq_t.shape[-1]
    blk = pl.BlockSpec((1, nk, d), lambda i: (i, 0, 0))
    return pl.pallas_call(
        _peer_fold_kernel,
        grid=(n,),
        in_specs=[pl.BlockSpec((1, nk, kd), lambda i: (i, 0, 0)), pl.BlockSpec((1, kd, d), lambda i: (i, 0, 0))],
        out_specs=[blk, blk],
        out_shape=[jax.ShapeDtypeStruct((n, nk, d), BF16)] * 2,
        compiler_params=_cparams(("parallel",)),
        name="peer_fold",
    )(keys_ph, wq_t)


def _peer_scores_kernel(kw_hi_ref, kw_lo_ref, x_hi_ref, x_lo_ref, st_ref):
    st_ref[...] = _dot3(kw_hi_ref[...], kw_lo_ref[...], x_hi_ref[...], x_lo_ref[...], _NT)


def _peer_scores_call(kw_hi, kw_lo, x_hi, x_lo, tm):
    n, d = x_hi.shape
    r = kw_hi.shape[0]
    kw = pl.BlockSpec((r, d), lambda t: (0, 0))
    xs = pl.BlockSpec((tm, d), lambda t: (t, 0))
    return pl.pallas_call(
        _peer_scores_kernel,
        grid=(n // tm,),
        in_specs=[kw, kw, xs, xs],
        out_specs=pl.BlockSpec((r, tm), lambda t: (0, t)),
        out_shape=jax.ShapeDtypeStruct((r, n), F32),
        compiler_params=_cparams(("parallel",)),
        name="peer_scores",
    )(kw_hi, kw_lo, x_hi, x_lo)


def _bitonic_merge_desc(v):
    n = len(v)
    if n == 1:
        return v
    half = n // 2
    top = [jnp.maximum(v[i], v[i + half]) for i in range(half)]
    bot = [jnp.minimum(v[i], v[i + half]) for i in range(half)]
    return _bitonic_merge_desc(top) + _bitonic_merge_desc(bot)


def _sort_desc(v):
    n = len(v)
    if n == 1:
        return v
    return _bitonic_merge_desc(_sort_desc(v[:n // 2]) + _sort_desc(v[n // 2:])[::-1])


def _merge_top(a, b):
    n = len(a)
    return _bitonic_merge_desc([jnp.maximum(a[i], b[n - 1 - i]) for i in range(n)])


def _peer_select_kernel(st_ref, th_ref, e1_ref, e2_ref):
    k = PEER_TOPK
    nk = PEER_N_KEYS
    half_rows = PEER_HEADS * PEER_N_KEYS
    tops = []
    for p in range(2):
        groups = []
        for g in range(PEER_N_KEYS // k):
            vals = [st_ref[pl.ds(p * half_rows + g * k + j, PEER_HEADS, stride=PEER_N_KEYS), :]
                    for j in range(k)]
            groups.append(_sort_desc(vals))
        while len(groups) > 1:
            groups = [_merge_top(groups[i], groups[i + 1]) for i in range(0, len(groups), 2)]
        tops.append(groups[0])
    t1, t2 = tops
    neg = jnp.full(t1[0].shape, NEG, F32)
    rows = [[t1[i] + t2[j] for j in range(k // (i + 1))] for i in range(k)]
    first = _merge_top(rows[0], rows[1] + [neg] * (k - len(rows[1])))
    rest = [c for row in rows[2:] for c in row]
    rest = _sort_desc(rest + [neg] * (2 * k - len(rest)))[:k]
    top = [jnp.maximum(first[i], rest[k - 1 - i]) for i in range(k)]
    tau = functools.reduce(jnp.minimum, top)
    m1, m2 = t1[0], t2[0]
    rz = 1.0 / functools.reduce(lambda a, c: a + c, [jnp.exp(c - rows[0][0]) for c in top])
    big = jnp.full((nk, LANES), -NEG, F32)
    for h in range(PEER_HEADS):
        hs = slice(h, h + 1)
        s1 = st_ref[h * nk:(h + 1) * nk, :]
        s2 = st_ref[half_rows + h * nk:half_rows + (h + 1) * nk, :]
        th = big
        for j in range(k):
            t2j = t2[j][hs]
            th = jnp.where(s1 + t2j >= tau[hs], t2j, th)
        th_ref[h * nk:(h + 1) * nk, :] = th
        e1_ref[h * nk:(h + 1) * nk, :] = jnp.exp(s1 - m1[hs]) * rz[hs]
        e2_ref[h * nk:(h + 1) * nk, :] = jnp.exp(s2 - m2[hs])


def _peer_select_call(st):
    r, n = st.shape
    out = pl.BlockSpec((r // 2, LANES), lambda t: (0, t))
    return pl.pallas_call(
        _peer_select_kernel,
        grid=(n // LANES,),
        in_specs=[pl.BlockSpec((r, LANES), lambda t: (0, t))],
        out_specs=[out, out, out],
        out_shape=[jax.ShapeDtypeStruct((r // 2, n), F32)] * 3,
        compiler_params=_cparams(("parallel",)),
        name="peer_select",
    )(st)


def _peer_dense_kernel(rows_per_chunk, x_ref, th_ref, e1_ref, s2_ref, e2_ref, u_ref, vt_ref, y_ref,
                       acc_ref, g_ref, ht_ref):
    c = pl.program_id(1)
    tm = x_ref.shape[0]
    nk = PEER_N_KEYS

    @pl.when(c == 0)
    def _():
        acc_ref[...] = jnp.zeros(acc_ref.shape, F32)

    ht_ref[...] = _dot(u_ref[...], x_ref[...], _NT)
    n_sub = nk // PEER_SUB_KEYS

    def block(i, carry):
        cols = pl.ds(pl.multiple_of((i // n_sub) * LANES, LANES), LANES)
        sub = (i % n_sub) * PEER_SUB_KEYS
        w = [jnp.zeros((PEER_SUB_KEYS, LANES), F32)] * rows_per_chunk
        for h in range(PEER_HEADS):
            first = pl.ds(pl.multiple_of(h * nk + c * SUBLANES, SUBLANES), SUBLANES)
            second = pl.ds(pl.multiple_of(h * nk + sub, PEER_SUB_KEYS), PEER_SUB_KEYS)
            th8, e18 = th_ref[first, cols], e1_ref[first, cols]
            s2, e2 = s2_ref[second, cols], e2_ref[second, cols]
            for r in range(rows_per_chunk):
                w[r] = w[r] + jnp.where(s2 >= th8[r:r + 1, :], e18[r:r + 1, :] * e2, 0.0)
        for r in range(rows_per_chunk):
            rows = pl.ds(pl.multiple_of(r * nk + sub, PEER_SUB_KEYS), PEER_SUB_KEYS)
            hv = ht_ref[rows, cols]
            act = 0.5 * hv * (1.0 + lax.erf(hv * SQRT_HALF))
            g_ref[rows, cols] = (w[r] * act).astype(BF16)
        return carry

    lax.fori_loop(0, (tm // LANES) * n_sub, block, 0)
    acc_ref[...] += _dot(vt_ref[...], g_ref[...])

    @pl.when(c == pl.num_programs(1) - 1)
    def _():
        y_ref[...] = acc_ref[...].T


def _peer_dense_call(x_hi, st, th, e1, e2, u_bf, vt_bf, tm, rows_per_chunk):
    n, d = x_hi.shape
    n_exp = u_bf.shape[0]
    ne = rows_per_chunk * PEER_N_KEYS
    half = pl.BlockSpec((th.shape[0], tm), lambda t, c: (0, t))
    return pl.pallas_call(
        functools.partial(_peer_dense_kernel, rows_per_chunk),
        grid=(n // tm, n_exp // ne),
        in_specs=[
            pl.BlockSpec((tm, d), lambda t, c: (t, 0)),
            half, half,
            pl.BlockSpec((th.shape[0], tm), lambda t, c: (1, t)),
            half,
            pl.BlockSpec((ne, d), lambda t, c: (c, 0)),
            pl.BlockSpec((d, ne), lambda t, c: (0, c)),
        ],
        out_specs=pl.BlockSpec((tm, d), lambda t, c: (t, 0)),
        out_shape=jax.ShapeDtypeStruct((n, d), F32),
        scratch_shapes=[pltpu.VMEM((d, tm), F32), pltpu.VMEM((ne, tm), BF16), pltpu.VMEM((ne, tm), F32)],
        compiler_params=_cparams(("parallel", "arbitrary")),
        name="peer_dense",
    )(x_hi, th, e1, st, e2, u_bf, vt_bf)


def _peer(x_hi, x_lo, wq, keys, u, v):
    b, s, d = x_hi.shape
    n = b * s
    heads, _, nk, kd = keys.shape
    keys_ph = keys.transpose(1, 0, 2, 3).reshape(2 * heads, nk, kd)
    wq_t = wq.T.reshape(heads, 2, kd, d).transpose(1, 0, 2, 3).reshape(2 * heads, kd, d)
    kw_hi, kw_lo = _peer_fold_call(keys_ph, wq_t)
    kw_hi = kw_hi.reshape(2 * heads * nk, d)
    kw_lo = kw_lo.reshape(2 * heads * nk, d)
    tm = 512 if n % 512 == 0 else TOK_TILE
    xh = x_hi.reshape(n, d)
    st = _peer_scores_call(kw_hi, kw_lo, xh, x_lo.reshape(n, d), tm)
    th, e1, e2 = _peer_select_call(st)
    y = _peer_dense_call(xh, st, th, e1, e2, u.astype(BF16), v.T.astype(BF16), tm, SUBLANES)
    return y.reshape(b, s, d)


def _final_kernel(x_ref, y_ref, mod_ref, o_ref):
    o_ref[0] = x_ref[0] + mod_ref[0, 0, 5:6, :] * y_ref[0]


def _final_call(x1, y, mods):
    b, s, d = x1.shape
    nt = s // TOK_TILE - 1
    src = pl.BlockSpec((1, TOK_TILE, d), lambda i, t: (i, t + 1, 0))
    return pl.pallas_call(
        _final_kernel,
        grid=(b, nt),
        in_specs=[src, src, pl.BlockSpec((1, 1, N_MOD, d), lambda i, t: (i, 1, 0, 0))],
        out_specs=pl.BlockSpec((1, TOK_TILE, d), lambda i, t: (i, t, 0)),
        out_shape=jax.ShapeDtypeStruct((b, nt * TOK_TILE, d), F32),
        compiler_params=_cparams(("parallel", "parallel")),
        name="final_residual",
    )(x1, y, mods)


def _gain2(g):
    return jnp.concatenate([g, g]).reshape(1, LANES).astype(F32)


def kernel(x, c, ctx, c_ctx, l0_ada_w, l0_ada_b, l0_norm1, l0_norm2, l0_w_in, l0_w_out, l0_pool_w, l0_pool_scale, l0_diff_qnorm, l0_diff_knorm, l0_lambda_q1, l0_lambda_k1, l0_lambda_q2, l0_lambda_k2, l0_diff_subln, l0_peer_wq, l0_peer_keys, l0_peer_u, l0_peer_v, l1_ada_w, l1_ada_b, l1_norm1, l1_norm2, l1_w_in, l1_w_out, l1_gqa_qnorm, l1_gqa_knorm, l1_na_qnorm, l1_na_knorm, l1_na_rpb, l1_peer_wq, l1_peer_keys, l1_peer_u, l1_peer_v):
    b, t_lat, d = x.shape
    ctx_len = ctx.shape[1]
    assert ctx_len == TOK_TILE and t_lat % TOK_TILE == 0 and t_lat // GRID_W >= NA_WIN_ROWS
    s = ctx_len + t_lat
    xs = jnp.concatenate([ctx, x], axis=1)
    cos, sin = _rope_tables(s, ctx_len)
    qk_scale = HEAD_DIM ** -0.5

    mods0 = _modulation(c, c_ctx, l0_ada_w, l0_ada_b)
    w0 = l0_w_in.astype(BF16)
    pw = l0_pool_scale.shape[0]
    dw = DIFF_HEADS * 2 * HEAD_DIM
    roles0 = [("plain_f32", pw, 1.0, False), ("norm_rope", dw, qk_scale, True), ("norm_rope", dw, 1.0, False),
              ("plain", dw, 1.0, True)]
    weights0 = [w0[:, :pw], w0[:, pw:pw + dw], w0[:, pw + dw:pw + 2 * dw], w0[:, pw + 2 * dw:]]
    gains0 = [None, _gain2(l0_diff_qnorm), _gain2(l0_diff_knorm), None]
    u0, q0, k0, v0 = _inproj_call(xs, mods0, l0_norm1, cos, sin, roles0, weights0, gains0)
    lam_init = 0.8 - 0.6 * math.exp(-0.3 * 0)
    lam_vecs = jnp.stack([l0_lambda_q1, l0_lambda_k1, l0_lambda_q2, l0_lambda_k2]).astype(F32)
    o_diff = _diff_attn_call(q0, k0, v0, lam_vecs, l0_diff_subln, lam_init)
    pool_bd = jax.scipy.linalg.block_diag(*[l0_pool_w[g] for g in range(l0_pool_w.shape[0])]).astype(BF16)
    y_pool = _pool_call(u0, pool_bd, l0_pool_scale)
    wo0 = l0_w_out.astype(BF16)
    x1, xh, xl = _outproj_call([y_pool, o_diff], [wo0[:pw], wo0[pw:]], xs, mods0, l0_norm2)
    y_peer0 = _peer(xh, xl, l0_peer_wq, l0_peer_keys, l0_peer_u, l0_peer_v)

    mods1 = _modulation(c, c_ctx, l1_ada_w, l1_ada_b)
    w1 = l1_w_in.astype(BF16)
    n_q = 8 * HEAD_DIM
    n_kv = GQA_KV_HEADS * HEAD_DIM
    o_ck, o_cv, o_nq, o_nk, o_nv = n_q, n_q + n_kv, n_q + 2 * n_kv, 2 * n_q + 2 * n_kv, 3 * n_q + 2 * n_kv
    dup = lambda w: jnp.concatenate([w[:, :HEAD_DIM], w[:, :HEAD_DIM], w[:, HEAD_DIM:], w[:, HEAD_DIM:]], axis=1)
    roles1 = [("norm_rope", n_q, qk_scale, True), ("norm_rope", 2 * n_kv, 1.0, False), ("plain", n_kv, 1.0, True),
              ("norm", n_q, qk_scale, False), ("norm", n_q, 1.0, False), ("plain", n_q, 1.0, False)]
    weights1 = [w1[:, :o_ck], dup(w1[:, o_ck:o_cv]), w1[:, o_cv:o_nq],
                w1[:, o_nq:o_nk], w1[:, o_nk:o_nv], w1[:, o_nv:]]
    gains1 = [_gain2(l1_gqa_qnorm), _gain2(l1_gqa_knorm), None, _gain2(l1_na_qnorm), _gain2(l1_na_knorm), None]
    x2, cq, ckd, cvd, nq, nk_, nv = _inproj_call(x1, mods1, l1_norm1, cos, sin, roles1, weights1, gains1,
                                                 residual=(y_peer0, mods0))
    o_gqa = _gqa_attn_call(cq, ckd, cvd)
    bias = _na_bias_table(l1_na_rpb, t_lat // GRID_W)
    o_na = _na_attn_call(nq, nk_, nv, bias)
    wo1 = l1_w_out.astype(BF16)
    x3, xh1, xl1 = _outproj_call([o_gqa, o_na], [wo1[:n_q], wo1[n_q:]], x2, mods1, l1_norm2)
    y_peer1 = _peer(xh1, xl1, l1_peer_wq, l1_peer_keys, l1_peer_u, l1_peer_v)
    return _final_call(x3, y_peer1, mods1)
```

```python
import functools
import math

import jax
import jax.numpy as jnp
from jax import lax
from jax.experimental import pallas as pl
from jax.experimental.pallas import tpu as pltpu

F32 = jnp.float32
BF16 = jnp.bfloat16

LANES = 128
SUBLANES = 8
VMEM_LIMIT_BYTES = 56 * 1024 * 1024

HEAD_DIM = 64
GRID_W = 64
ROPE_THETA = 10000.0
EPS = 1e-6
N_MOD = 6
POOL_WINDOWS = (2, 4, 8, 16)
POOL_HALO = max(POOL_WINDOWS) // 2
DIFF_HEADS = 6
GQA_KV_HEADS = 2
NA_WIN_ROWS = 8
NA_WIN_COLS = 16
PEER_HEADS = 8
PEER_N_KEYS = 128
PEER_TOPK = 16
PEER_SUB_KEYS = 32
TOK_TILE = 256
KV_CHUNK = 2048
ATTN_CHAINS = 2
ONES_ROWS = 16
LOG2E = 1.4426950408889634
NEG = -1e30
SQRT_HALF = 0.7071067811865476


def _cparams(sem, vmem=VMEM_LIMIT_BYTES):
    return pltpu.CompilerParams(dimension_semantics=sem, vmem_limit_bytes=vmem)


def _split(x):
    hi = x.astype(BF16)
    lo = (x - hi.astype(F32)).astype(BF16)
    return hi, lo


_NN = (((1,), (0,)), ((), ()))
_NT = (((1,), (1,)), ((), ()))


def _dot(a, b, dims=_NN):
    return lax.dot_general(a, b, dims, preferred_element_type=F32)


def _dot3(a_hi, a_lo, b_hi, b_lo, dims=_NN):
    return _dot(a_hi, b_hi, dims) + _dot(a_hi, b_lo, dims) + _dot(a_lo, b_hi, dims)


def _lane_iota(shape):
    return lax.broadcasted_iota(jnp.int32, shape, len(shape) - 1)


def _rms(x, gain):
    ms = jnp.mean(x * x, axis=-1, keepdims=True)
    return x * lax.rsqrt(ms + EPS) * gain


def _ada_kernel(c_ref, w_ref, b_ref, o_ref):
    c = c_ref[...]
    a = c / (1.0 + jnp.exp(-c))
    a_hi, a_lo = _split(a)
    w_hi, w_lo = _split(w_ref[...])
    o_ref[...] = _dot3(a_hi, a_lo, w_hi, w_lo) + b_ref[...]


def _ada_call(cvec, ada_w, ada_b):
    rows, d = cvec.shape
    n = ada_w.shape[1]
    bn = 768
    return pl.pallas_call(
        _ada_kernel,
        grid=(n // bn,),
        in_specs=[
            pl.BlockSpec((rows, d), lambda j: (0, 0)),
            pl.BlockSpec((d, bn), lambda j: (0, j)),
            pl.BlockSpec((1, bn), lambda j: (0, j)),
        ],
        out_specs=pl.BlockSpec((rows, bn), lambda j: (0, j)),
        out_shape=jax.ShapeDtypeStruct((rows, n), F32),
        compiler_params=_cparams(("parallel",)),
        name="ada_mod",
    )(cvec, ada_w, ada_b.reshape(1, n))


def _modulation(c, c_ctx, ada_w, ada_b):
    b, d = c.shape
    rows = -(-(b + 1) // 16) * 16
    cvec = jnp.zeros((rows, d), F32).at[0].set(c_ctx).at[1:b + 1].set(c)
    m = _ada_call(cvec, ada_w, ada_b).reshape(rows, N_MOD, d)
    return jnp.stack([jnp.broadcast_to(m[0], (b, N_MOD, d)), m[1:b + 1]], axis=1)


def _head_norm(y, gain):
    r = lax.broadcasted_iota(jnp.int32, (LANES, LANES), 0) // HEAD_DIM
    c = lax.broadcasted_iota(jnp.int32, (LANES, LANES), 1) // HEAD_DIM
    ones_bd = jnp.where(r == c, 1.0, 0.0).astype(BF16)
    hi, lo = _split(y * y)
    ss = _dot(hi, ones_bd) + _dot(lo, ones_bd)
    return y * lax.rsqrt(ss * (1.0 / HEAD_DIM) + EPS) * gain


def _rope(y, cos, sin):
    up = pltpu.roll(y, LANES - 16, 1)
    down = pltpu.roll(y, 16, 1)
    partner = jnp.where((_lane_iota(y.shape) & 16) == 0, up, down)
    return y * cos + partner * sin


def _inproj_kernel(roles, fuse_residual, *refs):
    refs = list(refs)
    x_ref = refs.pop(0)
    if fuse_residual:
        y_ref = refs.pop(0)
        pmod_ref = refs.pop(0)
    mod_ref = refs.pop(0)
    norm_ref = refs.pop(0)
    cos_ref = refs.pop(0)
    sin_ref = refs.pop(0)
    w_refs, g_refs = [], []
    for kind, _, _, _, _ in roles:
        w_refs.append(refs.pop(0))
        g_refs.append(refs.pop(0) if kind in ("norm", "norm_rope") else None)
    if fuse_residual:
        xo_ref = refs.pop(0)
    out_refs = refs

    x = x_ref[0]
    if fuse_residual:
        x = x + pmod_ref[0, 0, 5:6, :] * y_ref[0]
        xo_ref[0] = x
    xm = _rms(x, norm_ref[...]) * (1.0 + mod_ref[0, 0, 1:2, :]) + mod_ref[0, 0, 0:1, :]
    xm = xm.astype(BF16)
    cos = cos_ref[...]
    sin = sin_ref[...]
    for (kind, width, scale, transposed, unit), w_ref, g_ref, o_ref in zip(roles, w_refs, g_refs, out_refs):
        acc = _dot(xm, w_ref[...])
        if kind in ("plain", "plain_f32") and not transposed:
            o_ref[0] = acc.astype(o_ref.dtype)
            continue
        if unit:
            ones = jnp.ones((ONES_ROWS, TOK_TILE), o_ref.dtype)
            for j in range(width // LANES):
                yt = acc[:, j * LANES:(j + 1) * LANES].T.astype(o_ref.dtype)
                for k in range(LANES // unit):
                    base = (j * (LANES // unit) + k) * (unit + ONES_ROWS)
                    o_ref[0, base:base + unit, :] = yt[k * unit:(k + 1) * unit]
                    o_ref[0, base + unit:base + unit + ONES_ROWS, :] = ones
            continue
        for j in range(width // LANES):
            y = acc[:, j * LANES:(j + 1) * LANES]
            if kind in ("norm", "norm_rope"):
                y = _head_norm(y, g_ref[...])
            if kind == "norm_rope":
                y = _rope(y, cos, sin)
            if scale != 1.0:
                y = y * scale
            if transposed:
                o_ref[0, j * LANES:(j + 1) * LANES, :] = y.T.astype(o_ref.dtype)
            else:
                o_ref[0, :, j * LANES:(j + 1) * LANES] = y.astype(o_ref.dtype)


def _inproj_call(x, mods, norm_g, cos, sin, roles, weights, gains, residual=None):
    b, s, d = x.shape
    nt = s // TOK_TILE
    tok = lambda w: pl.BlockSpec((1, TOK_TILE, w), lambda i, t: (i, t, 0))
    mod_spec = pl.BlockSpec((1, 1, N_MOD, d), lambda i, t: (i, jnp.minimum(t, 1), 0, 0))
    args, specs = [x], [tok(d)]
    if residual is not None:
        y, pmods = residual
        args += [y, pmods]
        specs += [tok(d), mod_spec]
    args += [mods, norm_g.reshape(1, d), cos, sin]
    specs += [mod_spec, pl.BlockSpec((1, d), lambda i, t: (0, 0)),
              pl.BlockSpec((TOK_TILE, LANES), lambda i, t: (t, 0)),
              pl.BlockSpec((TOK_TILE, LANES), lambda i, t: (t, 0))]
    for (kind, width, _, _, _), w, g in zip(roles, weights, gains):
        args.append(w)
        specs.append(pl.BlockSpec((d, width), lambda i, t: (0, 0)))
        if kind in ("norm", "norm_rope"):
            args.append(g)
            specs.append(pl.BlockSpec((1, LANES), lambda i, t: (0, 0)))
    out_shapes, out_specs = [], []
    if residual is not None:
        out_shapes.append(jax.ShapeDtypeStruct((b, s, d), F32))
        out_specs.append(tok(d))
    for kind, width, _, transposed, unit in roles:
        dt = F32 if kind == "plain_f32" else BF16
        if transposed:
            rows = width // unit * (unit + ONES_ROWS) if unit else width
            out_shapes.append(jax.ShapeDtypeStruct((b, rows, s), dt))
            out_specs.append(pl.BlockSpec((1, rows, TOK_TILE), lambda i, t: (i, 0, t)))
        else:
            out_shapes.append(jax.ShapeDtypeStruct((b, s, width), dt))
            out_specs.append(tok(width))
    return pl.pallas_call(
        functools.partial(_inproj_kernel, tuple(roles), residual is not None),
        grid=(b, nt),
        in_specs=specs,
        out_specs=out_specs,
        out_shape=out_shapes,
        compiler_params=_cparams(("parallel", "parallel")),
        name="in_proj",
    )(*args)


def _rope_tables(s_total, ctx_len):
    t = jnp.arange(s_total - ctx_len, dtype=jnp.int32)
    pos = jnp.stack([t // GRID_W, t % GRID_W], axis=-1).astype(F32)
    n_freq = HEAD_DIM // 4
    inv_freq = ROPE_THETA ** (-jnp.arange(n_freq, dtype=F32) / n_freq)
    ang = pos[:, :, None] * inv_freq
    cos, sin = jnp.cos(ang), jnp.sin(ang)
    cos64 = jnp.concatenate([cos[:, 0], cos[:, 0], cos[:, 1], cos[:, 1]], axis=-1)
    sin64 = jnp.concatenate([-sin[:, 0], sin[:, 0], -sin[:, 1], sin[:, 1]], axis=-1)
    cos128 = jnp.concatenate([jnp.ones((ctx_len, LANES), F32), jnp.tile(cos64, (1, 2))], axis=0)
    sin128 = jnp.concatenate([jnp.zeros((ctx_len, LANES), F32), jnp.tile(sin64, (1, 2))], axis=0)
    return cos128, sin128


def _attend_all(chains, k_ref, vt_ref, is_latent):
    t_lat = k_ref.shape[1] - TOK_TILE
    chunk = math.gcd(t_lat, KV_CHUNK)

    def step(keys, carry):
        out = []
        for (qt, key_lanes, value_rows, acc_ref), m_prev in zip(chains, carry):
            s = _dot(k_ref[0, keys, key_lanes], qt)
            m_new = jnp.maximum(m_prev, jnp.max(s, axis=0, keepdims=True))
            p = jnp.exp2(s - m_new)
            acc_ref[...] = jnp.exp2(m_prev - m_new) * acc_ref[...] + _dot(vt_ref[0, value_rows, keys], p.astype(BF16))
            out.append(m_new)
        return tuple(out)

    init = []
    for qt, _, _, acc_ref in chains:
        acc_ref[...] = jnp.zeros(acc_ref.shape, F32)
        init.append(jnp.full((1, qt.shape[1]), NEG, F32))
    carry = step(slice(0, TOK_TILE), tuple(init))

    def body(j, carry):
        return step(pl.ds(pl.multiple_of(TOK_TILE + j * chunk, TOK_TILE), chunk), carry)

    lax.fori_loop(0, jnp.where(is_latent, t_lat // chunk, 0), body, carry)
    outs = []
    for _, _, _, acc_ref in chains:
        dv = acc_ref.shape[0] - ONES_ROWS
        outs.append(acc_ref[0:dv, :] / acc_ref[dv:dv + 1, :])
    return outs


def _half_masks(q):
    lo = _lane_iota(q.shape) < HEAD_DIM
    zero = jnp.zeros_like(q)
    return jnp.where(lo, q, zero), jnp.where(lo, zero, q)


def _row_half_masks(qt):
    top = lax.broadcasted_iota(jnp.int32, qt.shape, 0) < HEAD_DIM
    zero = jnp.zeros_like(qt)
    return jnp.where(top, qt, zero), jnp.where(top, zero, qt)


def _diff_attn_kernel(lam_init, qt_ref, k_ref, vt_ref, lam_ref, subln_ref, o_ref, *acc_refs):
    chains = []
    for j, acc_ref in enumerate(acc_refs):
        lanes = slice(j * LANES, (j + 1) * LANES)
        q1, q2 = _row_half_masks(qt_ref[0, lanes, :])
        values = slice(j * (LANES + ONES_ROWS), (j + 1) * (LANES + ONES_ROWS))
        chains.append((jnp.concatenate([q1, q2], axis=1), lanes, values, acc_ref))
    outs = _attend_all(chains, k_ref, vt_ref, pl.program_id(2) > 0)
    lv = lam_ref[...]
    lam = (jnp.exp(jnp.sum(lv[0:1] * lv[1:2], axis=-1, keepdims=True))
           - jnp.exp(jnp.sum(lv[2:3] * lv[3:4], axis=-1, keepdims=True)) + lam_init)
    for j, o in enumerate(outs):
        od = o[:, :TOK_TILE] - lam * o[:, TOK_TILE:]
        ms = jnp.mean(od * od, axis=0, keepdims=True)
        on = od * lax.rsqrt(ms + EPS) * subln_ref[...] * (1.0 - lam_init)
        o_ref[0, :, j * LANES:(j + 1) * LANES] = on.T.astype(o_ref.dtype)


def _diff_attn_call(qt, k, vt, lam_vecs, subln, lam_init):
    b, s, w = k.shape
    wide = ATTN_CHAINS * LANES
    return pl.pallas_call(
        functools.partial(_diff_attn_kernel, lam_init),
        grid=(b, w // wide, s // TOK_TILE),
        in_specs=[
            pl.BlockSpec((1, wide, TOK_TILE), lambda i, h, t: (i, h, t)),
            pl.BlockSpec((1, s, wide), lambda i, h, t: (i, 0, h)),
            pl.BlockSpec((1, ATTN_CHAINS * (LANES + ONES_ROWS), s), lambda i, h, t: (i, h, 0)),
            pl.BlockSpec((4, HEAD_DIM), lambda i, h, t: (0, 0)),
            pl.BlockSpec((LANES, 1), lambda i, h, t: (0, 0)),
        ],
        out_specs=pl.BlockSpec((1, TOK_TILE, wide), lambda i, h, t: (i, t, h)),
        out_shape=jax.ShapeDtypeStruct((b, s, w), BF16),
        scratch_shapes=[pltpu.VMEM((LANES + ONES_ROWS, 2 * TOK_TILE), F32)] * ATTN_CHAINS,
        compiler_params=_cparams(("parallel", "parallel", "parallel")),
        name="diff_attn",
    )(qt, k, vt, lam_vecs, subln.reshape(LANES, 1))


def _gqa_attn_kernel(qt_ref, k_ref, vt_ref, o_ref, *acc_refs):
    t = TOK_TILE
    chains = []
    for g, acc_ref in enumerate(acc_refs):
        qa = _row_half_masks(qt_ref[0, 2 * g * LANES:(2 * g + 1) * LANES, :])
        qb = _row_half_masks(qt_ref[0, (2 * g + 1) * LANES:(2 * g + 2) * LANES, :])
        qt = jnp.concatenate([qa[0], qa[1], qb[0], qb[1]], axis=1)
        values = slice(g * (HEAD_DIM + ONES_ROWS), (g + 1) * (HEAD_DIM + ONES_ROWS))
        chains.append((qt, slice(g * LANES, (g + 1) * LANES), values, acc_ref))
    outs = _attend_all(chains, k_ref, vt_ref, pl.program_id(1) > 0)
    ot = jnp.concatenate([o[:, j * t:(j + 1) * t] for o in outs for j in range(4)], axis=0)
    o_ref[0] = ot.T.astype(o_ref.dtype)


def _gqa_attn_call(qt, k_dup, vt):
    b, w, s = qt.shape
    groups = w // (2 * LANES)
    return pl.pallas_call(
        _gqa_attn_kernel,
        grid=(b, s // TOK_TILE),
        in_specs=[pl.BlockSpec((1, w, TOK_TILE), lambda i, t: (i, 0, t)),
                  pl.BlockSpec((1, s, groups * LANES), lambda i, t: (i, 0, 0)),
                  pl.BlockSpec((1, groups * (HEAD_DIM + ONES_ROWS), s), lambda i, t: (i, 0, 0))],
        out_specs=pl.BlockSpec((1, TOK_TILE, w), lambda i, t: (i, t, 0)),
        out_shape=jax.ShapeDtypeStruct((b, s, w), BF16),
        scratch_shapes=[pltpu.VMEM((HEAD_DIM + ONES_ROWS, 4 * TOK_TILE), F32)] * groups,
        compiler_params=_cparams(("parallel", "parallel")),
        name="gqa_attn",
    )(qt, k_dup, vt)


def _na_attn_kernel(n_rows, q_ref, k_ref, v_ref, bias_ref, o_ref):
    t = pl.program_id(2)

    @pl.when(t == 0)
    def _():
        o_ref[0] = jnp.zeros(o_ref.shape[1:], o_ref.dtype)

    @pl.when(t > 0)
    def _():
        kctx = k_ref[0, 0:TOK_TILE, :]
        vctx = v_ref[0, 0:TOK_TILE, :]
        win = NA_WIN_ROWS * GRID_W
        rows_per_tile = TOK_TILE // GRID_W
        lo = _lane_iota((GRID_W, LANES)) < HEAD_DIM
        for i in range(rows_per_tile):
            r = (t - 1) * rows_per_tile + i
            r0 = jnp.clip(r - NA_WIN_ROWS // 2, 0, n_rows - NA_WIN_ROWS)
            off = pl.multiple_of(TOK_TILE + r0 * GRID_W, GRID_W)
            q1, q2 = _half_masks(q_ref[0, i * GRID_W:(i + 1) * GRID_W, :])
            q = jnp.concatenate([q1, q2], axis=0)
            s_nb = _dot(q, k_ref[0, pl.ds(off, win), :], _NT) + bias_ref[0, r - r0]
            s_cx = _dot(q, kctx, _NT)
            m = jnp.maximum(jnp.max(s_nb, axis=-1, keepdims=True), jnp.max(s_cx, axis=-1, keepdims=True))
            p_nb = jnp.exp2(s_nb - m)
            p_cx = jnp.exp2(s_cx - m)
            den = jnp.sum(p_nb, axis=-1, keepdims=True) + jnp.sum(p_cx, axis=-1, keepdims=True)
            o = (_dot(p_nb.astype(BF16), v_ref[0, pl.ds(off, win), :]) + _dot(p_cx.astype(BF16), vctx)) / den
            o_ref[0, i * GRID_W:(i + 1) * GRID_W, :] = jnp.where(lo, o[:GRID_W], o[GRID_W:]).astype(o_ref.dtype)


def _na_bias_table(rpb, n_rows):
    kr = NA_WIN_ROWS
    cols = jnp.arange(GRID_W, dtype=jnp.int32)
    c0 = jnp.clip(cols - NA_WIN_COLS // 2, 0, GRID_W - NA_WIN_COLS)
    kc = jnp.arange(GRID_W, dtype=jnp.int32)
    inside = (kc[None, :] >= c0[:, None]) & (kc[None, :] < c0[:, None] + NA_WIN_COLS)
    dc = kc[None, :] - cols[:, None] + (NA_WIN_COLS - 1)
    onehot = ((dc[:, :, None] == jnp.arange(2 * NA_WIN_COLS - 1)) & inside[:, :, None]).astype(F32)
    by_col = jnp.einsum("hrd,ckd->hrck", rpb.astype(F32), onehot, precision=lax.Precision.HIGHEST)
    by_col = jnp.where(inside[None, None], by_col * LOG2E, NEG)
    tab = jnp.stack([by_col[:, NA_WIN_ROWS - 1 - var:2 * NA_WIN_ROWS - 1 - var]
                     for var in range(NA_WIN_ROWS)], axis=1)
    h = rpb.shape[0]
    tab = tab.transpose(0, 1, 3, 2, 4).reshape(h // 2, 2, NA_WIN_ROWS, GRID_W, kr * GRID_W)
    return tab.transpose(0, 2, 1, 3, 4).reshape(h // 2, NA_WIN_ROWS, 2 * GRID_W, kr * GRID_W)


def _na_attn_call(q, k, v, bias):
    b, s, w = q.shape
    n_rows = (s - TOK_TILE) // GRID_W
    seq = pl.BlockSpec((1, s, LANES), lambda i, h, t: (i, 0, h))
    tile = pl.BlockSpec((1, TOK_TILE, LANES), lambda i, h, t: (i, t, h))
    return pl.pallas_call(
        functools.partial(_na_attn_kernel, n_rows),
        grid=(b, w // LANES, s // TOK_TILE),
        in_specs=[tile, seq, seq,
                  pl.BlockSpec((1,) + bias.shape[1:], lambda i, h, t: (h, 0, 0, 0))],
        out_specs=tile,
        out_shape=jax.ShapeDtypeStruct((b, s, w), BF16),
        compiler_params=_cparams(("parallel", "parallel", "parallel")),
        name="na_attn",
    )(q, k, v, bias)


def _pool_kernel(s_total, u_ref, w_ref, scale_ref, o_ref):
    t = pl.program_id(1)
    t0 = t * TOK_TILE
    seg_lo = jnp.where(t == 0, 0, TOK_TILE)
    seg_hi = jnp.where(t == 0, TOK_TILE, s_total)
    span = TOK_TILE + 2 * POOL_HALO
    start = pl.multiple_of(jnp.clip(t0 - POOL_HALO, 0, s_total - span), SUBLANES)
    hi, lo = _split(u_ref[0, pl.ds(start, span), :])
    own = u_ref[0, pl.ds(pl.multiple_of(t0, TOK_TILE), TOK_TILE), :]
    tok_q = t0 + lax.broadcasted_iota(jnp.int32, (TOK_TILE, span), 0)
    tok_k = start + lax.broadcasted_iota(jnp.int32, (TOK_TILE, span), 1)
    group = _lane_iota(own.shape) // HEAD_DIM
    mean = jnp.zeros(own.shape, F32)
    for g, win in enumerate(POOL_WINDOWS):
        lo_t = jnp.maximum(tok_q - win // 2, seg_lo)
        hi_t = jnp.minimum(tok_q + win // 2, seg_hi)
        band = jnp.where(tok_k >= lo_t, jnp.where(tok_k < hi_t, 1.0, 0.0), 0.0).astype(BF16)
        count = (hi_t - lo_t)[:, 0:1].astype(F32)
        total = _dot(band, hi) + _dot(band, lo)
        mean = jnp.where(group == g, total / count, mean)
    p = (mean - own).astype(BF16)
    o_ref[0] = (_dot(p, w_ref[...]) * scale_ref[...]).astype(o_ref.dtype)


def _pool_call(u, w_blockdiag, scale):
    b, s, w = u.shape
    return pl.pallas_call(
        functools.partial(_pool_kernel, s),
        grid=(b, s // TOK_TILE),
        in_specs=[pl.BlockSpec((1, s, w), lambda i, t: (i, 0, 0)),
                  pl.BlockSpec((w, w), lambda i, t: (0, 0)),
                  pl.BlockSpec((1, w), lambda i, t: (0, 0))],
        out_specs=pl.BlockSpec((1, TOK_TILE, w), lambda i, t: (i, t, 0)),
        out_shape=jax.ShapeDtypeStruct((b, s, w), BF16),
        compiler_params=_cparams(("parallel", "parallel")),
        name="pool_mix",
    )(u, w_blockdiag, scale.reshape(1, w))


def _outproj_kernel(n_parts, *refs):
    y_refs = refs[:n_parts]
    w_refs = refs[n_parts:2 * n_parts]
    x_ref, mod_ref, norm_ref, x1_ref, hi_ref, lo_ref = refs[2 * n_parts:]
    acc = _dot(y_refs[0][0], w_refs[0][...])
    for y_ref, w_ref in zip(y_refs[1:], w_refs[1:]):
        acc = acc + _dot(y_ref[0], w_ref[...])
    x1 = x_ref[0] + mod_ref[0, 0, 2:3, :] * acc
    x1_ref[0] = x1
    xm = _rms(x1, norm_ref[...]) * (1.0 + mod_ref[0, 0, 4:5, :]) + mod_ref[0, 0, 3:4, :]
    hi, lo = _split(xm)
    hi_ref[0] = hi
    lo_ref[0] = lo


def _outproj_call(parts, weights, x, mods, norm_g):
    b, s, d = x.shape
    tok = lambda w: pl.BlockSpec((1, TOK_TILE, w), lambda i, t: (i, t, 0))
    specs = [tok(p.shape[-1]) for p in parts]
    specs += [pl.BlockSpec(w.shape, lambda i, t: (0, 0)) for w in weights]
    specs += [tok(d),
              pl.BlockSpec((1, 1, N_MOD, d), lambda i, t: (i, jnp.minimum(t, 1), 0, 0)),
              pl.BlockSpec((1, d), lambda i, t: (0, 0))]
    return pl.pallas_call(
        functools.partial(_outproj_kernel, len(parts)),
        grid=(b, s // TOK_TILE),
        in_specs=specs,
        out_specs=[tok(d), tok(d), tok(d)],
        out_shape=[jax.ShapeDtypeStruct((b, s, d), F32), jax.ShapeDtypeStruct((b, s, d), BF16),
                   jax.ShapeDtypeStruct((b, s, d), BF16)],
        compiler_params=_cparams(("parallel", "parallel")),
        name="out_proj",
    )(*parts, *weights, x, mods, norm_g.reshape(1, d))


def _peer_fold_kernel(k_ref, wt_ref, hi_ref, lo_ref):
    k_hi, k_lo = _split(k_ref[0])
    w_hi, w_lo = _split(wt_ref[0])
    kw = _dot3(k_hi, k_lo, w_hi, w_lo)
    hi, lo = _split(kw)
    hi_ref[0] = hi
    lo_ref[0] = lo


def _peer_fold_call(keys_ph, wq_t):
    n, nk, kd = keys_ph.shape
    d = wq_t.shape[-1]
    blk = pl.BlockSpec((1, nk, d), lambda i: (i, 0, 0))
    return pl.pallas_call(
        _peer_fold_kernel,
        grid=(n,),
        in_specs=[pl.BlockSpec((1, nk, kd), lambda i: (i, 0, 0)), pl.BlockSpec((1, kd, d), lambda i: (i, 0, 0))],
        out_specs=[blk, blk],
        out_shape=[jax.ShapeDtypeStruct((n, nk, d), BF16)] * 2,
        compiler_params=_cparams(("parallel",)),
        name="peer_fold",
    )(keys_ph, wq_t)


def _peer_scores_kernel(kw_hi_ref, kw_lo_ref, x_hi_ref, x_lo_ref, st_ref):
    st_ref[...] = _dot(kw_hi_ref[...], x_hi_ref[...], _NT)


def _peer_scores_call(kw_hi, kw_lo, x_hi, x_lo, tm):
    n, d = x_hi.shape
    r = kw_hi.shape[0]
    kw = pl.BlockSpec((r, d), lambda t: (0, 0))
    xs = pl.BlockSpec((tm, d), lambda t: (t, 0))
    return pl.pallas_call(
        _peer_scores_kernel,
        grid=(n // tm,),
        in_specs=[kw, kw, xs, xs],
        out_specs=pl.BlockSpec((r, tm), lambda t: (0, t)),
        out_shape=jax.ShapeDtypeStruct((r, n), F32),
        compiler_params=_cparams(("parallel",)),
        name="peer_scores",
    )(kw_hi, kw_lo, x_hi, x_lo)


def _bitonic_merge_desc(v):
    n = len(v)
    if n == 1:
        return v
    half = n // 2
    top = [jnp.maximum(v[i], v[i + half]) for i in range(half)]
    bot = [jnp.minimum(v[i], v[i + half]) for i in range(half)]
    return _bitonic_merge_desc(top) + _bitonic_merge_desc(bot)


def _sort_desc(v):
    n = len(v)
    if n == 1:
        return v
    return _bitonic_merge_desc(_sort_desc(v[:n // 2]) + _sort_desc(v[n // 2:])[::-1])


def _merge_top(a, b):
    n = len(a)
    return _bitonic_merge_desc([jnp.maximum(a[i], b[n - 1 - i]) for i in range(n)])


def _peer_select_kernel(st_ref, th_ref, e1_ref, e2_ref):
    k = PEER_TOPK
    nk = PEER_N_KEYS
    half_rows = PEER_HEADS * PEER_N_KEYS
    tops = []
    for p in range(2):
        groups = []
        for g in range(PEER_N_KEYS // k):
            vals = [st_ref[pl.ds(p * half_rows + g * k + j, PEER_HEADS, stride=PEER_N_KEYS), :]
                    for j in range(k)]
            groups.append(_sort_desc(vals))
        while len(groups) > 1:
            groups = [_merge_top(groups[i], groups[i + 1]) for i in range(0, len(groups), 2)]
        tops.append(groups[0])
    t1, t2 = tops
    neg = jnp.full(t1[0].shape, NEG, F32)
    rows = [[t1[i] + t2[j] for j in range(k // (i + 1))] for i in range(k)]
    first = _merge_top(rows[0], rows[1] + [neg] * (k - len(rows[1])))
    rest = [c for row in rows[2:] for c in row]
    rest = _sort_desc(rest + [neg] * (2 * k - len(rest)))[:k]
    top = [jnp.maximum(first[i], rest[k - 1 - i]) for i in range(k)]
    tau = functools.reduce(jnp.minimum, top)
    m1, m2 = t1[0], t2[0]
    rz = 1.0 / functools.reduce(lambda a, c: a + c, [jnp.exp(c - rows[0][0]) for c in top])
    big = jnp.full((nk, LANES), -NEG, F32)
    for h in range(PEER_HEADS):
        hs = slice(h, h + 1)
        s1 = st_ref[h * nk:(h + 1) * nk, :]
        s2 = st_ref[half_rows + h * nk:half_rows + (h + 1) * nk, :]
        th = big
        for j in range(k):
            t2j = t2[j][hs]
            th = jnp.where(s1 + t2j >= tau[hs], t2j, th)
        th_ref[h * nk:(h + 1) * nk, :] = th
        e1_ref[h * nk:(h + 1) * nk, :] = jnp.exp(s1 - m1[hs]) * rz[hs]
        e2_ref[h * nk:(h + 1) * nk, :] = jnp.exp(s2 - m2[hs])


def _peer_select_call(st):
    r, n = st.shape
    out = pl.BlockSpec((r // 2, LANES), lambda t: (0, t))
    return pl.pallas_call(
        _peer_select_kernel,
        grid=(n // LANES,),
        in_specs=[pl.BlockSpec((r, LANES), lambda t: (0, t))],
        out_specs=[out, out, out],
        out_shape=[jax.ShapeDtypeStruct((r // 2, n), F32)] * 3,
        compiler_params=_cparams(("parallel",)),
        name="peer_select",
    )(st)


def _peer_dense_kernel(x_ref, th_ref, e1_ref, s2_ref, e2_ref, u_ref, vt_ref, y_ref, acc_ref, g_ref, ht_ref):
    c = pl.program_id(1)
    tm = x_ref.shape[0]
    nk = PEER_N_KEYS
    half_rows = SUBLANES // 2
    n_sub = nk // PEER_SUB_KEYS

    @pl.when(c == 0)
    def _():
        acc_ref[...] = jnp.zeros(acc_ref.shape, F32)

    ht_ref[...] = _dot(u_ref[...], x_ref[...], _NT)

    def block(i, carry):
        cols = pl.ds(pl.multiple_of((i // n_sub) * LANES, LANES), LANES)
        sub = (i % n_sub) * PEER_SUB_KEYS
        for part in range(2):
            w = [jnp.zeros((PEER_SUB_KEYS, LANES), F32)] * half_rows
            for h in range(PEER_HEADS):
                first = pl.ds(pl.multiple_of(h * nk + c * SUBLANES, SUBLANES), SUBLANES)
                second = pl.ds(pl.multiple_of(h * nk + sub, PEER_SUB_KEYS), PEER_SUB_KEYS)
                th8, e18 = th_ref[first, cols], e1_ref[first, cols]
                s2, e2 = s2_ref[second, cols], e2_ref[second, cols]
                for q in range(half_rows):
                    r = part * half_rows + q
                    w[q] = w[q] + jnp.where(s2 >= th8[r:r + 1, :], e18[r:r + 1, :] * e2, 0.0)
            for q in range(half_rows):
                rows = pl.ds(pl.multiple_of((part * half_rows + q) * nk + sub, PEER_SUB_KEYS), PEER_SUB_KEYS)
                hv = ht_ref[rows, cols]
                act = 0.5 * hv * (1.0 + lax.erf(hv * SQRT_HALF))
                g_ref[rows, cols] = (w[q] * act).astype(BF16)
        return carry

    lax.fori_loop(0, (tm // LANES) * n_sub, block, 0)
    acc_ref[...] += _dot(vt_ref[...], g_ref[...])

    @pl.when(c == pl.num_programs(1) - 1)
    def _():
        y_ref[...] = acc_ref[...].T


def _peer_dense_call(x_hi, st, th, e1, e2, u_bf, vt_bf, tm):
    n, d = x_hi.shape
    n_exp = u_bf.shape[0]
    ne = SUBLANES * PEER_N_KEYS
    half = pl.BlockSpec((th.shape[0], tm), lambda t, c: (0, t))
    return pl.pallas_call(
        _peer_dense_kernel,
        grid=(n // tm, n_exp // ne),
        in_specs=[
            pl.BlockSpec((tm, d), lambda t, c: (t, 0)),
            half, half,
            pl.BlockSpec((th.shape[0], tm), lambda t, c: (1, t)),
            half,
            pl.BlockSpec((ne, d), lambda t, c: (c, 0)),
            pl.BlockSpec((d, ne), lambda t, c: (0, c)),
        ],
        out_specs=pl.BlockSpec((tm, d), lambda t, c: (t, 0)),
        out_shape=jax.ShapeDtypeStruct((n, d), F32),
        scratch_shapes=[pltpu.VMEM((d, tm), F32), pltpu.VMEM((ne, tm), BF16), pltpu.VMEM((ne, tm), F32)],
        compiler_params=_cparams(("parallel", "arbitrary")),
        name="peer_dense",
    )(x_hi, th, e1, st, e2, u_bf, vt_bf)


def _peer(x_hi, x_lo, wq, keys, u, v):
    b, s, d = x_hi.shape
    n = b * s
    heads, _, nk, kd = keys.shape
    keys_ph = keys.transpose(1, 0, 2, 3).reshape(2 * heads, nk, kd)
    wq_t = wq.T.reshape(heads, 2, kd, d).transpose(1, 0, 2, 3).reshape(2 * heads, kd, d)
    kw_hi, kw_lo = _peer_fold_call(keys_ph, wq_t)
    kw_hi = kw_hi.reshape(2 * heads * nk, d)
    kw_lo = kw_lo.reshape(2 * heads * nk, d)
    tm = 512 if n % 512 == 0 else TOK_TILE
    xh = x_hi.reshape(n, d)
    st = _peer_scores_call(kw_hi, kw_lo, xh, x_lo.reshape(n, d), tm)
    th, e1, e2 = _peer_select_call(st)
    y = _peer_dense_call(xh, st, th, e1, e2, u.astype(BF16), v.T.astype(BF16), tm)
    return y.reshape(b, s, d)


def _final_kernel(x_ref, y_ref, mod_ref, o_ref):
    o_ref[0] = x_ref[0] + mod_ref[0, 0, 5:6, :] * y_ref[0]


def _final_call(x1, y, mods):
    b, s, d = x1.shape
    nt = s // TOK_TILE - 1
    src = pl.BlockSpec((1, TOK_TILE, d), lambda i, t: (i, t + 1, 0))
    return pl.pallas_call(
        _final_kernel,
        grid=(b, nt),
        in_specs=[src, src, pl.BlockSpec((1, 1, N_MOD, d), lambda i, t: (i, 1, 0, 0))],
        out_specs=pl.BlockSpec((1, TOK_TILE, d), lambda i, t: (i, t, 0)),
        out_shape=jax.ShapeDtypeStruct((b, nt * TOK_TILE, d), F32),
        compiler_params=_cparams(("parallel", "parallel")),
        name="final_residual",
    )(x1, y, mods)


def _gain2(g):
    return jnp.concatenate([g, g]).reshape(1, LANES).astype(F32)


def kernel(x, c, ctx, c_ctx, l0_ada_w, l0_ada_b, l0_norm1, l0_norm2, l0_w_in, l0_w_out, l0_pool_w, l0_pool_scale, l0_diff_qnorm, l0_diff_knorm, l0_lambda_q1, l0_lambda_k1, l0_lambda_q2, l0_lambda_k2, l0_diff_subln, l0_peer_wq, l0_peer_keys, l0_peer_u, l0_peer_v, l1_ada_w, l1_ada_b, l1_norm1, l1_norm2, l1_w_in, l1_w_out, l1_gqa_qnorm, l1_gqa_knorm, l1_na_qnorm, l1_na_knorm, l1_na_rpb, l1_peer_wq, l1_peer_keys, l1_peer_u, l1_peer_v):
    b, t_lat, d = x.shape
    ctx_len = ctx.shape[1]
    assert ctx_len == TOK_TILE and t_lat % TOK_TILE == 0 and t_lat // GRID_W >= NA_WIN_ROWS
    s = ctx_len + t_lat
    xs = jnp.concatenate([ctx, x], axis=1)
    cos, sin = _rope_tables(s, ctx_len)
    qk_scale = HEAD_DIM ** -0.5 * LOG2E

    mods0 = _modulation(c, c_ctx, l0_ada_w, l0_ada_b)
    w0 = l0_w_in.astype(BF16)
    pw = l0_pool_scale.shape[0]
    dw = DIFF_HEADS * 2 * HEAD_DIM
    roles0 = [("plain_f32", pw, 1.0, False, 0), ("norm_rope", dw, qk_scale, True, 0),
              ("norm_rope", dw, 1.0, False, 0), ("plain", dw, 1.0, True, LANES)]
    weights0 = [w0[:, :pw], w0[:, pw:pw + dw], w0[:, pw + dw:pw + 2 * dw], w0[:, pw + 2 * dw:]]
    gains0 = [None, _gain2(l0_diff_qnorm), _gain2(l0_diff_knorm), None]
    u0, q0, k0, v0 = _inproj_call(xs, mods0, l0_norm1, cos, sin, roles0, weights0, gains0)
    lam_init = 0.8 - 0.6 * math.exp(-0.3 * 0)
    lam_vecs = jnp.stack([l0_lambda_q1, l0_lambda_k1, l0_lambda_q2, l0_lambda_k2]).astype(F32)
    o_diff = _diff_attn_call(q0, k0, v0, lam_vecs, l0_diff_subln, lam_init)
    pool_bd = jax.scipy.linalg.block_diag(*[l0_pool_w[g] for g in range(l0_pool_w.shape[0])]).astype(BF16)
    y_pool = _pool_call(u0, pool_bd, l0_pool_scale)
    wo0 = l0_w_out.astype(BF16)
    x1, xh, xl = _outproj_call([y_pool, o_diff], [wo0[:pw], wo0[pw:]], xs, mods0, l0_norm2)
    y_peer0 = _peer(xh, xl, l0_peer_wq, l0_peer_keys, l0_peer_u, l0_peer_v)

    mods1 = _modulation(c, c_ctx, l1_ada_w, l1_ada_b)
    w1 = l1_w_in.astype(BF16)
    n_q = 8 * HEAD_DIM
    n_kv = GQA_KV_HEADS * HEAD_DIM
    o_ck, o_cv, o_nq, o_nk, o_nv = n_q, n_q + n_kv, n_q + 2 * n_kv, 2 * n_q + 2 * n_kv, 3 * n_q + 2 * n_kv
    dup = lambda w: jnp.concatenate([w[:, :HEAD_DIM], w[:, :HEAD_DIM], w[:, HEAD_DIM:], w[:, HEAD_DIM:]], axis=1)
    roles1 = [("norm_rope", n_q, qk_scale, True, 0), ("norm_rope", 2 * n_kv, 1.0, False, 0),
              ("plain", n_kv, 1.0, True, HEAD_DIM), ("norm", n_q, qk_scale, False, 0), ("norm", n_q, 1.0, False, 0),
              ("plain", n_q, 1.0, False, 0)]
    weights1 = [w1[:, :o_ck], dup(w1[:, o_ck:o_cv]), w1[:, o_cv:o_nq],
                w1[:, o_nq:o_nk], w1[:, o_nk:o_nv], w1[:, o_nv:]]
    gains1 = [_gain2(l1_gqa_qnorm), _gain2(l1_gqa_knorm), None, _gain2(l1_na_qnorm), _gain2(l1_na_knorm), None]
    x2, cq, ckd, cvd, nq, nk_, nv = _inproj_call(x1, mods1, l1_norm1, cos, sin, roles1, weights1, gains1,
                                                 residual=(y_peer0, mods0))
    o_gqa = _gqa_attn_call(cq, ckd, cvd)
    bias = _na_bias_table(l1_na_rpb, t_lat // GRID_W)
    o_na = _na_attn_call(nq, nk_, nv, bias)
    wo1 = l1_w_out.astype(BF16)
    x3, xh1, xl1 = _outproj_call([o_gqa, o_na], [wo1[:n_q], wo1[n_q:]], x2, mods1, l1_norm2)
    y_peer1 = _peer(xh1, xl1, l1_peer_wq, l1_peer_keys, l1_peer_u, l1_peer_v)
    return _final_call(x3, y_peer1, mods1)
```

```python
import functools
import math

import jax
import jax.numpy as jnp
from jax import lax
from jax.experimental import pallas as pl
from jax.experimental.pallas import tpu as pltpu

F32 = jnp.float32
BF16 = jnp.bfloat16

LANES = 128
SUBLANES = 8
VMEM_LIMIT_BYTES = 56 * 1024 * 1024

HEAD_DIM = 64
GRID_W = 64
ROPE_THETA = 10000.0
EPS = 1e-6
N_MOD = 6
POOL_WINDOWS = (2, 4, 8, 16)
POOL_HALO = max(POOL_WINDOWS) // 2
DIFF_HEADS = 6
GQA_KV_HEADS = 2
NA_WIN_ROWS = 8
NA_WIN_COLS = 16
PEER_HEADS = 8
PEER_N_KEYS = 128
PEER_TOPK = 16
TOK_TILE = 256
KV_CHUNK = 2048
PEER_SUB_KEYS = 32
ATTN_CHAINS = 2
ONES_ROWS = 16
LOG2E = 1.4426950408889634
LAZY_EXP_LIMIT = 60.0
NORM_SLACK = 1.05
NEG = -1e30
SQRT_HALF = 0.7071067811865476


def _cparams(sem, vmem=VMEM_LIMIT_BYTES):
    return pltpu.CompilerParams(dimension_semantics=sem, vmem_limit_bytes=vmem)


def _split(x):
    hi = x.astype(BF16)
    lo = (x - hi.astype(F32)).astype(BF16)
    return hi, lo


_NN = (((1,), (0,)), ((), ()))
_NT = (((1,), (1,)), ((), ()))


def _dot(a, b, dims=_NN):
    return lax.dot_general(a, b, dims, preferred_element_type=F32)


def _dot3(a_hi, a_lo, b_hi, b_lo, dims=_NN):
    return _dot(a_hi, b_hi, dims) + _dot(a_hi, b_lo, dims) + _dot(a_lo, b_hi, dims)


def _lane_iota(shape):
    return lax.broadcasted_iota(jnp.int32, shape, len(shape) - 1)


def _rms(x, gain):
    ms = jnp.mean(x * x, axis=-1, keepdims=True)
    return x * lax.rsqrt(ms + EPS) * gain


def _ada_kernel(c_ref, w_ref, b_ref, o_ref):
    c = c_ref[...]
    a = c / (1.0 + jnp.exp(-c))
    a_hi, a_lo = _split(a)
    w_hi, w_lo = _split(w_ref[...])
    o_ref[...] = _dot3(a_hi, a_lo, w_hi, w_lo) + b_ref[...]


def _ada_call(cvec, ada_w, ada_b):
    rows, d = cvec.shape
    n = ada_w.shape[1]
    bn = 768
    return pl.pallas_call(
        _ada_kernel,
        grid=(n // bn,),
        in_specs=[
            pl.BlockSpec((rows, d), lambda j: (0, 0)),
            pl.BlockSpec((d, bn), lambda j: (0, j)),
            pl.BlockSpec((1, bn), lambda j: (0, j)),
        ],
        out_specs=pl.BlockSpec((rows, bn), lambda j: (0, j)),
        out_shape=jax.ShapeDtypeStruct((rows, n), F32),
        compiler_params=_cparams(("parallel",)),
        name="ada_mod",
    )(cvec, ada_w, ada_b.reshape(1, n))


def _modulation(c, c_ctx, ada_w, ada_b):
    b, d = c.shape
    rows = -(-(b + 1) // 16) * 16
    cvec = jnp.zeros((rows, d), F32).at[0].set(c_ctx).at[1:b + 1].set(c)
    m = _ada_call(cvec, ada_w, ada_b).reshape(rows, N_MOD, d)
    return jnp.stack([jnp.broadcast_to(m[0], (b, N_MOD, d)), m[1:b + 1]], axis=1)


def _head_norm(y, gain):
    r = lax.broadcasted_iota(jnp.int32, (LANES, LANES), 0) // HEAD_DIM
    c = lax.broadcasted_iota(jnp.int32, (LANES, LANES), 1) // HEAD_DIM
    ones_bd = jnp.where(r == c, 1.0, 0.0).astype(BF16)
    hi, lo = _split(y * y)
    ss = _dot(hi, ones_bd) + _dot(lo, ones_bd)
    return y * lax.rsqrt(ss * (1.0 / HEAD_DIM) + EPS) * gain


def _rope(y, cos, sin):
    up = pltpu.roll(y, LANES - 16, 1)
    down = pltpu.roll(y, 16, 1)
    partner = jnp.where((_lane_iota(y.shape) & 16) == 0, up, down)
    return y * cos + partner * sin


def _inproj_kernel(roles, fuse_residual, *refs):
    refs = list(refs)
    x_ref = refs.pop(0)
    if fuse_residual:
        y_ref = refs.pop(0)
        pmod_ref = refs.pop(0)
    mod_ref = refs.pop(0)
    norm_ref = refs.pop(0)
    cos_ref = refs.pop(0)
    sin_ref = refs.pop(0)
    w_refs, g_refs = [], []
    for kind, _, _, _, _ in roles:
        w_refs.append(refs.pop(0))
        g_refs.append(refs.pop(0) if kind in ("norm", "norm_rope") else None)
    if fuse_residual:
        xo_ref = refs.pop(0)
    out_refs = refs

    x = x_ref[0]
    if fuse_residual:
        x = x + pmod_ref[0, 0, 5:6, :] * y_ref[0]
        xo_ref[0] = x
    xm = _rms(x, norm_ref[...]) * (1.0 + mod_ref[0, 0, 1:2, :]) + mod_ref[0, 0, 0:1, :]
    xm = xm.astype(BF16)
    cos = cos_ref[...]
    sin = sin_ref[...]
    for (kind, width, scale, transposed, unit), w_ref, g_ref, o_ref in zip(roles, w_refs, g_refs, out_refs):
        acc = _dot(xm, w_ref[...])
        if kind in ("plain", "plain_f32") and not transposed:
            o_ref[0] = acc.astype(o_ref.dtype)
            continue
        if unit:
            ones = jnp.ones((ONES_ROWS, TOK_TILE), o_ref.dtype)
            for j in range(width // LANES):
                yt = acc[:, j * LANES:(j + 1) * LANES].T.astype(o_ref.dtype)
                for k in range(LANES // unit):
                    base = (j * (LANES // unit) + k) * (unit + ONES_ROWS)
                    o_ref[0, base:base + unit, :] = yt[k * unit:(k + 1) * unit]
                    o_ref[0, base + unit:base + unit + ONES_ROWS, :] = ones
            continue
        for j in range(width // LANES):
            y = acc[:, j * LANES:(j + 1) * LANES]
            if kind in ("norm", "norm_rope"):
                y = _head_norm(y, g_ref[...])
            if kind == "norm_rope":
                y = _rope(y, cos, sin)
            if scale != 1.0:
                y = y * scale
            if transposed:
                o_ref[0, j * LANES:(j + 1) * LANES, :] = y.T.astype(o_ref.dtype)
            else:
                o_ref[0, :, j * LANES:(j + 1) * LANES] = y.astype(o_ref.dtype)


def _inproj_call(x, mods, norm_g, cos, sin, roles, weights, gains, residual=None):
    b, s, d = x.shape
    nt = s // TOK_TILE
    tok = lambda w: pl.BlockSpec((1, TOK_TILE, w), lambda i, t: (i, t, 0))
    mod_spec = pl.BlockSpec((1, 1, N_MOD, d), lambda i, t: (i, jnp.minimum(t, 1), 0, 0))
    args, specs = [x], [tok(d)]
    if residual is not None:
        y, pmods = residual
        args += [y, pmods]
        specs += [tok(d), mod_spec]
    args += [mods, norm_g.reshape(1, d), cos, sin]
    specs += [mod_spec, pl.BlockSpec((1, d), lambda i, t: (0, 0)),
              pl.BlockSpec((TOK_TILE, LANES), lambda i, t: (t, 0)),
              pl.BlockSpec((TOK_TILE, LANES), lambda i, t: (t, 0))]
    for (kind, width, _, _, _), w, g in zip(roles, weights, gains):
        args.append(w)
        specs.append(pl.BlockSpec((d, width), lambda i, t: (0, 0)))
        if kind in ("norm", "norm_rope"):
            args.append(g)
            specs.append(pl.BlockSpec((1, LANES), lambda i, t: (0, 0)))
    out_shapes, out_specs = [], []
    if residual is not None:
        out_shapes.append(jax.ShapeDtypeStruct((b, s, d), F32))
        out_specs.append(tok(d))
    for kind, width, _, transposed, unit in roles:
        dt = F32 if kind == "plain_f32" else BF16
        if transposed:
            rows = width // unit * (unit + ONES_ROWS) if unit else width
            out_shapes.append(jax.ShapeDtypeStruct((b, rows, s), dt))
            out_specs.append(pl.BlockSpec((1, rows, TOK_TILE), lambda i, t: (i, 0, t)))
        else:
            out_shapes.append(jax.ShapeDtypeStruct((b, s, width), dt))
            out_specs.append(tok(width))
    return pl.pallas_call(
        functools.partial(_inproj_kernel, tuple(roles), residual is not None),
        grid=(b, nt),
        in_specs=specs,
        out_specs=out_specs,
        out_shape=out_shapes,
        compiler_params=_cparams(("parallel", "parallel")),
        name="in_proj",
    )(*args)


def _rope_tables(s_total, ctx_len):
    t = jnp.arange(s_total - ctx_len, dtype=jnp.int32)
    pos = jnp.stack([t // GRID_W, t % GRID_W], axis=-1).astype(F32)
    n_freq = HEAD_DIM // 4
    inv_freq = ROPE_THETA ** (-jnp.arange(n_freq, dtype=F32) / n_freq)
    ang = pos[:, :, None] * inv_freq
    cos, sin = jnp.cos(ang), jnp.sin(ang)
    cos64 = jnp.concatenate([cos[:, 0], cos[:, 0], cos[:, 1], cos[:, 1]], axis=-1)
    sin64 = jnp.concatenate([-sin[:, 0], sin[:, 0], -sin[:, 1], sin[:, 1]], axis=-1)
    cos128 = jnp.concatenate([jnp.ones((ctx_len, LANES), F32), jnp.tile(cos64, (1, 2))], axis=0)
    sin128 = jnp.concatenate([jnp.zeros((ctx_len, LANES), F32), jnp.tile(sin64, (1, 2))], axis=0)
    return cos128, sin128


def _attend_all(chains, k_ref, vt_ref, is_latent, lazy_ok):
    t_lat = k_ref.shape[1] - TOK_TILE
    chunk = math.gcd(t_lat, KV_CHUNK)

    def step(keys, carry):
        out = []
        for (qt, key_lanes, value_rows, acc_ref), m_prev in zip(chains, carry):
            s = _dot(k_ref[0, keys, key_lanes], qt)
            m_new = jnp.maximum(m_prev, jnp.max(s, axis=0, keepdims=True))
            p = jnp.exp2(s - m_new)
            acc_ref[...] = jnp.exp2(m_prev - m_new) * acc_ref[...] + _dot(vt_ref[0, value_rows, keys], p.astype(BF16))
            out.append(m_new)
        return tuple(out)

    init = []
    for qt, _, _, acc_ref in chains:
        acc_ref[...] = jnp.zeros(acc_ref.shape, F32)
        init.append(jnp.full((1, qt.shape[1]), NEG, F32))
    carry = step(slice(0, TOK_TILE), tuple(init))

    def lazy_step(keys, carry):
        out = []
        for (qt, key_lanes, value_rows, acc_ref), m_prev in zip(chains, carry):
            s = _dot(k_ref[0, keys, key_lanes], qt)
            p = jnp.exp2(s - m_prev)
            m_new = jnp.maximum(m_prev, jnp.max(s, axis=0, keepdims=True))
            acc_ref[...] = jnp.exp2(m_prev - m_new) * (acc_ref[...] + _dot(vt_ref[0, value_rows, keys], p.astype(BF16)))
            out.append(m_new)
        return tuple(out)

    def latent_chunks(step_fn):
        def body(j, carry):
            return step_fn(pl.ds(pl.multiple_of(TOK_TILE + j * chunk, TOK_TILE), chunk), carry)
        lax.fori_loop(0, jnp.where(is_latent, t_lat // chunk, 0), body, carry)

    pl.when(lazy_ok)(lambda: latent_chunks(lazy_step))
    pl.when(jnp.logical_not(lazy_ok))(lambda: latent_chunks(step))
    outs = []
    for _, _, _, acc_ref in chains:
        dv = acc_ref.shape[0] - ONES_ROWS
        outs.append(acc_ref[0:dv, :] / acc_ref[dv:dv + 1, :])
    return outs


def _lazy_softmax_ok(q_gain, k_gain, q_scale):
    bound = HEAD_DIM * jnp.max(jnp.abs(q_gain)) * jnp.max(jnp.abs(k_gain)) * q_scale
    return (2.0 * NORM_SLACK * bound <= LAZY_EXP_LIMIT).astype(jnp.int32).reshape(1)


def _half_masks(q):
    lo = _lane_iota(q.shape) < HEAD_DIM
    zero = jnp.zeros_like(q)
    return jnp.where(lo, q, zero), jnp.where(lo, zero, q)


def _row_half_masks(qt):
    top = lax.broadcasted_iota(jnp.int32, qt.shape, 0) < HEAD_DIM
    zero = jnp.zeros_like(qt)
    return jnp.where(top, qt, zero), jnp.where(top, zero, qt)


def _diff_attn_kernel(lam_init, lazy_ref, qt_ref, k_ref, vt_ref, lam_ref, subln_ref, o_ref, *acc_refs):
    chains = []
    for j, acc_ref in enumerate(acc_refs):
        lanes = slice(j * LANES, (j + 1) * LANES)
        q1, q2 = _row_half_masks(qt_ref[0, lanes, :])
        values = slice(j * (LANES + ONES_ROWS), (j + 1) * (LANES + ONES_ROWS))
        chains.append((jnp.concatenate([q1, q2], axis=1), lanes, values, acc_ref))
    outs = _attend_all(chains, k_ref, vt_ref, pl.program_id(2) > 0, lazy_ref[0] > 0)
    lv = lam_ref[...]
    lam = (jnp.exp(jnp.sum(lv[0:1] * lv[1:2], axis=-1, keepdims=True))
           - jnp.exp(jnp.sum(lv[2:3] * lv[3:4], axis=-1, keepdims=True)) + lam_init)
    for j, o in enumerate(outs):
        od = o[:, :TOK_TILE] - lam * o[:, TOK_TILE:]
        ms = jnp.mean(od * od, axis=0, keepdims=True)
        on = od * lax.rsqrt(ms + EPS) * subln_ref[...] * (1.0 - lam_init)
        o_ref[0, :, j * LANES:(j + 1) * LANES] = on.T.astype(o_ref.dtype)


def _diff_attn_call(lazy_ok, qt, k, vt, lam_vecs, subln, lam_init):
    b, s, w = k.shape
    wide = ATTN_CHAINS * LANES
    return pl.pallas_call(
        functools.partial(_diff_attn_kernel, lam_init),
        grid=(b, w // wide, s // TOK_TILE),
        in_specs=[
            pl.BlockSpec(memory_space=pltpu.SMEM),
            pl.BlockSpec((1, wide, TOK_TILE), lambda i, h, t: (i, h, t)),
            pl.BlockSpec((1, s, wide), lambda i, h, t: (i, 0, h)),
            pl.BlockSpec((1, ATTN_CHAINS * (LANES + ONES_ROWS), s), lambda i, h, t: (i, h, 0)),
            pl.BlockSpec((4, HEAD_DIM), lambda i, h, t: (0, 0)),
            pl.BlockSpec((LANES, 1), lambda i, h, t: (0, 0)),
        ],
        out_specs=pl.BlockSpec((1, TOK_TILE, wide), lambda i, h, t: (i, t, h)),
        out_shape=jax.ShapeDtypeStruct((b, s, w), BF16),
        scratch_shapes=[pltpu.VMEM((LANES + ONES_ROWS, 2 * TOK_TILE), F32)] * ATTN_CHAINS,
        compiler_params=_cparams(("parallel", "parallel", "parallel")),
        name="diff_attn",
    )(lazy_ok, qt, k, vt, lam_vecs, subln.reshape(LANES, 1))


def _gqa_attn_kernel(lazy_ref, qt_ref, k_ref, vt_ref, o_ref, *acc_refs):
    t = TOK_TILE
    chains = []
    for g, acc_ref in enumerate(acc_refs):
        qa = _row_half_masks(qt_ref[0, 2 * g * LANES:(2 * g + 1) * LANES, :])
        qb = _row_half_masks(qt_ref[0, (2 * g + 1) * LANES:(2 * g + 2) * LANES, :])
        qt = jnp.concatenate([qa[0], qa[1], qb[0], qb[1]], axis=1)
        values = slice(g * (HEAD_DIM + ONES_ROWS), (g + 1) * (HEAD_DIM + ONES_ROWS))
        chains.append((qt, slice(g * LANES, (g + 1) * LANES), values, acc_ref))
    outs = _attend_all(chains, k_ref, vt_ref, pl.program_id(1) > 0, lazy_ref[0] > 0)
    ot = jnp.concatenate([o[:, j * t:(j + 1) * t] for o in outs for j in range(4)], axis=0)
    o_ref[0] = ot.T.astype(o_ref.dtype)


def _gqa_attn_call(lazy_ok, qt, k_dup, vt):
    b, w, s = qt.shape
    groups = w // (2 * LANES)
    return pl.pallas_call(
        _gqa_attn_kernel,
        grid=(b, s // TOK_TILE),
        in_specs=[pl.BlockSpec(memory_space=pltpu.SMEM),
                  pl.BlockSpec((1, w, TOK_TILE), lambda i, t: (i, 0, t)),
                  pl.BlockSpec((1, s, groups * LANES), lambda i, t: (i, 0, 0)),
                  pl.BlockSpec((1, groups * (HEAD_DIM + ONES_ROWS), s), lambda i, t: (i, 0, 0))],
        out_specs=pl.BlockSpec((1, TOK_TILE, w), lambda i, t: (i, t, 0)),
        out_shape=jax.ShapeDtypeStruct((b, s, w), BF16),
        scratch_shapes=[pltpu.VMEM((HEAD_DIM + ONES_ROWS, 4 * TOK_TILE), F32)] * groups,
        compiler_params=_cparams(("parallel", "parallel")),
        name="gqa_attn",
    )(lazy_ok, qt, k_dup, vt)


def _na_attn_kernel(n_rows, q_ref, k_ref, v_ref, bias_ref, o_ref):
    t = pl.program_id(2)

    @pl.when(t == 0)
    def _():
        o_ref[0] = jnp.zeros(o_ref.shape[1:], o_ref.dtype)

    @pl.when(t > 0)
    def _():
        kctx = k_ref[0, 0:TOK_TILE, :]
        vctx = v_ref[0, 0:TOK_TILE, :]
        win = NA_WIN_ROWS * GRID_W
        rows_per_tile = TOK_TILE // GRID_W
        lo = _lane_iota((GRID_W, LANES)) < HEAD_DIM
        for i in range(rows_per_tile):
            r = (t - 1) * rows_per_tile + i
            r0 = jnp.clip(r - NA_WIN_ROWS // 2, 0, n_rows - NA_WIN_ROWS)
            off = pl.multiple_of(TOK_TILE + r0 * GRID_W, GRID_W)
            q1, q2 = _half_masks(q_ref[0, i * GRID_W:(i + 1) * GRID_W, :])
            q = jnp.concatenate([q1, q2], axis=0)
            s_nb = _dot(q, k_ref[0, pl.ds(off, win), :], _NT) + bias_ref[0, r - r0]
            s_cx = _dot(q, kctx, _NT)
            m = jnp.maximum(jnp.max(s_nb, axis=-1, keepdims=True), jnp.max(s_cx, axis=-1, keepdims=True))
            p_nb = jnp.exp2(s_nb - m)
            p_cx = jnp.exp2(s_cx - m)
            den = jnp.sum(p_nb, axis=-1, keepdims=True) + jnp.sum(p_cx, axis=-1, keepdims=True)
            o = (_dot(p_nb.astype(BF16), v_ref[0, pl.ds(off, win), :]) + _dot(p_cx.astype(BF16), vctx)) / den
            o_ref[0, i * GRID_W:(i + 1) * GRID_W, :] = jnp.where(lo, o[:GRID_W], o[GRID_W:]).astype(o_ref.dtype)


def _na_bias_table(rpb, n_rows):
    kr = NA_WIN_ROWS
    cols = jnp.arange(GRID_W, dtype=jnp.int32)
    c0 = jnp.clip(cols - NA_WIN_COLS // 2, 0, GRID_W - NA_WIN_COLS)
    kc = jnp.arange(GRID_W, dtype=jnp.int32)
    inside = (kc[None, :] >= c0[:, None]) & (kc[None, :] < c0[:, None] + NA_WIN_COLS)
    dc = kc[None, :] - cols[:, None] + (NA_WIN_COLS - 1)
    onehot = ((dc[:, :, None] == jnp.arange(2 * NA_WIN_COLS - 1)) & inside[:, :, None]).astype(F32)
    by_col = jnp.einsum("hrd,ckd->hrck", rpb.astype(F32), onehot, precision=lax.Precision.HIGHEST)
    by_col = jnp.where(inside[None, None], by_col * LOG2E, NEG)
    tab = jnp.stack([by_col[:, NA_WIN_ROWS - 1 - var:2 * NA_WIN_ROWS - 1 - var]
                     for var in range(NA_WIN_ROWS)], axis=1)
    h = rpb.shape[0]
    tab = tab.transpose(0, 1, 3, 2, 4).reshape(h // 2, 2, NA_WIN_ROWS, GRID_W, kr * GRID_W)
    return tab.transpose(0, 2, 1, 3, 4).reshape(h // 2, NA_WIN_ROWS, 2 * GRID_W, kr * GRID_W)


def _na_attn_call(q, k, v, bias):
    b, s, w = q.shape
    n_rows = (s - TOK_TILE) // GRID_W
    seq = pl.BlockSpec((1, s, LANES), lambda i, h, t: (i, 0, h))
    tile = pl.BlockSpec((1, TOK_TILE, LANES), lambda i, h, t: (i, t, h))
    return pl.pallas_call(
        functools.partial(_na_attn_kernel, n_rows),
        grid=(b, w // LANES, s // TOK_TILE),
        in_specs=[tile, seq, seq,
                  pl.BlockSpec((1,) + bias.shape[1:], lambda i, h, t: (h, 0, 0, 0))],
        out_specs=tile,
        out_shape=jax.ShapeDtypeStruct((b, s, w), BF16),
        compiler_params=_cparams(("parallel", "parallel", "parallel")),
        name="na_attn",
    )(q, k, v, bias)


def _pool_kernel(s_total, u_ref, w_ref, scale_ref, o_ref):
    t = pl.program_id(1)
    t0 = t * TOK_TILE
    seg_lo = jnp.where(t == 0, 0, TOK_TILE)
    seg_hi = jnp.where(t == 0, TOK_TILE, s_total)
    span = TOK_TILE + 2 * POOL_HALO
    start = pl.multiple_of(jnp.clip(t0 - POOL_HALO, 0, s_total - span), SUBLANES)
    hi, lo = _split(u_ref[0, pl.ds(start, span), :])
    own = u_ref[0, pl.ds(pl.multiple_of(t0, TOK_TILE), TOK_TILE), :]
    tok_q = t0 + lax.broadcasted_iota(jnp.int32, (TOK_TILE, span), 0)
    tok_k = start + lax.broadcasted_iota(jnp.int32, (TOK_TILE, span), 1)
    group = _lane_iota(own.shape) // HEAD_DIM
    mean = jnp.zeros(own.shape, F32)
    for g, win in enumerate(POOL_WINDOWS):
        lo_t = jnp.maximum(tok_q - win // 2, seg_lo)
        hi_t = jnp.minimum(tok_q + win // 2, seg_hi)
        band = jnp.where(tok_k >= lo_t, jnp.where(tok_k < hi_t, 1.0, 0.0), 0.0).astype(BF16)
        count = (hi_t - lo_t)[:, 0:1].astype(F32)
        total = _dot(band, hi) + _dot(band, lo)
        mean = jnp.where(group == g, total / count, mean)
    p = (mean - own).astype(BF16)
    o_ref[0] = (_dot(p, w_ref[...]) * scale_ref[...]).astype(o_ref.dtype)


def _pool_call(u, w_blockdiag, scale):
    b, s, w = u.shape
    return pl.pallas_call(
        functools.partial(_pool_kernel, s),
        grid=(b, s // TOK_TILE),
        in_specs=[pl.BlockSpec((1, s, w), lambda i, t: (i, 0, 0)),
                  pl.BlockSpec((w, w), lambda i, t: (0, 0)),
                  pl.BlockSpec((1, w), lambda i, t: (0, 0))],
        out_specs=pl.BlockSpec((1, TOK_TILE, w), lambda i, t: (i, t, 0)),
        out_shape=jax.ShapeDtypeStruct((b, s, w), BF16),
        compiler_params=_cparams(("parallel", "parallel")),
        name="pool_mix",
    )(u, w_blockdiag, scale.reshape(1, w))


def _outproj_kernel(n_parts, *refs):
    y_refs = refs[:n_parts]
    w_refs = refs[n_parts:2 * n_parts]
    x_ref, mod_ref, norm_ref, x1_ref, xm_ref = refs[2 * n_parts:]
    acc = _dot(y_refs[0][0], w_refs[0][...])
    for y_ref, w_ref in zip(y_refs[1:], w_refs[1:]):
        acc = acc + _dot(y_ref[0], w_ref[...])
    x1 = x_ref[0] + mod_ref[0, 0, 2:3, :] * acc
    x1_ref[0] = x1
    xm = _rms(x1, norm_ref[...]) * (1.0 + mod_ref[0, 0, 4:5, :]) + mod_ref[0, 0, 3:4, :]
    xm_ref[0] = xm.astype(xm_ref.dtype)


def _outproj_call(parts, weights, x, mods, norm_g):
    b, s, d = x.shape
    tok = lambda w: pl.BlockSpec((1, TOK_TILE, w), lambda i, t: (i, t, 0))
    specs = [tok(p.shape[-1]) for p in parts]
    specs += [pl.BlockSpec(w.shape, lambda i, t: (0, 0)) for w in weights]
    specs += [tok(d),
              pl.BlockSpec((1, 1, N_MOD, d), lambda i, t: (i, jnp.minimum(t, 1), 0, 0)),
              pl.BlockSpec((1, d), lambda i, t: (0, 0))]
    return pl.pallas_call(
        functools.partial(_outproj_kernel, len(parts)),
        grid=(b, s // TOK_TILE),
        in_specs=specs,
        out_specs=[tok(d), tok(d)],
        out_shape=[jax.ShapeDtypeStruct((b, s, d), F32), jax.ShapeDtypeStruct((b, s, d), BF16)],
        compiler_params=_cparams(("parallel", "parallel")),
        name="out_proj",
    )(*parts, *weights, x, mods, norm_g.reshape(1, d))


def _peer_fold_kernel(k_ref, wt_ref, kw_ref):
    k_hi, k_lo = _split(k_ref[0])
    w_hi, w_lo = _split(wt_ref[0])
    kw_ref[0] = _dot3(k_hi, k_lo, w_hi, w_lo).astype(kw_ref.dtype)


def _peer_fold_call(keys_ph, wq_t):
    n, nk, kd = keys_ph.shape
    d = wq_t.shape[-1]
    blk = pl.BlockSpec((1, nk, d), lambda i: (i, 0, 0))
    return pl.pallas_call(
        _peer_fold_kernel,
        grid=(n,),
        in_specs=[pl.BlockSpec((1, nk, kd), lambda i: (i, 0, 0)), pl.BlockSpec((1, kd, d), lambda i: (i, 0, 0))],
        out_specs=blk,
        out_shape=jax.ShapeDtypeStruct((n, nk, d), BF16),
        compiler_params=_cparams(("parallel",)),
        name="peer_fold",
    )(keys_ph, wq_t)


def _peer_scores_kernel(kw_ref, x_ref, st_ref):
    st_ref[...] = _dot(kw_ref[...], x_ref[...], _NT)


def _peer_scores_call(kw, x, tm):
    n, d = x.shape
    r = kw.shape[0]
    return pl.pallas_call(
        _peer_scores_kernel,
        grid=(n // tm,),
        in_specs=[pl.BlockSpec((r, d), lambda t: (0, 0)), pl.BlockSpec((tm, d), lambda t: (t, 0))],
        out_specs=pl.BlockSpec((r, tm), lambda t: (0, t)),
        out_shape=jax.ShapeDtypeStruct((r, n), F32),
        compiler_params=_cparams(("parallel",)),
        name="peer_scores",
    )(kw, x)


def _bitonic_merge_desc(v):
    n = len(v)
    if n == 1:
        return v
    half = n // 2
    top = [jnp.maximum(v[i], v[i + half]) for i in range(half)]
    bot = [jnp.minimum(v[i], v[i + half]) for i in range(half)]
    return _bitonic_merge_desc(top) + _bitonic_merge_desc(bot)


def _sort_desc(v):
    n = len(v)
    if n == 1:
        return v
    return _bitonic_merge_desc(_sort_desc(v[:n // 2]) + _sort_desc(v[n // 2:])[::-1])


def _merge_top(a, b):
    n = len(a)
    return _bitonic_merge_desc([jnp.maximum(a[i], b[n - 1 - i]) for i in range(n)])


def _peer_select_kernel(st_ref, th_ref, e1_ref, e2_ref):
    k = PEER_TOPK
    nk = PEER_N_KEYS
    half_rows = PEER_HEADS * PEER_N_KEYS
    tops = []
    for p in range(2):
        groups = []
        for g in range(PEER_N_KEYS // k):
            vals = [st_ref[pl.ds(p * half_rows + g * k + j, PEER_HEADS, stride=PEER_N_KEYS), :]
                    for j in range(k)]
            groups.append(_sort_desc(vals))
        while len(groups) > 1:
            groups = [_merge_top(groups[i], groups[i + 1]) for i in range(0, len(groups), 2)]
        tops.append(groups[0])
    t1, t2 = tops
    neg = jnp.full(t1[0].shape, NEG, F32)
    rows = [[t1[i] + t2[j] for j in range(k // (i + 1))] for i in range(k)]
    first = _merge_top(rows[0], rows[1] + [neg] * (k - len(rows[1])))
    rest = [c for row in rows[2:] for c in row]
    rest = _sort_desc(rest + [neg] * (2 * k - len(rest)))[:k]
    top = [jnp.maximum(first[i], rest[k - 1 - i]) for i in range(k)]
    tau = functools.reduce(jnp.minimum, top)
    m1, m2 = t1[0], t2[0]
    rz = 1.0 / functools.reduce(lambda a, c: a + c, [jnp.exp(c - rows[0][0]) for c in top])
    big = jnp.full((nk, LANES), -NEG, F32)
    for h in range(PEER_HEADS):
        hs = slice(h, h + 1)
        head = slice(h * nk, (h + 1) * nk)
        s1 = st_ref[head, :]
        s2 = st_ref[half_rows + h * nk:half_rows + (h + 1) * nk, :]
        th = big
        for j in range(k):
            t2j = t2[j][hs]
            th = jnp.where(s1 + t2j >= tau[hs], t2j, th)
        th_ref[head, :] = th
        e1_ref[head, :] = jnp.exp(s1 - m1[hs]) * rz[hs]
        e2_ref[head, :] = jnp.exp(s2 - m2[hs])


def _peer_select_call(st):
    r, n = st.shape
    out = pl.BlockSpec((r // 2, LANES), lambda t: (0, t))
    return pl.pallas_call(
        _peer_select_kernel,
        grid=(n // LANES,),
        in_specs=[pl.BlockSpec((r, LANES), lambda t: (0, t))],
        out_specs=[out, out, out],
        out_shape=[jax.ShapeDtypeStruct((r // 2, n), F32)] * 3,
        compiler_params=_cparams(("parallel",)),
        name="peer_select",
    )(st)


def _peer_dense_kernel(x_ref, th_ref, e1_ref, s2_ref, e2_ref, u_ref, vt_ref, y_ref, acc_ref, g_ref, ht_ref):
    c = pl.program_id(1)
    tm = x_ref.shape[0]
    nk = PEER_N_KEYS
    half_rows = SUBLANES // 2
    n_sub = nk // PEER_SUB_KEYS

    @pl.when(c == 0)
    def _():
        acc_ref[...] = jnp.zeros(acc_ref.shape, F32)

    ht_ref[...] = _dot(u_ref[...], x_ref[...], _NT)

    def block(i, carry):
        cols = pl.ds(pl.multiple_of((i // n_sub) * LANES, LANES), LANES)
        sub = (i % n_sub) * PEER_SUB_KEYS
        for part in range(2):
            w = [jnp.zeros((PEER_SUB_KEYS, LANES), F32)] * half_rows
            for h in range(PEER_HEADS):
                first = pl.ds(pl.multiple_of(h * nk + c * SUBLANES, SUBLANES), SUBLANES)
                second = pl.ds(pl.multiple_of(h * nk + sub, PEER_SUB_KEYS), PEER_SUB_KEYS)
                th8, e18 = th_ref[first, cols], e1_ref[first, cols]
                s2, e2 = s2_ref[second, cols], e2_ref[second, cols]
                for q in range(half_rows):
                    r = part * half_rows + q
                    w[q] = w[q] + jnp.where(s2 >= th8[r:r + 1, :], e18[r:r + 1, :] * e2, 0.0)
            for q in range(half_rows):
                rows = pl.ds(pl.multiple_of((part * half_rows + q) * nk + sub, PEER_SUB_KEYS), PEER_SUB_KEYS)
                hv = ht_ref[rows, cols]
                act = 0.5 * hv * (1.0 + lax.erf(hv * SQRT_HALF))
                g_ref[rows, cols] = (w[q] * act).astype(BF16)
        return carry

    lax.fori_loop(0, (tm // LANES) * n_sub, block, 0)
    acc_ref[...] += _dot(vt_ref[...], g_ref[...])

    @pl.when(c == pl.num_programs(1) - 1)
    def _():
        y_ref[...] = acc_ref[...].T


def _peer_dense_call(x, st, th, e1, e2, u_bf, vt_bf, tm):
    n, d = x.shape
    n_exp = u_bf.shape[0]
    ne = SUBLANES * PEER_N_KEYS
    half = pl.BlockSpec((th.shape[0], tm), lambda t, c: (0, t))
    return pl.pallas_call(
        _peer_dense_kernel,
        grid=(n // tm, n_exp // ne),
        in_specs=[
            pl.BlockSpec((tm, d), lambda t, c: (t, 0)),
            half, half,
            pl.BlockSpec((th.shape[0], tm), lambda t, c: (1, t)),
            half,
            pl.BlockSpec((ne, d), lambda t, c: (c, 0)),
            pl.BlockSpec((d, ne), lambda t, c: (0, c)),
        ],
        out_specs=pl.BlockSpec((tm, d), lambda t, c: (t, 0)),
        out_shape=jax.ShapeDtypeStruct((n, d), F32),
        scratch_shapes=[pltpu.VMEM((d, tm), F32), pltpu.VMEM((ne, tm), BF16), pltpu.VMEM((ne, tm), F32)],
        compiler_params=_cparams(("parallel", "arbitrary")),
        name="peer_dense",
    )(x, th, e1, st, e2, u_bf, vt_bf)


def _peer(xm, wq, keys, u, v):
    b, s, d = xm.shape
    n = b * s
    heads, _, nk, kd = keys.shape
    keys_ph = keys.transpose(1, 0, 2, 3).reshape(2 * heads, nk, kd)
    wq_t = wq.T.reshape(heads, 2, kd, d).transpose(1, 0, 2, 3).reshape(2 * heads, kd, d)
    kw = _peer_fold_call(keys_ph, wq_t).reshape(2 * heads * nk, d)
    tm = 512 if n % 512 == 0 else TOK_TILE
    x = xm.reshape(n, d)
    st = _peer_scores_call(kw, x, tm)
    th, e1, e2 = _peer_select_call(st)
    y = _peer_dense_call(x, st, th, e1, e2, u.astype(BF16), v.T.astype(BF16), tm)
    return y.reshape(b, s, d)


def _final_kernel(x_ref, y_ref, mod_ref, o_ref):
    o_ref[0] = x_ref[0] + mod_ref[0, 0, 5:6, :] * y_ref[0]


def _final_call(x1, y, mods):
    b, s, d = x1.shape
    nt = s // TOK_TILE - 1
    src = pl.BlockSpec((1, TOK_TILE, d), lambda i, t: (i, t + 1, 0))
    return pl.pallas_call(
        _final_kernel,
        grid=(b, nt),
        in_specs=[src, src, pl.BlockSpec((1, 1, N_MOD, d), lambda i, t: (i, 1, 0, 0))],
        out_specs=pl.BlockSpec((1, TOK_TILE, d), lambda i, t: (i, t, 0)),
        out_shape=jax.ShapeDtypeStruct((b, nt * TOK_TILE, d), F32),
        compiler_params=_cparams(("parallel", "parallel")),
        name="final_residual",
    )(x1, y, mods)


def _gain2(g):
    return jnp.concatenate([g, g]).reshape(1, LANES).astype(F32)


def kernel(x, c, ctx, c_ctx, l0_ada_w, l0_ada_b, l0_norm1, l0_norm2, l0_w_in, l0_w_out, l0_pool_w, l0_pool_scale, l0_diff_qnorm, l0_diff_knorm, l0_lambda_q1, l0_lambda_k1, l0_lambda_q2, l0_lambda_k2, l0_diff_subln, l0_peer_wq, l0_peer_keys, l0_peer_u, l0_peer_v, l1_ada_w, l1_ada_b, l1_norm1, l1_norm2, l1_w_in, l1_w_out, l1_gqa_qnorm, l1_gqa_knorm, l1_na_qnorm, l1_na_knorm, l1_na_rpb, l1_peer_wq, l1_peer_keys, l1_peer_u, l1_peer_v):
    b, t_lat, d = x.shape
    ctx_len = ctx.shape[1]
    assert ctx_len == TOK_TILE and t_lat % TOK_TILE == 0 and t_lat // GRID_W >= NA_WIN_ROWS
    s = ctx_len + t_lat
    xs = jnp.concatenate([ctx, x], axis=1)
    cos, sin = _rope_tables(s, ctx_len)
    qk_scale = HEAD_DIM ** -0.5 * LOG2E

    mods0 = _modulation(c, c_ctx, l0_ada_w, l0_ada_b)
    w0 = l0_w_in.astype(BF16)
    pw = l0_pool_scale.shape[0]
    dw = DIFF_HEADS * 2 * HEAD_DIM
    roles0 = [("plain_f32", pw, 1.0, False, 0), ("norm_rope", dw, qk_scale, True, 0),
              ("norm_rope", dw, 1.0, False, 0), ("plain", dw, 1.0, True, LANES)]
    weights0 = [w0[:, :pw], w0[:, pw:pw + dw], w0[:, pw + dw:pw + 2 * dw], w0[:, pw + 2 * dw:]]
    gains0 = [None, _gain2(l0_diff_qnorm), _gain2(l0_diff_knorm), None]
    u0, q0, k0, v0 = _inproj_call(xs, mods0, l0_norm1, cos, sin, roles0, weights0, gains0)
    lam_init = 0.8 - 0.6 * math.exp(-0.3 * 0)
    lam_vecs = jnp.stack([l0_lambda_q1, l0_lambda_k1, l0_lambda_q2, l0_lambda_k2]).astype(F32)
    lazy0 = _lazy_softmax_ok(l0_diff_qnorm, l0_diff_knorm, qk_scale)
    o_diff = _diff_attn_call(lazy0, q0, k0, v0, lam_vecs, l0_diff_subln, lam_init)
    pool_bd = jax.scipy.linalg.block_diag(*[l0_pool_w[g] for g in range(l0_pool_w.shape[0])]).astype(BF16)
    y_pool = _pool_call(u0, pool_bd, l0_pool_scale)
    wo0 = l0_w_out.astype(BF16)
    x1, xm0 = _outproj_call([y_pool, o_diff], [wo0[:pw], wo0[pw:]], xs, mods0, l0_norm2)
    y_peer0 = _peer(xm0, l0_peer_wq, l0_peer_keys, l0_peer_u, l0_peer_v)

    mods1 = _modulation(c, c_ctx, l1_ada_w, l1_ada_b)
    w1 = l1_w_in.astype(BF16)
    n_q = 8 * HEAD_DIM
    n_kv = GQA_KV_HEADS * HEAD_DIM
    o_ck, o_cv, o_nq, o_nk, o_nv = n_q, n_q + n_kv, n_q + 2 * n_kv, 2 * n_q + 2 * n_kv, 3 * n_q + 2 * n_kv
    dup = lambda w: jnp.concatenate([w[:, :HEAD_DIM], w[:, :HEAD_DIM], w[:, HEAD_DIM:], w[:, HEAD_DIM:]], axis=1)
    roles1 = [("norm_rope", n_q, qk_scale, True, 0), ("norm_rope", 2 * n_kv, 1.0, False, 0),
              ("plain", n_kv, 1.0, True, HEAD_DIM), ("norm", n_q, qk_scale, False, 0), ("norm", n_q, 1.0, False, 0),
              ("plain", n_q, 1.0, False, 0)]
    weights1 = [w1[:, :o_ck], dup(w1[:, o_ck:o_cv]), w1[:, o_cv:o_nq],
                w1[:, o_nq:o_nk], w1[:, o_nk:o_nv], w1[:, o_nv:]]
    gains1 = [_gain2(l1_gqa_qnorm), _gain2(l1_gqa_knorm), None, _gain2(l1_na_qnorm), _gain2(l1_na_knorm), None]
    x2, cq, ckd, cvd, nq, nk_, nv = _inproj_call(x1, mods1, l1_norm1, cos, sin, roles1, weights1, gains1,
                                                 residual=(y_peer0, mods0))
    o_gqa = _gqa_attn_call(_lazy_softmax_ok(l1_gqa_qnorm, l1_gqa_knorm, qk_scale), cq, ckd, cvd)
    bias = _na_bias_table(l1_na_rpb, t_lat // GRID_W)
    o_na = _na_attn_call(nq, nk_, nv, bias)
    wo1 = l1_w_out.astype(BF16)
    x3, xm1 = _outproj_call([o_gqa, o_na], [wo1[:n_q], wo1[n_q:]], x2, mods1, l1_norm2)
    y_peer1 = _peer(xm1, l1_peer_wq, l1_peer_keys, l1_peer_u, l1_peer_v)
    return _final_call(x3, y_peer1, mods1)
```

```python
import functools
import math

import jax
import jax.numpy as jnp
from jax import lax
from jax.experimental import pallas as pl
from jax.experimental.pallas import tpu as pltpu

F32 = jnp.float32
BF16 = jnp.bfloat16

LANES = 128
SUBLANES = 8
VMEM_LIMIT_BYTES = 56 * 1024 * 1024

HEAD_DIM = 64
GRID_W = 64
ROPE_THETA = 10000.0
EPS = 1e-6
N_MOD = 6
POOL_WINDOWS = (2, 4, 8, 16)
POOL_HALO = max(POOL_WINDOWS) // 2
DIFF_HEADS = 6
GQA_KV_HEADS = 2
GQA_REP = 4
ROPE_PAIR = HEAD_DIM // 4
NA_WIN_ROWS = 8
NA_WIN_COLS = 16
PEER_HEADS = 8
PEER_N_KEYS = 128
PEER_TOPK = 16
TOK_TILE = 256
PEER_TOK_TILE = 512
ADA_COLS = 768
KV_CHUNK = 4096
PEER_SUB_KEYS = 32
ATTN_CHAINS = 2
ONES_ROWS = 16
LOG2E = 1.4426950408889634
LAZY_EXP_LIMIT = 60.0
NORM_SLACK = 1.05
NEG = -1e30
SQRT_HALF = 0.7071067811865476


def _cparams(sem, vmem=VMEM_LIMIT_BYTES):
    return pltpu.CompilerParams(dimension_semantics=sem, vmem_limit_bytes=vmem)


def _split(x):
    hi = x.astype(BF16)
    lo = (x - hi.astype(F32)).astype(BF16)
    return hi, lo


_NN = (((1,), (0,)), ((), ()))
_NT = (((1,), (1,)), ((), ()))


def _dot(a, b, dims=_NN):
    return lax.dot_general(a, b, dims, preferred_element_type=F32)


def _dot3(a_hi, a_lo, b_hi, b_lo, dims=_NN):
    return _dot(a_hi, b_hi, dims) + _dot(a_hi, b_lo, dims) + _dot(a_lo, b_hi, dims)


def _lane_iota(shape):
    return lax.broadcasted_iota(jnp.int32, shape, len(shape) - 1)


def _rms(x, gain):
    ms = jnp.mean(x * x, axis=-1, keepdims=True)
    return x * lax.rsqrt(ms + EPS) * gain


def _ada_kernel(c_ref, w_ref, b_ref, o_ref):
    c = c_ref[...]
    a = c / (1.0 + jnp.exp(-c))
    a_hi, a_lo = _split(a)
    w_hi, w_lo = _split(w_ref[...])
    o_ref[...] = _dot3(a_hi, a_lo, w_hi, w_lo) + b_ref[...]


def _ada_call(cvec, ada_w, ada_b):
    rows, d = cvec.shape
    n = ada_w.shape[1]
    bn = ADA_COLS
    return pl.pallas_call(
        _ada_kernel,
        grid=(n // bn,),
        in_specs=[
            pl.BlockSpec((rows, d), lambda j: (0, 0)),
            pl.BlockSpec((d, bn), lambda j: (0, j)),
            pl.BlockSpec((1, bn), lambda j: (0, j)),
        ],
        out_specs=pl.BlockSpec((rows, bn), lambda j: (0, j)),
        out_shape=jax.ShapeDtypeStruct((rows, n), F32),
        compiler_params=_cparams(("parallel",)),
        name="ada_mod",
    )(cvec, ada_w, ada_b.reshape(1, n))


def _modulation(c, c_ctx, ada_w, ada_b):
    b, d = c.shape
    rows = -(-(b + 1) // 16) * 16
    cvec = jnp.zeros((rows, d), F32).at[0].set(c_ctx).at[1:b + 1].set(c)
    m = _ada_call(cvec, ada_w, ada_b).reshape(rows, N_MOD, d)
    return jnp.stack([jnp.broadcast_to(m[0], (b, N_MOD, d)), m[1:b + 1]], axis=1)


def _head_norm(y, gain):
    r = lax.broadcasted_iota(jnp.int32, (LANES, LANES), 0) // HEAD_DIM
    c = lax.broadcasted_iota(jnp.int32, (LANES, LANES), 1) // HEAD_DIM
    ones_bd = jnp.where(r == c, 1.0, 0.0).astype(BF16)
    hi, lo = _split(y * y)
    ss = _dot(hi, ones_bd) + _dot(lo, ones_bd)
    return y * lax.rsqrt(ss * (1.0 / HEAD_DIM) + EPS) * gain


def _rope(y, cos, sin):
    up = pltpu.roll(y, LANES - ROPE_PAIR, 1)
    down = pltpu.roll(y, ROPE_PAIR, 1)
    partner = jnp.where((_lane_iota(y.shape) & ROPE_PAIR) == 0, up, down)
    return y * cos + partner * sin


def _inproj_kernel(roles, fuse_residual, *refs):
    refs = list(refs)
    x_ref = refs.pop(0)
    if fuse_residual:
        y_ref = refs.pop(0)
        pmod_ref = refs.pop(0)
    mod_ref = refs.pop(0)
    norm_ref = refs.pop(0)
    cos_ref = refs.pop(0)
    sin_ref = refs.pop(0)
    w_refs, g_refs = [], []
    for kind, _, _, _, _ in roles:
        w_refs.append(refs.pop(0))
        g_refs.append(refs.pop(0) if kind in ("norm", "norm_rope") else None)
    if fuse_residual:
        xo_ref = refs.pop(0)
    out_refs = refs

    x = x_ref[0]
    if fuse_residual:
        x = x + pmod_ref[0, 0, 5:6, :] * y_ref[0]
        xo_ref[0] = x
    xm = _rms(x, norm_ref[...]) * (1.0 + mod_ref[0, 0, 1:2, :]) + mod_ref[0, 0, 0:1, :]
    xm = xm.astype(BF16)
    cos = cos_ref[...]
    sin = sin_ref[...]
    for (kind, width, scale, transposed, unit), w_ref, g_ref, o_ref in zip(roles, w_refs, g_refs, out_refs):
        acc = _dot(xm, w_ref[...])
        if kind in ("plain", "plain_f32") and not transposed:
            o_ref[0] = acc.astype(o_ref.dtype)
            continue
        if unit:
            ones = jnp.ones((ONES_ROWS, TOK_TILE), o_ref.dtype)
            for j in range(width // LANES):
                yt = acc[:, j * LANES:(j + 1) * LANES].T.astype(o_ref.dtype)
                for k in range(LANES // unit):
                    base = (j * (LANES // unit) + k) * (unit + ONES_ROWS)
                    o_ref[0, base:base + unit, :] = yt[k * unit:(k + 1) * unit]
                    o_ref[0, base + unit:base + unit + ONES_ROWS, :] = ones
            continue
        for j in range(width // LANES):
            y = acc[:, j * LANES:(j + 1) * LANES]
            if kind in ("norm", "norm_rope"):
                y = _head_norm(y, g_ref[...])
            if kind == "norm_rope":
                y = _rope(y, cos, sin)
            if scale != 1.0:
                y = y * scale
            if transposed:
                o_ref[0, j * LANES:(j + 1) * LANES, :] = y.T.astype(o_ref.dtype)
            else:
                o_ref[0, :, j * LANES:(j + 1) * LANES] = y.astype(o_ref.dtype)


def _inproj_call(x, mods, norm_g, cos, sin, roles, weights, gains, residual=None):
    b, s, d = x.shape
    nt = s // TOK_TILE
    tok = lambda w: pl.BlockSpec((1, TOK_TILE, w), lambda i, t: (i, t, 0))
    mod_spec = pl.BlockSpec((1, 1, N_MOD, d), lambda i, t: (i, jnp.minimum(t, 1), 0, 0))
    args, specs = [x], [tok(d)]
    if residual is not None:
        y, pmods = residual
        args += [y, pmods]
        specs += [tok(d), mod_spec]
    args += [mods, norm_g.reshape(1, d), cos, sin]
    specs += [mod_spec, pl.BlockSpec((1, d), lambda i, t: (0, 0)),
              pl.BlockSpec((TOK_TILE, LANES), lambda i, t: (t, 0)),
              pl.BlockSpec((TOK_TILE, LANES), lambda i, t: (t, 0))]
    for (kind, width, _, _, _), w, g in zip(roles, weights, gains):
        args.append(w)
        specs.append(pl.BlockSpec((d, width), lambda i, t: (0, 0)))
        if kind in ("norm", "norm_rope"):
            args.append(g)
            specs.append(pl.BlockSpec((1, LANES), lambda i, t: (0, 0)))
    out_shapes, out_specs = [], []
    if residual is not None:
        out_shapes.append(jax.ShapeDtypeStruct((b, s, d), F32))
        out_specs.append(tok(d))
    for kind, width, _, transposed, unit in roles:
        dt = F32 if kind == "plain_f32" else BF16
        if transposed:
            rows = width // unit * (unit + ONES_ROWS) if unit else width
            out_shapes.append(jax.ShapeDtypeStruct((b, rows, s), dt))
            out_specs.append(pl.BlockSpec((1, rows, TOK_TILE), lambda i, t: (i, 0, t)))
        else:
            out_shapes.append(jax.ShapeDtypeStruct((b, s, width), dt))
            out_specs.append(tok(width))
    return pl.pallas_call(
        functools.partial(_inproj_kernel, tuple(roles), residual is not None),
        grid=(b, nt),
        in_specs=specs,
        out_specs=out_specs,
        out_shape=out_shapes,
        compiler_params=_cparams(("parallel", "parallel")),
        name="in_proj",
    )(*args)


def _rope_tables(s_total, ctx_len):
    t = jnp.arange(s_total - ctx_len, dtype=jnp.int32)
    pos = jnp.stack([t // GRID_W, t % GRID_W], axis=-1).astype(F32)
    n_freq = HEAD_DIM // 4
    inv_freq = ROPE_THETA ** (-jnp.arange(n_freq, dtype=F32) / n_freq)
    ang = pos[:, :, None] * inv_freq
    cos, sin = jnp.cos(ang), jnp.sin(ang)
    cos64 = jnp.concatenate([cos[:, 0], cos[:, 0], cos[:, 1], cos[:, 1]], axis=-1)
    sin64 = jnp.concatenate([-sin[:, 0], sin[:, 0], -sin[:, 1], sin[:, 1]], axis=-1)
    cos128 = jnp.concatenate([jnp.ones((ctx_len, LANES), F32), jnp.tile(cos64, (1, 2))], axis=0)
    sin128 = jnp.concatenate([jnp.zeros((ctx_len, LANES), F32), jnp.tile(sin64, (1, 2))], axis=0)
    return cos128, sin128


def _attend_all(chains, k_ref, vt_ref, is_latent, lazy_ok):
    t_lat = k_ref.shape[1] - TOK_TILE
    chunk = math.gcd(t_lat, KV_CHUNK)

    def step(keys, carry):
        out = []
        for (qt, key_lanes, value_rows, acc_ref), m_prev in zip(chains, carry):
            s = _dot(k_ref[0, keys, key_lanes], qt)
            m_new = jnp.maximum(m_prev, jnp.max(s, axis=0, keepdims=True))
            p = jnp.exp2(s - m_new)
            acc_ref[...] = jnp.exp2(m_prev - m_new) * acc_ref[...] + _dot(vt_ref[0, value_rows, keys], p.astype(BF16))
            out.append(m_new)
        return tuple(out)

    init = []
    for qt, _, _, acc_ref in chains:
        acc_ref[...] = jnp.zeros(acc_ref.shape, F32)
        init.append(jnp.full((1, qt.shape[1]), NEG, F32))
    carry = step(slice(0, TOK_TILE), tuple(init))

    def lazy_step(keys, carry):
        out = []
        for (qt, key_lanes, value_rows, acc_ref), m_prev in zip(chains, carry):
            s = _dot(k_ref[0, keys, key_lanes], qt)
            p = jnp.exp2(s - m_prev)
            m_new = jnp.maximum(m_prev, jnp.max(s, axis=0, keepdims=True))
            acc_ref[...] = jnp.exp2(m_prev - m_new) * (acc_ref[...] + _dot(vt_ref[0, value_rows, keys], p.astype(BF16)))
            out.append(m_new)
        return tuple(out)

    def latent_chunks(step_fn):
        def body(j, carry):
            return step_fn(pl.ds(pl.multiple_of(TOK_TILE + j * chunk, TOK_TILE), chunk), carry)
        lax.fori_loop(0, jnp.where(is_latent, t_lat // chunk, 0), body, carry)

    pl.when(lazy_ok)(lambda: latent_chunks(lazy_step))
    pl.when(jnp.logical_not(lazy_ok))(lambda: latent_chunks(step))
    outs = []
    for _, _, _, acc_ref in chains:
        dv = acc_ref.shape[0] - ONES_ROWS
        outs.append(acc_ref[0:dv, :] / acc_ref[dv:dv + 1, :])
    return outs


def _lazy_softmax_ok(q_gain, k_gain, q_scale):
    bound = HEAD_DIM * jnp.max(jnp.abs(q_gain)) * jnp.max(jnp.abs(k_gain)) * q_scale
    return (2.0 * NORM_SLACK * bound <= LAZY_EXP_LIMIT).astype(jnp.int32).reshape(1)


def _half_masks(q):
    lo = _lane_iota(q.shape) < HEAD_DIM
    zero = jnp.zeros_like(q)
    return jnp.where(lo, q, zero), jnp.where(lo, zero, q)


def _row_half_masks(qt):
    top = lax.broadcasted_iota(jnp.int32, qt.shape, 0) < HEAD_DIM
    zero = jnp.zeros_like(qt)
    return jnp.where(top, qt, zero), jnp.where(top, zero, qt)


def _diff_attn_kernel(lam_init, lazy_ref, qt_ref, k_ref, vt_ref, lam_ref, subln_ref, o_ref, *acc_refs):
    chains = []
    for j, acc_ref in enumerate(acc_refs):
        lanes = slice(j * LANES, (j + 1) * LANES)
        q1, q2 = _row_half_masks(qt_ref[0, lanes, :])
        values = slice(j * (LANES + ONES_ROWS), (j + 1) * (LANES + ONES_ROWS))
        chains.append((jnp.concatenate([q1, q2], axis=1), lanes, values, acc_ref))
    outs = _attend_all(chains, k_ref, vt_ref, pl.program_id(2) > 0, lazy_ref[0] > 0)
    lv = lam_ref[...]
    lam = (jnp.exp(jnp.sum(lv[0:1] * lv[1:2], axis=-1, keepdims=True))
           - jnp.exp(jnp.sum(lv[2:3] * lv[3:4], axis=-1, keepdims=True)) + lam_init)
    for j, o in enumerate(outs):
        od = o[:, :TOK_TILE] - lam * o[:, TOK_TILE:]
        ms = jnp.mean(od * od, axis=0, keepdims=True)
        on = od * lax.rsqrt(ms + EPS) * subln_ref[...] * (1.0 - lam_init)
        o_ref[0, :, j * LANES:(j + 1) * LANES] = on.T.astype(o_ref.dtype)


def _diff_attn_call(lazy_ok, qt, k, vt, lam_vecs, subln, lam_init):
    b, s, w = k.shape
    wide = ATTN_CHAINS * LANES
    return pl.pallas_call(
        functools.partial(_diff_attn_kernel, lam_init),
        grid=(b, w // wide, s // TOK_TILE),
        in_specs=[
            pl.BlockSpec(memory_space=pltpu.SMEM),
            pl.BlockSpec((1, wide, TOK_TILE), lambda i, h, t: (i, h, t)),
            pl.BlockSpec((1, s, wide), lambda i, h, t: (i, 0, h)),
            pl.BlockSpec((1, ATTN_CHAINS * (LANES + ONES_ROWS), s), lambda i, h, t: (i, h, 0)),
            pl.BlockSpec((4, HEAD_DIM), lambda i, h, t: (0, 0)),
            pl.BlockSpec((LANES, 1), lambda i, h, t: (0, 0)),
        ],
        out_specs=pl.BlockSpec((1, TOK_TILE, wide), lambda i, h, t: (i, t, h)),
        out_shape=jax.ShapeDtypeStruct((b, s, w), BF16),
        scratch_shapes=[pltpu.VMEM((LANES + ONES_ROWS, 2 * TOK_TILE), F32)] * ATTN_CHAINS,
        compiler_params=_cparams(("parallel", "parallel", "parallel")),
        name="diff_attn",
    )(lazy_ok, qt, k, vt, lam_vecs, subln.reshape(LANES, 1))


def _gqa_attn_kernel(lazy_ref, qt_ref, k_ref, vt_ref, o_ref, *acc_refs):
    t = TOK_TILE
    chains = []
    for g, acc_ref in enumerate(acc_refs):
        qa = _row_half_masks(qt_ref[0, 2 * g * LANES:(2 * g + 1) * LANES, :])
        qb = _row_half_masks(qt_ref[0, (2 * g + 1) * LANES:(2 * g + 2) * LANES, :])
        qt = jnp.concatenate([qa[0], qa[1], qb[0], qb[1]], axis=1)
        values = slice(g * (HEAD_DIM + ONES_ROWS), (g + 1) * (HEAD_DIM + ONES_ROWS))
        chains.append((qt, slice(g * LANES, (g + 1) * LANES), values, acc_ref))
    outs = _attend_all(chains, k_ref, vt_ref, pl.program_id(1) > 0, lazy_ref[0] > 0)
    ot = jnp.concatenate([o[:, j * t:(j + 1) * t] for o in outs for j in range(GQA_REP)], axis=0)
    o_ref[0] = ot.T.astype(o_ref.dtype)


def _gqa_attn_call(lazy_ok, qt, k_dup, vt):
    b, w, s = qt.shape
    groups = w // (2 * LANES)
    return pl.pallas_call(
        _gqa_attn_kernel,
        grid=(b, s // TOK_TILE),
        in_specs=[pl.BlockSpec(memory_space=pltpu.SMEM),
                  pl.BlockSpec((1, w, TOK_TILE), lambda i, t: (i, 0, t)),
                  pl.BlockSpec((1, s, groups * LANES), lambda i, t: (i, 0, 0)),
                  pl.BlockSpec((1, groups * (HEAD_DIM + ONES_ROWS), s), lambda i, t: (i, 0, 0))],
        out_specs=pl.BlockSpec((1, TOK_TILE, w), lambda i, t: (i, t, 0)),
        out_shape=jax.ShapeDtypeStruct((b, s, w), BF16),
        scratch_shapes=[pltpu.VMEM((HEAD_DIM + ONES_ROWS, 4 * TOK_TILE), F32)] * groups,
        compiler_params=_cparams(("parallel", "parallel")),
        name="gqa_attn",
    )(lazy_ok, qt, k_dup, vt)


def _na_attn_kernel(n_rows, q_ref, k_ref, v_ref, bias_ref, o_ref):
    t = pl.program_id(2)

    @pl.when(t == 0)
    def _():
        o_ref[0] = jnp.zeros(o_ref.shape[1:], o_ref.dtype)

    @pl.when(t > 0)
    def _():
        kctx = k_ref[0, 0:TOK_TILE, :]
        vctx = v_ref[0, 0:TOK_TILE, :]
        win = NA_WIN_ROWS * GRID_W
        rows_per_tile = TOK_TILE // GRID_W
        lo = _lane_iota((GRID_W, LANES)) < HEAD_DIM
        for i in range(rows_per_tile):
            r = (t - 1) * rows_per_tile + i
            r0 = jnp.clip(r - NA_WIN_ROWS // 2, 0, n_rows - NA_WIN_ROWS)
            off = pl.multiple_of(TOK_TILE + r0 * GRID_W, GRID_W)
            q1, q2 = _half_masks(q_ref[0, i * GRID_W:(i + 1) * GRID_W, :])
            q = jnp.concatenate([q1, q2], axis=0)
            s_nb = _dot(q, k_ref[0, pl.ds(off, win), :], _NT) + bias_ref[0, r - r0]
            s_cx = _dot(q, kctx, _NT)
            m = jnp.maximum(jnp.max(s_nb, axis=-1, keepdims=True), jnp.max(s_cx, axis=-1, keepdims=True))
            p_nb = jnp.exp2(s_nb - m)
            p_cx = jnp.exp2(s_cx - m)
            den = jnp.sum(p_nb, axis=-1, keepdims=True) + jnp.sum(p_cx, axis=-1, keepdims=True)
            o = (_dot(p_nb.astype(BF16), v_ref[0, pl.ds(off, win), :]) + _dot(p_cx.astype(BF16), vctx)) / den
            o_ref[0, i * GRID_W:(i + 1) * GRID_W, :] = jnp.where(lo, o[:GRID_W], o[GRID_W:]).astype(o_ref.dtype)


def _na_bias_table(rpb):
    kr = NA_WIN_ROWS
    cols = jnp.arange(GRID_W, dtype=jnp.int32)
    c0 = jnp.clip(cols - NA_WIN_COLS // 2, 0, GRID_W - NA_WIN_COLS)
    kc = jnp.arange(GRID_W, dtype=jnp.int32)
    inside = (kc[None, :] >= c0[:, None]) & (kc[None, :] < c0[:, None] + NA_WIN_COLS)
    dc = kc[None, :] - cols[:, None] + (NA_WIN_COLS - 1)
    onehot = ((dc[:, :, None] == jnp.arange(2 * NA_WIN_COLS - 1)) & inside[:, :, None]).astype(F32)
    by_col = jnp.einsum("hrd,ckd->hrck", rpb.astype(F32), onehot, precision=lax.Precision.HIGHEST)
    by_col = jnp.where(inside[None, None], by_col * LOG2E, NEG)
    tab = jnp.stack([by_col[:, NA_WIN_ROWS - 1 - var:2 * NA_WIN_ROWS - 1 - var]
                     for var in range(NA_WIN_ROWS)], axis=1)
    h = rpb.shape[0]
    tab = tab.transpose(0, 1, 3, 2, 4).reshape(h // 2, 2, NA_WIN_ROWS, GRID_W, kr * GRID_W)
    return tab.transpose(0, 2, 1, 3, 4).reshape(h // 2, NA_WIN_ROWS, 2 * GRID_W, kr * GRID_W)


def _na_attn_call(q, k, v, bias):
    b, s, w = q.shape
    n_rows = (s - TOK_TILE) // GRID_W
    seq = pl.BlockSpec((1, s, LANES), lambda i, h, t: (i, 0, h))
    tile = pl.BlockSpec((1, TOK_TILE, LANES), lambda i, h, t: (i, t, h))
    return pl.pallas_call(
        functools.partial(_na_attn_kernel, n_rows),
        grid=(b, w // LANES, s // TOK_TILE),
        in_specs=[tile, seq, seq,
                  pl.BlockSpec((1,) + bias.shape[1:], lambda i, h, t: (h, 0, 0, 0))],
        out_specs=tile,
        out_shape=jax.ShapeDtypeStruct((b, s, w), BF16),
        compiler_params=_cparams(("parallel", "parallel", "parallel")),
        name="na_attn",
    )(q, k, v, bias)


def _pool_kernel(s_total, u_ref, w_ref, scale_ref, o_ref):
    t = pl.program_id(1)
    t0 = t * TOK_TILE
    seg_lo = jnp.where(t == 0, 0, TOK_TILE)
    seg_hi = jnp.where(t == 0, TOK_TILE, s_total)
    span = TOK_TILE + 2 * POOL_HALO
    start = pl.multiple_of(jnp.clip(t0 - POOL_HALO, 0, s_total - span), SUBLANES)
    hi, lo = _split(u_ref[0, pl.ds(start, span), :])
    own = u_ref[0, pl.ds(pl.multiple_of(t0, TOK_TILE), TOK_TILE), :]
    tok_q = t0 + lax.broadcasted_iota(jnp.int32, (TOK_TILE, span), 0)
    tok_k = start + lax.broadcasted_iota(jnp.int32, (TOK_TILE, span), 1)
    group = _lane_iota(own.shape) // HEAD_DIM
    mean = jnp.zeros(own.shape, F32)
    for g, win in enumerate(POOL_WINDOWS):
        lo_t = jnp.maximum(tok_q - win // 2, seg_lo)
        hi_t = jnp.minimum(tok_q + win // 2, seg_hi)
        band = jnp.where(tok_k >= lo_t, jnp.where(tok_k < hi_t, 1.0, 0.0), 0.0).astype(BF16)
        count = (hi_t - lo_t)[:, 0:1].astype(F32)
        total = _dot(band, hi) + _dot(band, lo)
        mean = jnp.where(group == g, total / count, mean)
    p = (mean - own).astype(BF16)
    o_ref[0] = (_dot(p, w_ref[...]) * scale_ref[...]).astype(o_ref.dtype)


def _pool_call(u, w_blockdiag, scale):
    b, s, w = u.shape
    return pl.pallas_call(
        functools.partial(_pool_kernel, s),
        grid=(b, s // TOK_TILE),
        in_specs=[pl.BlockSpec((1, s, w), lambda i, t: (i, 0, 0)),
                  pl.BlockSpec((w, w), lambda i, t: (0, 0)),
                  pl.BlockSpec((1, w), lambda i, t: (0, 0))],
        out_specs=pl.BlockSpec((1, TOK_TILE, w), lambda i, t: (i, t, 0)),
        out_shape=jax.ShapeDtypeStruct((b, s, w), BF16),
        compiler_params=_cparams(("parallel", "parallel")),
        name="pool_mix",
    )(u, w_blockdiag, scale.reshape(1, w))


def _outproj_kernel(n_parts, *refs):
    y_refs = refs[:n_parts]
    w_refs = refs[n_parts:2 * n_parts]
    x_ref, mod_ref, norm_ref, x1_ref, xm_ref = refs[2 * n_parts:]
    acc = _dot(y_refs[0][0], w_refs[0][...])
    for y_ref, w_ref in zip(y_refs[1:], w_refs[1:]):
        acc = acc + _dot(y_ref[0], w_ref[...])
    x1 = x_ref[0] + mod_ref[0, 0, 2:3, :] * acc
    x1_ref[0] = x1
    xm = _rms(x1, norm_ref[...]) * (1.0 + mod_ref[0, 0, 4:5, :]) + mod_ref[0, 0, 3:4, :]
    xm_ref[0] = xm.astype(xm_ref.dtype)


def _outproj_call(parts, weights, x, mods, norm_g):
    b, s, d = x.shape
    tok = lambda w: pl.BlockSpec((1, TOK_TILE, w), lambda i, t: (i, t, 0))
    specs = [tok(p.shape[-1]) for p in parts]
    specs += [pl.BlockSpec(w.shape, lambda i, t: (0, 0)) for w in weights]
    specs += [tok(d),
              pl.BlockSpec((1, 1, N_MOD, d), lambda i, t: (i, jnp.minimum(t, 1), 0, 0)),
              pl.BlockSpec((1, d), lambda i, t: (0, 0))]
    return pl.pallas_call(
        functools.partial(_outproj_kernel, len(parts)),
        grid=(b, s // TOK_TILE),
        in_specs=specs,
        out_specs=[tok(d), tok(d)],
        out_shape=[jax.ShapeDtypeStruct((b, s, d), F32), jax.ShapeDtypeStruct((b, s, d), BF16)],
        compiler_params=_cparams(("parallel", "parallel")),
        name="out_proj",
    )(*parts, *weights, x, mods, norm_g.reshape(1, d))


def _peer_fold_kernel(k_ref, wt_ref, kw_ref):
    k_hi, k_lo = _split(k_ref[0])
    w_hi, w_lo = _split(wt_ref[0])
    kw_ref[0] = _dot3(k_hi, k_lo, w_hi, w_lo).astype(kw_ref.dtype)


def _peer_fold_call(keys_ph, wq_t):
    n, nk, kd = keys_ph.shape
    d = wq_t.shape[-1]
    blk = pl.BlockSpec((1, nk, d), lambda i: (i, 0, 0))
    return pl.pallas_call(
        _peer_fold_kernel,
        grid=(n,),
        in_specs=[pl.BlockSpec((1, nk, kd), lambda i: (i, 0, 0)), pl.BlockSpec((1, kd, d), lambda i: (i, 0, 0))],
        out_specs=blk,
        out_shape=jax.ShapeDtypeStruct((n, nk, d), BF16),
        compiler_params=_cparams(("parallel",)),
        name="peer_fold",
    )(keys_ph, wq_t)


def _peer_scores_kernel(kw_ref, x_ref, st_ref):
    st_ref[...] = _dot(kw_ref[...], x_ref[...], _NT)


def _peer_scores_call(kw, x, tm):
    n, d = x.shape
    r = kw.shape[0]
    return pl.pallas_call(
        _peer_scores_kernel,
        grid=(n // tm,),
        in_specs=[pl.BlockSpec((r, d), lambda t: (0, 0)), pl.BlockSpec((tm, d), lambda t: (t, 0))],
        out_specs=pl.BlockSpec((r, tm), lambda t: (0, t)),
        out_shape=jax.ShapeDtypeStruct((r, n), F32),
        compiler_params=_cparams(("parallel",)),
        name="peer_scores",
    )(kw, x)


def _bitonic_merge_desc(v):
    n = len(v)
    if n == 1:
        return v
    half = n // 2
    top = [jnp.maximum(v[i], v[i + half]) for i in range(half)]
    bot = [jnp.minimum(v[i], v[i + half]) for i in range(half)]
    return _bitonic_merge_desc(top) + _bitonic_merge_desc(bot)


def _sort_desc(v):
    n = len(v)
    if n == 1:
        return v
    return _bitonic_merge_desc(_sort_desc(v[:n // 2]) + _sort_desc(v[n // 2:])[::-1])


def _merge_top(a, b):
    n = len(a)
    return _bitonic_merge_desc([jnp.maximum(a[i], b[n - 1 - i]) for i in range(n)])


def _peer_select_kernel(st_ref, th_ref, e1_ref, e2_ref):
    k = PEER_TOPK
    nk = PEER_N_KEYS
    half_rows = PEER_HEADS * PEER_N_KEYS
    tops = []
    for p in range(2):
        groups = []
        for g in range(PEER_N_KEYS // k):
            vals = [st_ref[pl.ds(p * half_rows + g * k + j, PEER_HEADS, stride=PEER_N_KEYS), :]
                    for j in range(k)]
            groups.append(_sort_desc(vals))
        while len(groups) > 1:
            groups = [_merge_top(groups[i], groups[i + 1]) for i in range(0, len(groups), 2)]
        tops.append(groups[0])
    t1, t2 = tops
    neg = jnp.full(t1[0].shape, NEG, F32)
    rows = [[t1[i] + t2[j] for j in range(k // (i + 1))] for i in range(k)]
    first = _merge_top(rows[0], rows[1] + [neg] * (k - len(rows[1])))
    rest = [c for row in rows[2:] for c in row]
    rest = _sort_desc(rest + [neg] * (2 * k - len(rest)))[:k]
    top = [jnp.maximum(first[i], rest[k - 1 - i]) for i in range(k)]
    tau = functools.reduce(jnp.minimum, top)
    m1, m2 = t1[0], t2[0]
    rz = 1.0 / functools.reduce(lambda a, c: a + c, [jnp.exp(c - rows[0][0]) for c in top])
    th_rank = []
    for i in range(k):
        th_i = jnp.full(tau.shape, -NEG, F32)
        for j in range(len(rows[i])):
            th_i = jnp.where(rows[i][j] >= tau, t2[j], th_i)
        th_rank.append(th_i)
    big = jnp.full((nk, LANES), -NEG, F32)
    for h in range(PEER_HEADS):
        hs = slice(h, h + 1)
        head = slice(h * nk, (h + 1) * nk)
        s1 = st_ref[head, :]
        s2 = st_ref[half_rows + h * nk:half_rows + (h + 1) * nk, :]
        th = big
        for i in range(k):
            th = jnp.where(s1 == t1[i][hs], th_rank[i][hs], th)
        th_ref[head, :] = th
        e1_ref[head, :] = jnp.exp(s1 - m1[hs]) * rz[hs]
        e2_ref[head, :] = jnp.exp(s2 - m2[hs])


def _peer_select_call(st):
    r, n = st.shape
    out = pl.BlockSpec((r // 2, LANES), lambda t: (0, t))
    return pl.pallas_call(
        _peer_select_kernel,
        grid=(n // LANES,),
        in_specs=[pl.BlockSpec((r, LANES), lambda t: (0, t))],
        out_specs=[out, out, out],
        out_shape=[jax.ShapeDtypeStruct((r // 2, n), F32)] * 3,
        compiler_params=_cparams(("parallel",)),
        name="peer_select",
    )(st)


def _peer_dense_kernel(x_ref, th_ref, e1_ref, s2_ref, e2_ref, u_ref, vt_ref, y_ref, acc_ref, g_ref, act_ref):
    c = pl.program_id(1)
    tm = x_ref.shape[0]
    nk = PEER_N_KEYS
    half_rows = SUBLANES // 2
    n_sub = nk // PEER_SUB_KEYS

    @pl.when(c == 0)
    def _():
        acc_ref[...] = jnp.zeros(acc_ref.shape, F32)

    hv = _dot(u_ref[...], x_ref[...], _NT)
    act_ref[...] = 0.5 * hv * (1.0 + lax.erf(hv * SQRT_HALF))

    def block(i, carry):
        cols = pl.ds(pl.multiple_of((i // n_sub) * LANES, LANES), LANES)
        sub = (i % n_sub) * PEER_SUB_KEYS
        for part in range(2):
            w = [jnp.zeros((PEER_SUB_KEYS, LANES), F32)] * half_rows
            for h in range(PEER_HEADS):
                first = pl.ds(pl.multiple_of(h * nk + c * SUBLANES, SUBLANES), SUBLANES)
                second = pl.ds(pl.multiple_of(h * nk + sub, PEER_SUB_KEYS), PEER_SUB_KEYS)
                th8, e18 = th_ref[first, cols], e1_ref[first, cols]
                s2, e2 = s2_ref[second, cols], e2_ref[second, cols]
                for q in range(half_rows):
                    r = part * half_rows + q
                    w[q] = w[q] + jnp.where(s2 >= th8[r:r + 1, :], e18[r:r + 1, :] * e2, 0.0)
            for q in range(half_rows):
                rows = pl.ds(pl.multiple_of((part * half_rows + q) * nk + sub, PEER_SUB_KEYS), PEER_SUB_KEYS)
                g_ref[rows, cols] = (w[q] * act_ref[rows, cols]).astype(BF16)
        return carry

    lax.fori_loop(0, (tm // LANES) * n_sub, block, 0)
    acc_ref[...] += _dot(vt_ref[...], g_ref[...])

    @pl.when(c == pl.num_programs(1) - 1)
    def _():
        y_ref[...] = acc_ref[...].T


def _peer_dense_call(x, st, th, e1, e2, u_bf, vt_bf, tm):
    n, d = x.shape
    n_exp = u_bf.shape[0]
    ne = SUBLANES * PEER_N_KEYS
    half = pl.BlockSpec((th.shape[0], tm), lambda t, c: (0, t))
    return pl.pallas_call(
        _peer_dense_kernel,
        grid=(n // tm, n_exp // ne),
        in_specs=[
            pl.BlockSpec((tm, d), lambda t, c: (t, 0)),
            half, half,
            pl.BlockSpec((th.shape[0], tm), lambda t, c: (1, t)),
            half,
            pl.BlockSpec((ne, d), lambda t, c: (c, 0)),
            pl.BlockSpec((d, ne), lambda t, c: (0, c)),
        ],
        out_specs=pl.BlockSpec((tm, d), lambda t, c: (t, 0)),
        out_shape=jax.ShapeDtypeStruct((n, d), F32),
        scratch_shapes=[pltpu.VMEM((d, tm), F32), pltpu.VMEM((ne, tm), BF16), pltpu.VMEM((ne, tm), F32)],
        compiler_params=_cparams(("parallel", "arbitrary")),
        name="peer_dense",
    )(x, th, e1, st, e2, u_bf, vt_bf)


def _peer(xm, wq, keys, u, v):
    b, s, d = xm.shape
    n = b * s
    heads, _, nk, kd = keys.shape
    keys_ph = keys.transpose(1, 0, 2, 3).reshape(2 * heads, nk, kd)
    wq_t = wq.T.reshape(heads, 2, kd, d).transpose(1, 0, 2, 3).reshape(2 * heads, kd, d)
    kw = _peer_fold_call(keys_ph, wq_t).reshape(2 * heads * nk, d)
    tm = PEER_TOK_TILE if n % PEER_TOK_TILE == 0 else TOK_TILE
    x = xm.reshape(n, d)
    st = _peer_scores_call(kw, x, tm)
    th, e1, e2 = _peer_select_call(st)
    y = _peer_dense_call(x, st, th, e1, e2, u.astype(BF16), v.T.astype(BF16), tm)
    return y.reshape(b, s, d)


def _final_kernel(x_ref, y_ref, mod_ref, o_ref):
    o_ref[0] = x_ref[0] + mod_ref[0, 0, 5:6, :] * y_ref[0]


def _final_call(x1, y, mods):
    b, s, d = x1.shape
    nt = s // TOK_TILE - 1
    src = pl.BlockSpec((1, TOK_TILE, d), lambda i, t: (i, t + 1, 0))
    return pl.pallas_call(
        _final_kernel,
        grid=(b, nt),
        in_specs=[src, src, pl.BlockSpec((1, 1, N_MOD, d), lambda i, t: (i, 1, 0, 0))],
        out_specs=pl.BlockSpec((1, TOK_TILE, d), lambda i, t: (i, t, 0)),
        out_shape=jax.ShapeDtypeStruct((b, nt * TOK_TILE, d), F32),
        compiler_params=_cparams(("parallel", "parallel")),
        name="final_residual",
    )(x1, y, mods)


def _gain2(g):
    return jnp.concatenate([g, g]).reshape(1, LANES).astype(F32)


def kernel(x, c, ctx, c_ctx, l0_ada_w, l0_ada_b, l0_norm1, l0_norm2, l0_w_in, l0_w_out, l0_pool_w, l0_pool_scale, l0_diff_qnorm, l0_diff_knorm, l0_lambda_q1, l0_lambda_k1, l0_lambda_q2, l0_lambda_k2, l0_diff_subln, l0_peer_wq, l0_peer_keys, l0_peer_u, l0_peer_v, l1_ada_w, l1_ada_b, l1_norm1, l1_norm2, l1_w_in, l1_w_out, l1_gqa_qnorm, l1_gqa_knorm, l1_na_qnorm, l1_na_knorm, l1_na_rpb, l1_peer_wq, l1_peer_keys, l1_peer_u, l1_peer_v):
    b, t_lat, d = x.shape
    ctx_len = ctx.shape[1]
    assert ctx_len == TOK_TILE and t_lat % TOK_TILE == 0 and t_lat // GRID_W >= NA_WIN_ROWS
    s = ctx_len + t_lat
    xs = jnp.concatenate([ctx, x], axis=1)
    cos, sin = _rope_tables(s, ctx_len)
    qk_scale = HEAD_DIM ** -0.5 * LOG2E

    mods0 = _modulation(c, c_ctx, l0_ada_w, l0_ada_b)
    w0 = l0_w_in.astype(BF16)
    pw = l0_pool_scale.shape[0]
    dw = DIFF_HEADS * 2 * HEAD_DIM
    roles0 = [("plain_f32", pw, 1.0, False, 0), ("norm_rope", dw, qk_scale, True, 0),
              ("norm_rope", dw, 1.0, False, 0), ("plain", dw, 1.0, True, LANES)]
    weights0 = [w0[:, :pw], w0[:, pw:pw + dw], w0[:, pw + dw:pw + 2 * dw], w0[:, pw + 2 * dw:]]
    gains0 = [None, _gain2(l0_diff_qnorm), _gain2(l0_diff_knorm), None]
    u0, q0, k0, v0 = _inproj_call(xs, mods0, l0_norm1, cos, sin, roles0, weights0, gains0)
    lam_init = 0.8 - 0.6 * math.exp(-0.3 * 0)
    lam_vecs = jnp.stack([l0_lambda_q1, l0_lambda_k1, l0_lambda_q2, l0_lambda_k2]).astype(F32)
    lazy0 = _lazy_softmax_ok(l0_diff_qnorm, l0_diff_knorm, qk_scale)
    o_diff = _diff_attn_call(lazy0, q0, k0, v0, lam_vecs, l0_diff_subln, lam_init)
    pool_bd = jax.scipy.linalg.block_diag(*[l0_pool_w[g] for g in range(l0_pool_w.shape[0])]).astype(BF16)
    y_pool = _pool_call(u0, pool_bd, l0_pool_scale)
    wo0 = l0_w_out.astype(BF16)
    x1, xm0 = _outproj_call([y_pool, o_diff], [wo0[:pw], wo0[pw:]], xs, mods0, l0_norm2)
    y_peer0 = _peer(xm0, l0_peer_wq, l0_peer_keys, l0_peer_u, l0_peer_v)

    mods1 = _modulation(c, c_ctx, l1_ada_w, l1_ada_b)
    w1 = l1_w_in.astype(BF16)
    n_q = 8 * HEAD_DIM
    n_kv = GQA_KV_HEADS * HEAD_DIM
    o_ck, o_cv, o_nq, o_nk, o_nv = n_q, n_q + n_kv, n_q + 2 * n_kv, 2 * n_q + 2 * n_kv, 3 * n_q + 2 * n_kv
    dup = lambda w: jnp.concatenate([w[:, :HEAD_DIM], w[:, :HEAD_DIM], w[:, HEAD_DIM:], w[:, HEAD_DIM:]], axis=1)
    roles1 = [("norm_rope", n_q, qk_scale, True, 0), ("norm_rope", 2 * n_kv, 1.0, False, 0),
              ("plain", n_kv, 1.0, True, HEAD_DIM), ("norm", n_q, qk_scale, False, 0), ("norm", n_q, 1.0, False, 0),
              ("plain", n_q, 1.0, False, 0)]
    weights1 = [w1[:, :o_ck], dup(w1[:, o_ck:o_cv]), w1[:, o_cv:o_nq],
                w1[:, o_nq:o_nk], w1[:, o_nk:o_nv], w1[:, o_nv:]]
    gains1 = [_gain2(l1_gqa_qnorm), _gain2(l1_gqa_knorm), None, _gain2(l1_na_qnorm), _gain2(l1_na_knorm), None]
    x2, cq, ckd, cvd, nq, nk_, nv = _inproj_call(x1, mods1, l1_norm1, cos, sin, roles1, weights1, gains1,
                                                 residual=(y_peer0, mods0))
    o_gqa = _gqa_attn_call(_lazy_softmax_ok(l1_gqa_qnorm, l1_gqa_knorm, qk_scale), cq, ckd, cvd)
    bias = _na_bias_table(l1_na_rpb)
    o_na = _na_attn_call(nq, nk_, nv, bias)
    wo1 = l1_w_out.astype(BF16)
    x3, xm1 = _outproj_call([o_gqa, o_na], [wo1[:n_q], wo1[n_q:]], x2, mods1, l1_norm2)
    y_peer1 = _peer(xm1, l1_peer_wq, l1_peer_keys, l1_peer_u, l1_peer_v)
    return _final_call(x3, y_peer1, mods1)
```

```python
import functools
import math

import jax
import jax.numpy as jnp
from jax import lax
from jax.experimental import pallas as pl
from jax.experimental.pallas import tpu as pltpu

F32 = jnp.float32
BF16 = jnp.bfloat16

LANES = 128
SUBLANES = 8
VMEM_LIMIT_BYTES = 56 * 1024 * 1024

HEAD_DIM = 64
GRID_W = 64
ROPE_THETA = 10000.0
EPS = 1e-6
N_MOD = 6
POOL_WINDOWS = (2, 4, 8, 16)
POOL_HALO = max(POOL_WINDOWS) // 2
DIFF_HEADS = 6
GQA_KV_HEADS = 2
GQA_REP = 4
ROPE_PAIR = HEAD_DIM // 4
NA_WIN_ROWS = 8
NA_WIN_COLS = 16
PEER_HEADS = 8
PEER_N_KEYS = 128
PEER_TOPK = 16
TOK_TILE = 256
PEER_TOK_TILE = 512
ADA_COLS = 768
KV_CHUNK = 4096
PEER_SUB_KEYS = 32
ATTN_CHAINS = 2
ONES_ROWS = 16
LOG2E = 1.4426950408889634
LAZY_EXP_LIMIT = 60.0
NORM_SLACK = 1.05
NEG = -1e30
SQRT_HALF = 0.7071067811865476


def _cparams(sem, vmem=VMEM_LIMIT_BYTES):
    return pltpu.CompilerParams(dimension_semantics=sem, vmem_limit_bytes=vmem)


def _split(x):
    hi = x.astype(BF16)
    lo = (x - hi.astype(F32)).astype(BF16)
    return hi, lo


_NN = (((1,), (0,)), ((), ()))
_NT = (((1,), (1,)), ((), ()))


def _dot(a, b, dims=_NN):
    return lax.dot_general(a, b, dims, preferred_element_type=F32)


def _dot3(a_hi, a_lo, b_hi, b_lo, dims=_NN):
    return _dot(a_hi, b_hi, dims) + _dot(a_hi, b_lo, dims) + _dot(a_lo, b_hi, dims)


def _lane_iota(shape):
    return lax.broadcasted_iota(jnp.int32, shape, len(shape) - 1)


def _rms(x, gain):
    ms = jnp.mean(x * x, axis=-1, keepdims=True)
    return x * lax.rsqrt(ms + EPS) * gain


def _ada_kernel(c_ref, w_ref, b_ref, o_ref):
    c = c_ref[...]
    a = c / (1.0 + jnp.exp(-c))
    a_hi, a_lo = _split(a)
    w_hi, w_lo = _split(w_ref[...])
    o_ref[...] = _dot3(a_hi, a_lo, w_hi, w_lo) + b_ref[...]


def _ada_call(cvec, ada_w, ada_b):
    rows, d = cvec.shape
    n = ada_w.shape[1]
    bn = ADA_COLS
    return pl.pallas_call(
        _ada_kernel,
        grid=(n // bn,),
        in_specs=[
            pl.BlockSpec((rows, d), lambda j: (0, 0)),
            pl.BlockSpec((d, bn), lambda j: (0, j)),
            pl.BlockSpec((1, bn), lambda j: (0, j)),
        ],
        out_specs=pl.BlockSpec((rows, bn), lambda j: (0, j)),
        out_shape=jax.ShapeDtypeStruct((rows, n), F32),
        compiler_params=_cparams(("parallel",)),
        name="ada_mod",
    )(cvec, ada_w, ada_b.reshape(1, n))


def _modulation(c, c_ctx, ada_w, ada_b):
    b, d = c.shape
    rows = -(-(b + 1) // 16) * 16
    cvec = jnp.zeros((rows, d), F32).at[0].set(c_ctx).at[1:b + 1].set(c)
    m = _ada_call(cvec, ada_w, ada_b).reshape(rows, N_MOD, d)
    return jnp.stack([jnp.broadcast_to(m[0], (b, N_MOD, d)), m[1:b + 1]], axis=1)


def _head_norm(y, gain):
    r = lax.broadcasted_iota(jnp.int32, (LANES, LANES), 0) // HEAD_DIM
    c = lax.broadcasted_iota(jnp.int32, (LANES, LANES), 1) // HEAD_DIM
    ones_bd = jnp.where(r == c, 1.0, 0.0).astype(BF16)
    hi, lo = _split(y * y)
    ss = _dot(hi, ones_bd) + _dot(lo, ones_bd)
    return y * lax.rsqrt(ss * (1.0 / HEAD_DIM) + EPS) * gain


def _rope(y, cos, sin):
    up = pltpu.roll(y, LANES - ROPE_PAIR, 1)
    down = pltpu.roll(y, ROPE_PAIR, 1)
    partner = jnp.where((_lane_iota(y.shape) & ROPE_PAIR) == 0, up, down)
    return y * cos + partner * sin


def _inproj_kernel(roles, fuse_residual, *refs):
    refs = list(refs)
    x_ref = refs.pop(0)
    if fuse_residual:
        y_ref = refs.pop(0)
        pmod_ref = refs.pop(0)
    mod_ref = refs.pop(0)
    norm_ref = refs.pop(0)
    cos_ref = refs.pop(0)
    sin_ref = refs.pop(0)
    w_refs, g_refs = [], []
    for kind, _, _, _, _ in roles:
        w_refs.append(refs.pop(0))
        g_refs.append(refs.pop(0) if kind in ("norm", "norm_rope") else None)
    if fuse_residual:
        xo_ref = refs.pop(0)
    out_refs = refs

    x = x_ref[0]
    if fuse_residual:
        x = x + pmod_ref[0, 0, 5:6, :] * y_ref[0]
        xo_ref[0] = x
    xm = _rms(x, norm_ref[...]) * (1.0 + mod_ref[0, 0, 1:2, :]) + mod_ref[0, 0, 0:1, :]
    xm = xm.astype(BF16)
    cos = cos_ref[...]
    sin = sin_ref[...]
    for (kind, width, scale, transposed, unit), w_ref, g_ref, o_ref in zip(roles, w_refs, g_refs, out_refs):
        acc = _dot(xm, w_ref[...])
        if kind in ("plain", "plain_f32") and not transposed:
            o_ref[0] = acc.astype(o_ref.dtype)
            continue
        if unit:
            ones = jnp.ones((ONES_ROWS, TOK_TILE), o_ref.dtype)
            for j in range(width // LANES):
                yt = acc[:, j * LANES:(j + 1) * LANES].T.astype(o_ref.dtype)
                for k in range(LANES // unit):
                    base = (j * (LANES // unit) + k) * (unit + ONES_ROWS)
                    o_ref[0, base:base + unit, :] = yt[k * unit:(k + 1) * unit]
                    o_ref[0, base + unit:base + unit + ONES_ROWS, :] = ones
            continue
        for j in range(width // LANES):
            y = acc[:, j * LANES:(j + 1) * LANES]
            if kind in ("norm", "norm_rope"):
                y = _head_norm(y, g_ref[...])
            if kind == "norm_rope":
                y = _rope(y, cos, sin)
            if scale != 1.0:
                y = y * scale
            if transposed:
                o_ref[0, j * LANES:(j + 1) * LANES, :] = y.T.astype(o_ref.dtype)
            else:
                o_ref[0, :, j * LANES:(j + 1) * LANES] = y.astype(o_ref.dtype)


def _inproj_call(x, mods, norm_g, cos, sin, roles, weights, gains, residual=None):
    b, s, d = x.shape
    nt = s // TOK_TILE
    tok = lambda w: pl.BlockSpec((1, TOK_TILE, w), lambda i, t: (i, t, 0))
    mod_spec = pl.BlockSpec((1, 1, N_MOD, d), lambda i, t: (i, jnp.minimum(t, 1), 0, 0))
    args, specs = [x], [tok(d)]
    if residual is not None:
        y, pmods = residual
        args += [y, pmods]
        specs += [tok(d), mod_spec]
    args += [mods, norm_g.reshape(1, d), cos, sin]
    specs += [mod_spec, pl.BlockSpec((1, d), lambda i, t: (0, 0)),
              pl.BlockSpec((TOK_TILE, LANES), lambda i, t: (t, 0)),
              pl.BlockSpec((TOK_TILE, LANES), lambda i, t: (t, 0))]
    for (kind, width, _, _, _), w, g in zip(roles, weights, gains):
        args.append(w)
        specs.append(pl.BlockSpec((d, width), lambda i, t: (0, 0)))
        if kind in ("norm", "norm_rope"):
            args.append(g)
            specs.append(pl.BlockSpec((1, LANES), lambda i, t: (0, 0)))
    out_shapes, out_specs = [], []
    if residual is not None:
        out_shapes.append(jax.ShapeDtypeStruct((b, s, d), F32))
        out_specs.append(tok(d))
    for kind, width, _, transposed, unit in roles:
        dt = F32 if kind == "plain_f32" else BF16
        if transposed:
            rows = width // unit * (unit + ONES_ROWS) if unit else width
            out_shapes.append(jax.ShapeDtypeStruct((b, rows, s), dt))
            out_specs.append(pl.BlockSpec((1, rows, TOK_TILE), lambda i, t: (i, 0, t)))
        else:
            out_shapes.append(jax.ShapeDtypeStruct((b, s, width), dt))
            out_specs.append(tok(width))
    return pl.pallas_call(
        functools.partial(_inproj_kernel, tuple(roles), residual is not None),
        grid=(b, nt),
        in_specs=specs,
        out_specs=out_specs,
        out_shape=out_shapes,
        compiler_params=_cparams(("parallel", "parallel")),
        name="in_proj",
    )(*args)


def _rope_tables(s_total, ctx_len):
    t = jnp.arange(s_total - ctx_len, dtype=jnp.int32)
    pos = jnp.stack([t // GRID_W, t % GRID_W], axis=-1).astype(F32)
    n_freq = HEAD_DIM // 4
    inv_freq = ROPE_THETA ** (-jnp.arange(n_freq, dtype=F32) / n_freq)
    ang = pos[:, :, None] * inv_freq
    cos, sin = jnp.cos(ang), jnp.sin(ang)
    cos64 = jnp.concatenate([cos[:, 0], cos[:, 0], cos[:, 1], cos[:, 1]], axis=-1)
    sin64 = jnp.concatenate([-sin[:, 0], sin[:, 0], -sin[:, 1], sin[:, 1]], axis=-1)
    cos128 = jnp.concatenate([jnp.ones((ctx_len, LANES), F32), jnp.tile(cos64, (1, 2))], axis=0)
    sin128 = jnp.concatenate([jnp.zeros((ctx_len, LANES), F32), jnp.tile(sin64, (1, 2))], axis=0)
    return cos128, sin128


def _attend_all(chains, k_ref, vt_ref, is_latent, lazy_ok):
    t_lat = k_ref.shape[1] - TOK_TILE
    chunk = math.gcd(t_lat, KV_CHUNK)

    def step(keys, carry):
        out = []
        for (qt, key_lanes, value_rows, acc_ref), m_prev in zip(chains, carry):
            s = _dot(k_ref[0, keys, key_lanes], qt)
            m_new = jnp.maximum(m_prev, jnp.max(s, axis=0, keepdims=True))
            p = jnp.exp2(s - m_new)
            acc_ref[...] = jnp.exp2(m_prev - m_new) * acc_ref[...] + _dot(vt_ref[0, value_rows, keys], p.astype(BF16))
            out.append(m_new)
        return tuple(out)

    init = []
    for qt, _, _, acc_ref in chains:
        acc_ref[...] = jnp.zeros(acc_ref.shape, F32)
        init.append(jnp.full((1, qt.shape[1]), NEG, F32))
    carry = step(slice(0, TOK_TILE), tuple(init))

    def lazy_step(keys, carry):
        out = []
        for (qt, key_lanes, value_rows, acc_ref), m_prev in zip(chains, carry):
            s = _dot(k_ref[0, keys, key_lanes], qt)
            p = jnp.exp2(s - m_prev)
            m_new = jnp.maximum(m_prev, jnp.max(s, axis=0, keepdims=True))
            acc_ref[...] = jnp.exp2(m_prev - m_new) * (acc_ref[...] + _dot(vt_ref[0, value_rows, keys], p.astype(BF16)))
            out.append(m_new)
        return tuple(out)

    def latent_chunks(step_fn):
        def body(j, carry):
            return step_fn(pl.ds(pl.multiple_of(TOK_TILE + j * chunk, TOK_TILE), chunk), carry)
        lax.fori_loop(0, jnp.where(is_latent, t_lat // chunk, 0), body, carry)

    pl.when(lazy_ok)(lambda: latent_chunks(lazy_step))
    pl.when(jnp.logical_not(lazy_ok))(lambda: latent_chunks(step))
    outs = []
    for _, _, _, acc_ref in chains:
        dv = acc_ref.shape[0] - ONES_ROWS
        outs.append(acc_ref[0:dv, :] / acc_ref[dv:dv + 1, :])
    return outs


def _lazy_softmax_ok(q_gain, k_gain, q_scale):
    bound = HEAD_DIM * jnp.max(jnp.abs(q_gain)) * jnp.max(jnp.abs(k_gain)) * q_scale
    return (2.0 * NORM_SLACK * bound <= LAZY_EXP_LIMIT).astype(jnp.int32).reshape(1)


def _half_masks(q):
    lo = _lane_iota(q.shape) < HEAD_DIM
    zero = jnp.zeros_like(q)
    return jnp.where(lo, q, zero), jnp.where(lo, zero, q)


def _row_half_masks(qt):
    top = lax.broadcasted_iota(jnp.int32, qt.shape, 0) < HEAD_DIM
    zero = jnp.zeros_like(qt)
    return jnp.where(top, qt, zero), jnp.where(top, zero, qt)


def _diff_attn_kernel(lam_init, lazy_ref, qt_ref, k_ref, vt_ref, lam_ref, subln_ref, o_ref, *acc_refs):
    chains = []
    for j, acc_ref in enumerate(acc_refs):
        lanes = slice(j * LANES, (j + 1) * LANES)
        q1, q2 = _row_half_masks(qt_ref[0, lanes, :])
        values = slice(j * (LANES + ONES_ROWS), (j + 1) * (LANES + ONES_ROWS))
        chains.append((jnp.concatenate([q1, q2], axis=1), lanes, values, acc_ref))
    outs = _attend_all(chains, k_ref, vt_ref, pl.program_id(2) > 0, lazy_ref[0] > 0)
    lv = lam_ref[...]
    lam = (jnp.exp(jnp.sum(lv[0:1] * lv[1:2], axis=-1, keepdims=True))
           - jnp.exp(jnp.sum(lv[2:3] * lv[3:4], axis=-1, keepdims=True)) + lam_init)
    for j, o in enumerate(outs):
        od = o[:, :TOK_TILE] - lam * o[:, TOK_TILE:]
        ms = jnp.mean(od * od, axis=0, keepdims=True)
        on = od * lax.rsqrt(ms + EPS) * subln_ref[...] * (1.0 - lam_init)
        o_ref[0, :, j * LANES:(j + 1) * LANES] = on.T.astype(o_ref.dtype)


def _diff_attn_call(lazy_ok, qt, k, vt, lam_vecs, subln, lam_init):
    b, s, w = k.shape
    wide = ATTN_CHAINS * LANES
    return pl.pallas_call(
        functools.partial(_diff_attn_kernel, lam_init),
        grid=(b, w // wide, s // TOK_TILE),
        in_specs=[
            pl.BlockSpec(memory_space=pltpu.SMEM),
            pl.BlockSpec((1, wide, TOK_TILE), lambda i, h, t: (i, h, t)),
            pl.BlockSpec((1, s, wide), lambda i, h, t: (i, 0, h)),
            pl.BlockSpec((1, ATTN_CHAINS * (LANES + ONES_ROWS), s), lambda i, h, t: (i, h, 0)),
            pl.BlockSpec((4, HEAD_DIM), lambda i, h, t: (0, 0)),
            pl.BlockSpec((LANES, 1), lambda i, h, t: (0, 0)),
        ],
        out_specs=pl.BlockSpec((1, TOK_TILE, wide), lambda i, h, t: (i, t, h)),
        out_shape=jax.ShapeDtypeStruct((b, s, w), BF16),
        scratch_shapes=[pltpu.VMEM((LANES + ONES_ROWS, 2 * TOK_TILE), F32)] * ATTN_CHAINS,
        compiler_params=_cparams(("parallel", "parallel", "parallel")),
        name="diff_attn",
    )(lazy_ok, qt, k, vt, lam_vecs, subln.reshape(LANES, 1))


def _gqa_attn_kernel(lazy_ref, qt_ref, k_ref, vt_ref, o_ref, *acc_refs):
    t = TOK_TILE
    chains = []
    for g, acc_ref in enumerate(acc_refs):
        qa = _row_half_masks(qt_ref[0, 2 * g * LANES:(2 * g + 1) * LANES, :])
        qb = _row_half_masks(qt_ref[0, (2 * g + 1) * LANES:(2 * g + 2) * LANES, :])
        qt = jnp.concatenate([qa[0], qa[1], qb[0], qb[1]], axis=1)
        values = slice(g * (HEAD_DIM + ONES_ROWS), (g + 1) * (HEAD_DIM + ONES_ROWS))
        chains.append((qt, slice(g * LANES, (g + 1) * LANES), values, acc_ref))
    outs = _attend_all(chains, k_ref, vt_ref, pl.program_id(1) > 0, lazy_ref[0] > 0)
    ot = jnp.concatenate([o[:, j * t:(j + 1) * t] for o in outs for j in range(GQA_REP)], axis=0)
    o_ref[0] = ot.T.astype(o_ref.dtype)


def _gqa_attn_call(lazy_ok, qt, k_dup, vt):
    b, w, s = qt.shape
    groups = w // (2 * LANES)
    return pl.pallas_call(
        _gqa_attn_kernel,
        grid=(b, s // TOK_TILE),
        in_specs=[pl.BlockSpec(memory_space=pltpu.SMEM),
                  pl.BlockSpec((1, w, TOK_TILE), lambda i, t: (i, 0, t)),
                  pl.BlockSpec((1, s, groups * LANES), lambda i, t: (i, 0, 0)),
                  pl.BlockSpec((1, groups * (HEAD_DIM + ONES_ROWS), s), lambda i, t: (i, 0, 0))],
        out_specs=pl.BlockSpec((1, TOK_TILE, w), lambda i, t: (i, t, 0)),
        out_shape=jax.ShapeDtypeStruct((b, s, w), BF16),
        scratch_shapes=[pltpu.VMEM((HEAD_DIM + ONES_ROWS, 4 * TOK_TILE), F32)] * groups,
        compiler_params=_cparams(("parallel", "parallel")),
        name="gqa_attn",
    )(lazy_ok, qt, k_dup, vt)


def _na_attn_kernel(n_rows, q_ref, k_ref, v_ref, bias_ref, o_ref):
    t = pl.program_id(2)

    @pl.when(t == 0)
    def _():
        o_ref[0] = jnp.zeros(o_ref.shape[1:], o_ref.dtype)

    @pl.when(t > 0)
    def _():
        kctx = k_ref[0, 0:TOK_TILE, :]
        vctx = v_ref[0, 0:TOK_TILE, :]
        win = NA_WIN_ROWS * GRID_W
        rows_per_tile = TOK_TILE // GRID_W
        lo = _lane_iota((GRID_W, LANES)) < HEAD_DIM
        for i in range(rows_per_tile):
            r = (t - 1) * rows_per_tile + i
            r0 = jnp.clip(r - NA_WIN_ROWS // 2, 0, n_rows - NA_WIN_ROWS)
            off = pl.multiple_of(TOK_TILE + r0 * GRID_W, GRID_W)
            q1, q2 = _half_masks(q_ref[0, i * GRID_W:(i + 1) * GRID_W, :])
            q = jnp.concatenate([q1, q2], axis=0)
            s_nb = _dot(q, k_ref[0, pl.ds(off, win), :], _NT) + bias_ref[0, r - r0]
            s_cx = _dot(q, kctx, _NT)
            m = jnp.maximum(jnp.max(s_nb, axis=-1, keepdims=True), jnp.max(s_cx, axis=-1, keepdims=True))
            p_nb = jnp.exp2(s_nb - m)
            p_cx = jnp.exp2(s_cx - m)
            den = jnp.sum(p_nb, axis=-1, keepdims=True) + jnp.sum(p_cx, axis=-1, keepdims=True)
            o = (_dot(p_nb.astype(BF16), v_ref[0, pl.ds(off, win), :]) + _dot(p_cx.astype(BF16), vctx)) / den
            o_ref[0, i * GRID_W:(i + 1) * GRID_W, :] = jnp.where(lo, o[:GRID_W], o[GRID_W:]).astype(o_ref.dtype)


def _na_bias_table(rpb):
    kr = NA_WIN_ROWS
    cols = jnp.arange(GRID_W, dtype=jnp.int32)
    c0 = jnp.clip(cols - NA_WIN_COLS // 2, 0, GRID_W - NA_WIN_COLS)
    kc = jnp.arange(GRID_W, dtype=jnp.int32)
    inside = (kc[None, :] >= c0[:, None]) & (kc[None, :] < c0[:, None] + NA_WIN_COLS)
    dc = kc[None, :] - cols[:, None] + (NA_WIN_COLS - 1)
    onehot = ((dc[:, :, None] == jnp.arange(2 * NA_WIN_COLS - 1)) & inside[:, :, None]).astype(F32)
    by_col = jnp.einsum("hrd,ckd->hrck", rpb.astype(F32), onehot, precision=lax.Precision.HIGHEST)
    by_col = jnp.where(inside[None, None], by_col * LOG2E, NEG)
    tab = jnp.stack([by_col[:, NA_WIN_ROWS - 1 - var:2 * NA_WIN_ROWS - 1 - var]
                     for var in range(NA_WIN_ROWS)], axis=1)
    h = rpb.shape[0]
    tab = tab.transpose(0, 1, 3, 2, 4).reshape(h // 2, 2, NA_WIN_ROWS, GRID_W, kr * GRID_W)
    return tab.transpose(0, 2, 1, 3, 4).reshape(h // 2, NA_WIN_ROWS, 2 * GRID_W, kr * GRID_W)


def _na_attn_call(q, k, v, bias):
    b, s, w = q.shape
    n_rows = (s - TOK_TILE) // GRID_W
    seq = pl.BlockSpec((1, s, LANES), lambda i, h, t: (i, 0, h))
    tile = pl.BlockSpec((1, TOK_TILE, LANES), lambda i, h, t: (i, t, h))
    return pl.pallas_call(
        functools.partial(_na_attn_kernel, n_rows),
        grid=(b, w // LANES, s // TOK_TILE),
        in_specs=[tile, seq, seq,
                  pl.BlockSpec((1,) + bias.shape[1:], lambda i, h, t: (h, 0, 0, 0))],
        out_specs=tile,
        out_shape=jax.ShapeDtypeStruct((b, s, w), BF16),
        compiler_params=_cparams(("parallel", "parallel", "parallel")),
        name="na_attn",
    )(q, k, v, bias)


def _pool_kernel(s_total, u_ref, w_ref, scale_ref, o_ref):
    t = pl.program_id(1)
    t0 = t * TOK_TILE
    seg_lo = jnp.where(t == 0, 0, TOK_TILE)
    seg_hi = jnp.where(t == 0, TOK_TILE, s_total)
    span = TOK_TILE + 2 * POOL_HALO
    start = pl.multiple_of(jnp.clip(t0 - POOL_HALO, 0, s_total - span), SUBLANES)
    hi, lo = _split(u_ref[0, pl.ds(start, span), :])
    own = u_ref[0, pl.ds(pl.multiple_of(t0, TOK_TILE), TOK_TILE), :]
    tok_q = t0 + lax.broadcasted_iota(jnp.int32, (TOK_TILE, span), 0)
    tok_k = start + lax.broadcasted_iota(jnp.int32, (TOK_TILE, span), 1)
    group = _lane_iota(own.shape) // HEAD_DIM
    mean = jnp.zeros(own.shape, F32)
    for g, win in enumerate(POOL_WINDOWS):
        lo_t = jnp.maximum(tok_q - win // 2, seg_lo)
        hi_t = jnp.minimum(tok_q + win // 2, seg_hi)
        band = jnp.where(tok_k >= lo_t, jnp.where(tok_k < hi_t, 1.0, 0.0), 0.0).astype(BF16)
        count = (hi_t - lo_t)[:, 0:1].astype(F32)
        total = _dot(band, hi) + _dot(band, lo)
        mean = jnp.where(group == g, total / count, mean)
    p = (mean - own).astype(BF16)
    o_ref[0] = (_dot(p, w_ref[...]) * scale_ref[...]).astype(o_ref.dtype)


def _pool_call(u, w_blockdiag, scale):
    b, s, w = u.shape
    return pl.pallas_call(
        functools.partial(_pool_kernel, s),
        grid=(b, s // TOK_TILE),
        in_specs=[pl.BlockSpec((1, s, w), lambda i, t: (i, 0, 0)),
                  pl.BlockSpec((w, w), lambda i, t: (0, 0)),
                  pl.BlockSpec((1, w), lambda i, t: (0, 0))],
        out_specs=pl.BlockSpec((1, TOK_TILE, w), lambda i, t: (i, t, 0)),
        out_shape=jax.ShapeDtypeStruct((b, s, w), BF16),
        compiler_params=_cparams(("parallel", "parallel")),
        name="pool_mix",
    )(u, w_blockdiag, scale.reshape(1, w))


def _outproj_kernel(n_parts, *refs):
    y_refs = refs[:n_parts]
    w_refs = refs[n_parts:2 * n_parts]
    x_ref, mod_ref, norm_ref, x1_ref, xm_ref = refs[2 * n_parts:]
    acc = _dot(y_refs[0][0], w_refs[0][...])
    for y_ref, w_ref in zip(y_refs[1:], w_refs[1:]):
        acc = acc + _dot(y_ref[0], w_ref[...])
    x1 = x_ref[0] + mod_ref[0, 0, 2:3, :] * acc
    x1_ref[0] = x1
    xm = _rms(x1, norm_ref[...]) * (1.0 + mod_ref[0, 0, 4:5, :]) + mod_ref[0, 0, 3:4, :]
    xm_ref[0] = xm.astype(xm_ref.dtype)


def _outproj_call(parts, weights, x, mods, norm_g):
    b, s, d = x.shape
    tok = lambda w: pl.BlockSpec((1, TOK_TILE, w), lambda i, t: (i, t, 0))
    specs = [tok(p.shape[-1]) for p in parts]
    specs += [pl.BlockSpec(w.shape, lambda i, t: (0, 0)) for w in weights]
    specs += [tok(d),
              pl.BlockSpec((1, 1, N_MOD, d), lambda i, t: (i, jnp.minimum(t, 1), 0, 0)),
              pl.BlockSpec((1, d), lambda i, t: (0, 0))]
    return pl.pallas_call(
        functools.partial(_outproj_kernel, len(parts)),
        grid=(b, s // TOK_TILE),
        in_specs=specs,
        out_specs=[tok(d), tok(d)],
        out_shape=[jax.ShapeDtypeStruct((b, s, d), F32), jax.ShapeDtypeStruct((b, s, d), BF16)],
        compiler_params=_cparams(("parallel", "parallel")),
        name="out_proj",
    )(*parts, *weights, x, mods, norm_g.reshape(1, d))


def _peer_fold_kernel(k_ref, wt_ref, kw_ref):
    k_hi, k_lo = _split(k_ref[0])
    w_hi, w_lo = _split(wt_ref[0])
    kw_ref[0] = _dot3(k_hi, k_lo, w_hi, w_lo).astype(kw_ref.dtype)


def _peer_fold_call(keys_ph, wq_t):
    n, nk, kd = keys_ph.shape
    d = wq_t.shape[-1]
    blk = pl.BlockSpec((1, nk, d), lambda i: (i, 0, 0))
    return pl.pallas_call(
        _peer_fold_kernel,
        grid=(n,),
        in_specs=[pl.BlockSpec((1, nk, kd), lambda i: (i, 0, 0)), pl.BlockSpec((1, kd, d), lambda i: (i, 0, 0))],
        out_specs=blk,
        out_shape=jax.ShapeDtypeStruct((n, nk, d), BF16),
        compiler_params=_cparams(("parallel",)),
        name="peer_fold",
    )(keys_ph, wq_t)


def _peer_scores_kernel(kw_ref, x_ref, st_ref):
    st_ref[...] = _dot(kw_ref[...], x_ref[...], _NT)


def _peer_scores_call(kw, x, tm):
    n, d = x.shape
    r = kw.shape[0]
    return pl.pallas_call(
        _peer_scores_kernel,
        grid=(n // tm,),
        in_specs=[pl.BlockSpec((r, d), lambda t: (0, 0)), pl.BlockSpec((tm, d), lambda t: (t, 0))],
        out_specs=pl.BlockSpec((r, tm), lambda t: (0, t)),
        out_shape=jax.ShapeDtypeStruct((r, n), F32),
        compiler_params=_cparams(("parallel",)),
        name="peer_scores",
    )(kw, x)


def _bitonic_merge_desc(v):
    n = len(v)
    if n == 1:
        return v
    half = n // 2
    top = [jnp.maximum(v[i], v[i + half]) for i in range(half)]
    bot = [jnp.minimum(v[i], v[i + half]) for i in range(half)]
    return _bitonic_merge_desc(top) + _bitonic_merge_desc(bot)


def _sort_desc(v):
    n = len(v)
    if n == 1:
        return v
    return _bitonic_merge_desc(_sort_desc(v[:n // 2]) + _sort_desc(v[n // 2:])[::-1])


def _merge_top(a, b):
    n = len(a)
    return _bitonic_merge_desc([jnp.maximum(a[i], b[n - 1 - i]) for i in range(n)])


def _peer_select_kernel(st_ref, th_ref, e1_ref, e2_ref):
    k = PEER_TOPK
    nk = PEER_N_KEYS
    half_rows = PEER_HEADS * PEER_N_KEYS
    tops = []
    for p in range(2):
        groups = []
        for g in range(PEER_N_KEYS // k):
            vals = [st_ref[pl.ds(p * half_rows + g * k + j, PEER_HEADS, stride=PEER_N_KEYS), :]
                    for j in range(k)]
            groups.append(_sort_desc(vals))
        while len(groups) > 1:
            groups = [_merge_top(groups[i], groups[i + 1]) for i in range(0, len(groups), 2)]
        tops.append(groups[0])
    t1, t2 = tops
    neg = jnp.full(t1[0].shape, NEG, F32)
    rows = [[t1[i] + t2[j] for j in range(k // (i + 1))] for i in range(k)]
    first = _merge_top(rows[0], rows[1] + [neg] * (k - len(rows[1])))
    rest = [c for row in rows[2:] for c in row]
    rest = _sort_desc(rest + [neg] * (2 * k - len(rest)))[:k]
    top = [jnp.maximum(first[i], rest[k - 1 - i]) for i in range(k)]
    tau = functools.reduce(jnp.minimum, top)
    m1, m2 = t1[0], t2[0]
    rz = 1.0 / functools.reduce(lambda a, c: a + c, [jnp.exp(c - rows[0][0]) for c in top])
    th_rank = []
    for i in range(k):
        th_i = jnp.full(tau.shape, -NEG, F32)
        for j in range(len(rows[i])):
            th_i = jnp.where(rows[i][j] >= tau, t2[j], th_i)
        th_rank.append(th_i)
    big = jnp.full((nk, LANES), -NEG, F32)
    for h in range(PEER_HEADS):
        hs = slice(h, h + 1)
        head = slice(h * nk, (h + 1) * nk)
        s1 = st_ref[head, :]
        s2 = st_ref[half_rows + h * nk:half_rows + (h + 1) * nk, :]
        th = big
        for i in range(k):
            th = jnp.where(s1 == t1[i][hs], th_rank[i][hs], th)
        th_ref[head, :] = th
        e1_ref[head, :] = jnp.exp(s1 - m1[hs]) * rz[hs]
        e2_ref[head, :] = jnp.exp(s2 - m2[hs])


def _peer_select_call(st):
    r, n = st.shape
    out = pl.BlockSpec((r // 2, LANES), lambda t: (0, t))
    return pl.pallas_call(
        _peer_select_kernel,
        grid=(n // LANES,),
        in_specs=[pl.BlockSpec((r, LANES), lambda t: (0, t))],
        out_specs=[out, out, out],
        out_shape=[jax.ShapeDtypeStruct((r // 2, n), F32)] * 3,
        compiler_params=_cparams(("parallel",)),
        name="peer_select",
    )(st)


def _peer_dense_kernel(x_ref, th_ref, e1_ref, s2_ref, e2_ref, u_ref, vt_ref, y_ref, acc_ref, w_ref):
    c = pl.program_id(1)
    tm = x_ref.shape[0]
    nk = PEER_N_KEYS
    half_rows = SUBLANES // 2
    n_sub = nk // PEER_SUB_KEYS

    @pl.when(c == 0)
    def _():
        acc_ref[...] = jnp.zeros(acc_ref.shape, F32)

    def block(i, carry):
        cols = pl.ds(pl.multiple_of((i // n_sub) * LANES, LANES), LANES)
        sub = (i % n_sub) * PEER_SUB_KEYS
        for part in range(2):
            w = [jnp.zeros((PEER_SUB_KEYS, LANES), F32)] * half_rows
            for h in range(PEER_HEADS):
                first = pl.ds(pl.multiple_of(h * nk + c * SUBLANES, SUBLANES), SUBLANES)
                second = pl.ds(pl.multiple_of(h * nk + sub, PEER_SUB_KEYS), PEER_SUB_KEYS)
                th8, e18 = th_ref[first, cols], e1_ref[first, cols]
                s2, e2 = s2_ref[second, cols], e2_ref[second, cols]
                for q in range(half_rows):
                    r = part * half_rows + q
                    w[q] = w[q] + jnp.where(s2 >= th8[r:r + 1, :], e18[r:r + 1, :] * e2, 0.0)
            for q in range(half_rows):
                rows = pl.ds(pl.multiple_of((part * half_rows + q) * nk + sub, PEER_SUB_KEYS), PEER_SUB_KEYS)
                w_ref[rows, cols] = w[q]
        return carry

    lax.fori_loop(0, (tm // LANES) * n_sub, block, 0)
    hv = _dot(u_ref[...], x_ref[...], _NT)
    g = w_ref[...] * (0.5 * hv * (1.0 + lax.erf(hv * SQRT_HALF)))
    acc_ref[...] += _dot(vt_ref[...], g.astype(BF16))

    @pl.when(c == pl.num_programs(1) - 1)
    def _():
        y_ref[...] = acc_ref[...].T


def _peer_dense_call(x, st, th, e1, e2, u_bf, vt_bf, tm):
    n, d = x.shape
    n_exp = u_bf.shape[0]
    ne = SUBLANES * PEER_N_KEYS
    half = pl.BlockSpec((th.shape[0], tm), lambda t, c: (0, t))
    return pl.pallas_call(
        _peer_dense_kernel,
        grid=(n // tm, n_exp // ne),
        in_specs=[
            pl.BlockSpec((tm, d), lambda t, c: (t, 0)),
            half, half,
            pl.BlockSpec((th.shape[0], tm), lambda t, c: (1, t)),
            half,
            pl.BlockSpec((ne, d), lambda t, c: (c, 0)),
            pl.BlockSpec((d, ne), lambda t, c: (0, c)),
        ],
        out_specs=pl.BlockSpec((tm, d), lambda t, c: (t, 0)),
        out_shape=jax.ShapeDtypeStruct((n, d), F32),
        scratch_shapes=[pltpu.VMEM((d, tm), F32), pltpu.VMEM((ne, tm), F32)],
        compiler_params=_cparams(("parallel", "arbitrary")),
        name="peer_dense",
    )(x, th, e1, st, e2, u_bf, vt_bf)


def _peer(xm, wq, keys, u, v):
    b, s, d = xm.shape
    n = b * s
    heads, _, nk, kd = keys.shape
    keys_ph = keys.transpose(1, 0, 2, 3).reshape(2 * heads, nk, kd)
    wq_t = wq.T.reshape(heads, 2, kd, d).transpose(1, 0, 2, 3).reshape(2 * heads, kd, d)
    kw = _peer_fold_call(keys_ph, wq_t).reshape(2 * heads * nk, d)
    tm = PEER_TOK_TILE if n % PEER_TOK_TILE == 0 else TOK_TILE
    x = xm.reshape(n, d)
    st = _peer_scores_call(kw, x, tm)
    th, e1, e2 = _peer_select_call(st)
    y = _peer_dense_call(x, st, th, e1, e2, u.astype(BF16), v.T.astype(BF16), tm)
    return y.reshape(b, s, d)


def _final_kernel(x_ref, y_ref, mod_ref, o_ref):
    o_ref[0] = x_ref[0] + mod_ref[0, 0, 5:6, :] * y_ref[0]


def _final_call(x1, y, mods):
    b, s, d = x1.shape
    nt = s // TOK_TILE - 1
    src = pl.BlockSpec((1, TOK_TILE, d), lambda i, t: (i, t + 1, 0))
    return pl.pallas_call(
        _final_kernel,
        grid=(b, nt),
        in_specs=[src, src, pl.BlockSpec((1, 1, N_MOD, d), lambda i, t: (i, 1, 0, 0))],
        out_specs=pl.BlockSpec((1, TOK_TILE, d), lambda i, t: (i, t, 0)),
        out_shape=jax.ShapeDtypeStruct((b, nt * TOK_TILE, d), F32),
        compiler_params=_cparams(("parallel", "parallel")),
        name="final_residual",
    )(x1, y, mods)


def _gain2(g):
    return jnp.concatenate([g, g]).reshape(1, LANES).astype(F32)


def kernel(x, c, ctx, c_ctx, l0_ada_w, l0_ada_b, l0_norm1, l0_norm2, l0_w_in, l0_w_out, l0_pool_w, l0_pool_scale, l0_diff_qnorm, l0_diff_knorm, l0_lambda_q1, l0_lambda_k1, l0_lambda_q2, l0_lambda_k2, l0_diff_subln, l0_peer_wq, l0_peer_keys, l0_peer_u, l0_peer_v, l1_ada_w, l1_ada_b, l1_norm1, l1_norm2, l1_w_in, l1_w_out, l1_gqa_qnorm, l1_gqa_knorm, l1_na_qnorm, l1_na_knorm, l1_na_rpb, l1_peer_wq, l1_peer_keys, l1_peer_u, l1_peer_v):
    b, t_lat, d = x.shape
    ctx_len = ctx.shape[1]
    assert ctx_len == TOK_TILE and t_lat % TOK_TILE == 0 and t_lat // GRID_W >= NA_WIN_ROWS
    s = ctx_len + t_lat
    xs = jnp.concatenate([ctx, x], axis=1)
    cos, sin = _rope_tables(s, ctx_len)
    qk_scale = HEAD_DIM ** -0.5 * LOG2E

    mods0 = _modulation(c, c_ctx, l0_ada_w, l0_ada_b)
    w0 = l0_w_in.astype(BF16)
    pw = l0_pool_scale.shape[0]
    dw = DIFF_HEADS * 2 * HEAD_DIM
    roles0 = [("plain_f32", pw, 1.0, False, 0), ("norm_rope", dw, qk_scale, True, 0),
              ("norm_rope", dw, 1.0, False, 0), ("plain", dw, 1.0, True, LANES)]
    weights0 = [w0[:, :pw], w0[:, pw:pw + dw], w0[:, pw + dw:pw + 2 * dw], w0[:, pw + 2 * dw:]]
    gains0 = [None, _gain2(l0_diff_qnorm), _gain2(l0_diff_knorm), None]
    u0, q0, k0, v0 = _inproj_call(xs, mods0, l0_norm1, cos, sin, roles0, weights0, gains0)
    lam_init = 0.8 - 0.6 * math.exp(-0.3 * 0)
    lam_vecs = jnp.stack([l0_lambda_q1, l0_lambda_k1, l0_lambda_q2, l0_lambda_k2]).astype(F32)
    lazy0 = _lazy_softmax_ok(l0_diff_qnorm, l0_diff_knorm, qk_scale)
    o_diff = _diff_attn_call(lazy0, q0, k0, v0, lam_vecs, l0_diff_subln, lam_init)
    pool_bd = jax.scipy.linalg.block_diag(*[l0_pool_w[g] for g in range(l0_pool_w.shape[0])]).astype(BF16)
    y_pool = _pool_call(u0, pool_bd, l0_pool_scale)
    wo0 = l0_w_out.astype(BF16)
    x1, xm0 = _outproj_call([y_pool, o_diff], [wo0[:pw], wo0[pw:]], xs, mods0, l0_norm2)
    y_peer0 = _peer(xm0, l0_peer_wq, l0_peer_keys, l0_peer_u, l0_peer_v)

    mods1 = _modulation(c, c_ctx, l1_ada_w, l1_ada_b)
    w1 = l1_w_in.astype(BF16)
    n_q = 8 * HEAD_DIM
    n_kv = GQA_KV_HEADS * HEAD_DIM
    o_ck, o_cv, o_nq, o_nk, o_nv = n_q, n_q + n_kv, n_q + 2 * n_kv, 2 * n_q + 2 * n_kv, 3 * n_q + 2 * n_kv
    dup = lambda w: jnp.concatenate([w[:, :HEAD_DIM], w[:, :HEAD_DIM], w[:, HEAD_DIM:], w[:, HEAD_DIM:]], axis=1)
    roles1 = [("norm_rope", n_q, qk_scale, True, 0), ("norm_rope", 2 * n_kv, 1.0, False, 0),
              ("plain", n_kv, 1.0, True, HEAD_DIM), ("norm", n_q, qk_scale, False, 0), ("norm", n_q, 1.0, False, 0),
              ("plain", n_q, 1.0, False, 0)]
    weights1 = [w1[:, :o_ck], dup(w1[:, o_ck:o_cv]), w1[:, o_cv:o_nq],
                w1[:, o_nq:o_nk], w1[:, o_nk:o_nv], w1[:, o_nv:]]
    gains1 = [_gain2(l1_gqa_qnorm), _gain2(l1_gqa_knorm), None, _gain2(l1_na_qnorm), _gain2(l1_na_knorm), None]
    x2, cq, ckd, cvd, nq, nk_, nv = _inproj_call(x1, mods1, l1_norm1, cos, sin, roles1, weights1, gains1,
                                                 residual=(y_peer0, mods0))
    o_gqa = _gqa_attn_call(_lazy_softmax_ok(l1_gqa_qnorm, l1_gqa_knorm, qk_scale), cq, ckd, cvd)
    bias = _na_bias_table(l1_na_rpb)
    o_na = _na_attn_call(nq, nk_, nv, bias)
    wo1 = l1_w_out.astype(BF16)
    x3, xm1 = _outproj_call([o_gqa, o_na], [wo1[:n_q], wo1[n_q:]], x2, mods1, l1_norm2)
    y_peer1 = _peer(xm1, l1_peer_wq, l1_peer_keys, l1_peer_u, l1_peer_v)
    return _final_call(x3, y_peer1, mods1)
```

```python
import functools
import math

import jax
import jax.numpy as jnp
from jax import lax
from jax.experimental import pallas as pl
from jax.experimental.pallas import tpu as pltpu

F32 = jnp.float32
BF16 = jnp.bfloat16

LANES = 128
SUBLANES = 8
VMEM_LIMIT_BYTES = 56 * 1024 * 1024

HEAD_DIM = 64
GRID_W = 64
ROPE_THETA = 10000.0
EPS = 1e-6
N_MOD = 6
POOL_WINDOWS = (2, 4, 8, 16)
POOL_HALO = max(POOL_WINDOWS) // 2
DIFF_HEADS = 6
GQA_KV_HEADS = 2
GQA_REP = 4
ROPE_PAIR = HEAD_DIM // 4
NA_WIN_ROWS = 8
NA_WIN_COLS = 16
PEER_HEADS = 8
PEER_N_KEYS = 128
PEER_TOPK = 16
TOK_TILE = 256
PEER_TOK_TILE = 512
ADA_COLS = 768
KV_CHUNK = 4096
PEER_SUB_KEYS = 64
PEER_KEY_PARTS = 2
ATTN_CHAINS = 2
ONES_ROWS = 16
LOG2E = 1.4426950408889634
LAZY_EXP_LIMIT = 60.0
NORM_SLACK = 1.05
NEG = -1e30
SQRT_HALF = 0.7071067811865476


def _cparams(sem, vmem=VMEM_LIMIT_BYTES):
    return pltpu.CompilerParams(dimension_semantics=sem, vmem_limit_bytes=vmem)


def _split(x):
    hi = x.astype(BF16)
    lo = (x - hi.astype(F32)).astype(BF16)
    return hi, lo


_NN = (((1,), (0,)), ((), ()))
_NT = (((1,), (1,)), ((), ()))


def _dot(a, b, dims=_NN):
    return lax.dot_general(a, b, dims, preferred_element_type=F32)


def _dot3(a_hi, a_lo, b_hi, b_lo, dims=_NN):
    return _dot(a_hi, b_hi, dims) + _dot(a_hi, b_lo, dims) + _dot(a_lo, b_hi, dims)


def _lane_iota(shape):
    return lax.broadcasted_iota(jnp.int32, shape, len(shape) - 1)


def _rms(x, gain):
    ms = jnp.mean(x * x, axis=-1, keepdims=True)
    return x * lax.rsqrt(ms + EPS) * gain


def _ada_kernel(c_ref, w_ref, b_ref, o_ref):
    c = c_ref[...]
    a = c / (1.0 + jnp.exp(-c))
    a_hi, a_lo = _split(a)
    w_hi, w_lo = _split(w_ref[...])
    o_ref[...] = _dot3(a_hi, a_lo, w_hi, w_lo) + b_ref[...]


def _ada_call(cvec, ada_w, ada_b):
    rows, d = cvec.shape
    n = ada_w.shape[1]
    bn = ADA_COLS
    return pl.pallas_call(
        _ada_kernel,
        grid=(n // bn,),
        in_specs=[
            pl.BlockSpec((rows, d), lambda j: (0, 0)),
            pl.BlockSpec((d, bn), lambda j: (0, j)),
            pl.BlockSpec((1, bn), lambda j: (0, j)),
        ],
        out_specs=pl.BlockSpec((rows, bn), lambda j: (0, j)),
        out_shape=jax.ShapeDtypeStruct((rows, n), F32),
        compiler_params=_cparams(("parallel",)),
        name="ada_mod",
    )(cvec, ada_w, ada_b.reshape(1, n))


def _modulation(c, c_ctx, ada_w, ada_b):
    b, d = c.shape
    rows = -(-(b + 1) // 16) * 16
    cvec = jnp.zeros((rows, d), F32).at[0].set(c_ctx).at[1:b + 1].set(c)
    m = _ada_call(cvec, ada_w, ada_b).reshape(rows, N_MOD, d)
    return jnp.stack([jnp.broadcast_to(m[0], (b, N_MOD, d)), m[1:b + 1]], axis=1)


def _head_norm(y, gain):
    r = lax.broadcasted_iota(jnp.int32, (LANES, LANES), 0) // HEAD_DIM
    c = lax.broadcasted_iota(jnp.int32, (LANES, LANES), 1) // HEAD_DIM
    ones_bd = jnp.where(r == c, 1.0, 0.0).astype(BF16)
    hi, lo = _split(y * y)
    ss = _dot(hi, ones_bd) + _dot(lo, ones_bd)
    return y * lax.rsqrt(ss * (1.0 / HEAD_DIM) + EPS) * gain


def _rope(y, cos, sin):
    up = pltpu.roll(y, LANES - ROPE_PAIR, 1)
    down = pltpu.roll(y, ROPE_PAIR, 1)
    partner = jnp.where((_lane_iota(y.shape) & ROPE_PAIR) == 0, up, down)
    return y * cos + partner * sin


def _inproj_kernel(roles, fuse_residual, *refs):
    refs = list(refs)
    x_ref = refs.pop(0)
    if fuse_residual:
        y_ref = refs.pop(0)
        pmod_ref = refs.pop(0)
    mod_ref = refs.pop(0)
    norm_ref = refs.pop(0)
    cos_ref = refs.pop(0)
    sin_ref = refs.pop(0)
    w_refs, g_refs = [], []
    for kind, _, _, _, _ in roles:
        w_refs.append(refs.pop(0))
        g_refs.append(refs.pop(0) if kind in ("norm", "norm_rope") else None)
    if fuse_residual:
        xo_ref = refs.pop(0)
    out_refs = refs

    x = x_ref[0]
    if fuse_residual:
        x = x + pmod_ref[0, 0, 5:6, :] * y_ref[0]
        xo_ref[0] = x
    xm = _rms(x, norm_ref[...]) * (1.0 + mod_ref[0, 0, 1:2, :]) + mod_ref[0, 0, 0:1, :]
    xm = xm.astype(BF16)
    cos = cos_ref[...]
    sin = sin_ref[...]
    for (kind, width, scale, transposed, unit), w_ref, g_ref, o_ref in zip(roles, w_refs, g_refs, out_refs):
        acc = _dot(xm, w_ref[...])
        if kind in ("plain", "plain_f32") and not transposed:
            o_ref[0] = acc.astype(o_ref.dtype)
            continue
        if unit:
            ones = jnp.ones((ONES_ROWS, TOK_TILE), o_ref.dtype)
            for j in range(width // LANES):
                yt = acc[:, j * LANES:(j + 1) * LANES].T.astype(o_ref.dtype)
                for k in range(LANES // unit):
                    base = (j * (LANES // unit) + k) * (unit + ONES_ROWS)
                    o_ref[0, base:base + unit, :] = yt[k * unit:(k + 1) * unit]
                    o_ref[0, base + unit:base + unit + ONES_ROWS, :] = ones
            continue
        for j in range(width // LANES):
            y = acc[:, j * LANES:(j + 1) * LANES]
            if kind in ("norm", "norm_rope"):
                y = _head_norm(y, g_ref[...])
            if kind == "norm_rope":
                y = _rope(y, cos, sin)
            if scale != 1.0:
                y = y * scale
            if transposed:
                o_ref[0, j * LANES:(j + 1) * LANES, :] = y.T.astype(o_ref.dtype)
            else:
                o_ref[0, :, j * LANES:(j + 1) * LANES] = y.astype(o_ref.dtype)


def _inproj_call(x, mods, norm_g, cos, sin, roles, weights, gains, residual=None):
    b, s, d = x.shape
    nt = s // TOK_TILE
    tok = lambda w: pl.BlockSpec((1, TOK_TILE, w), lambda i, t: (i, t, 0))
    mod_spec = pl.BlockSpec((1, 1, N_MOD, d), lambda i, t: (i, jnp.minimum(t, 1), 0, 0))
    args, specs = [x], [tok(d)]
    if residual is not None:
        y, pmods = residual
        args += [y, pmods]
        specs += [tok(d), mod_spec]
    args += [mods, norm_g.reshape(1, d), cos, sin]
    specs += [mod_spec, pl.BlockSpec((1, d), lambda i, t: (0, 0)),
              pl.BlockSpec((TOK_TILE, LANES), lambda i, t: (t, 0)),
              pl.BlockSpec((TOK_TILE, LANES), lambda i, t: (t, 0))]
    for (kind, width, _, _, _), w, g in zip(roles, weights, gains):
        args.append(w)
        specs.append(pl.BlockSpec((d, width), lambda i, t: (0, 0)))
        if kind in ("norm", "norm_rope"):
            args.append(g)
            specs.append(pl.BlockSpec((1, LANES), lambda i, t: (0, 0)))
    out_shapes, out_specs = [], []
    if residual is not None:
        out_shapes.append(jax.ShapeDtypeStruct((b, s, d), F32))
        out_specs.append(tok(d))
    for kind, width, _, transposed, unit in roles:
        dt = F32 if kind == "plain_f32" else BF16
        if transposed:
            rows = width // unit * (unit + ONES_ROWS) if unit else width
            out_shapes.append(jax.ShapeDtypeStruct((b, rows, s), dt))
            out_specs.append(pl.BlockSpec((1, rows, TOK_TILE), lambda i, t: (i, 0, t)))
        else:
            out_shapes.append(jax.ShapeDtypeStruct((b, s, width), dt))
            out_specs.append(tok(width))
    return pl.pallas_call(
        functools.partial(_inproj_kernel, tuple(roles), residual is not None),
        grid=(b, nt),
        in_specs=specs,
        out_specs=out_specs,
        out_shape=out_shapes,
        compiler_params=_cparams(("parallel", "parallel")),
        name="in_proj",
    )(*args)


def _rope_tables(s_total, ctx_len):
    t = jnp.arange(s_total - ctx_len, dtype=jnp.int32)
    pos = jnp.stack([t // GRID_W, t % GRID_W], axis=-1).astype(F32)
    n_freq = HEAD_DIM // 4
    inv_freq = ROPE_THETA ** (-jnp.arange(n_freq, dtype=F32) / n_freq)
    ang = pos[:, :, None] * inv_freq
    cos, sin = jnp.cos(ang), jnp.sin(ang)
    cos64 = jnp.concatenate([cos[:, 0], cos[:, 0], cos[:, 1], cos[:, 1]], axis=-1)
    sin64 = jnp.concatenate([-sin[:, 0], sin[:, 0], -sin[:, 1], sin[:, 1]], axis=-1)
    cos128 = jnp.concatenate([jnp.ones((ctx_len, LANES), F32), jnp.tile(cos64, (1, 2))], axis=0)
    sin128 = jnp.concatenate([jnp.zeros((ctx_len, LANES), F32), jnp.tile(sin64, (1, 2))], axis=0)
    return cos128, sin128


def _attend_all(chains, k_ref, vt_ref, is_latent, lazy_ok):
    t_lat = k_ref.shape[1] - TOK_TILE
    chunk = math.gcd(t_lat, KV_CHUNK)

    def step(keys, carry):
        out = []
        for (qt, key_lanes, value_rows, acc_ref), m_prev in zip(chains, carry):
            s = _dot(k_ref[0, keys, key_lanes], qt)
            m_new = jnp.maximum(m_prev, jnp.max(s, axis=0, keepdims=True))
            p = jnp.exp2(s - m_new)
            acc_ref[...] = jnp.exp2(m_prev - m_new) * acc_ref[...] + _dot(vt_ref[0, value_rows, keys], p.astype(BF16))
            out.append(m_new)
        return tuple(out)

    init = []
    for qt, _, _, acc_ref in chains:
        acc_ref[...] = jnp.zeros(acc_ref.shape, F32)
        init.append(jnp.full((1, qt.shape[1]), NEG, F32))
    carry = step(slice(0, TOK_TILE), tuple(init))

    def lazy_step(keys, carry):
        out = []
        for (qt, key_lanes, value_rows, acc_ref), m_prev in zip(chains, carry):
            s = _dot(k_ref[0, keys, key_lanes], qt)
            p = jnp.exp2(s - m_prev)
            m_new = jnp.maximum(m_prev, jnp.max(s, axis=0, keepdims=True))
            acc_ref[...] = jnp.exp2(m_prev - m_new) * (acc_ref[...] + _dot(vt_ref[0, value_rows, keys], p.astype(BF16)))
            out.append(m_new)
        return tuple(out)

    def latent_chunks(step_fn):
        def body(j, carry):
            return step_fn(pl.ds(pl.multiple_of(TOK_TILE + j * chunk, TOK_TILE), chunk), carry)
        lax.fori_loop(0, jnp.where(is_latent, t_lat // chunk, 0), body, carry)

    pl.when(lazy_ok)(lambda: latent_chunks(lazy_step))
    pl.when(jnp.logical_not(lazy_ok))(lambda: latent_chunks(step))
    outs = []
    for _, _, _, acc_ref in chains:
        dv = acc_ref.shape[0] - ONES_ROWS
        outs.append(acc_ref[0:dv, :] / acc_ref[dv:dv + 1, :])
    return outs


def _lazy_softmax_ok(q_gain, k_gain, q_scale):
    bound = HEAD_DIM * jnp.max(jnp.abs(q_gain)) * jnp.max(jnp.abs(k_gain)) * q_scale
    return (2.0 * NORM_SLACK * bound <= LAZY_EXP_LIMIT).astype(jnp.int32).reshape(1)


def _half_masks(q):
    lo = _lane_iota(q.shape) < HEAD_DIM
    zero = jnp.zeros_like(q)
    return jnp.where(lo, q, zero), jnp.where(lo, zero, q)


def _row_half_masks(qt):
    top = lax.broadcasted_iota(jnp.int32, qt.shape, 0) < HEAD_DIM
    zero = jnp.zeros_like(qt)
    return jnp.where(top, qt, zero), jnp.where(top, zero, qt)


def _diff_attn_kernel(lam_init, lazy_ref, qt_ref, k_ref, vt_ref, lam_ref, subln_ref, o_ref, *acc_refs):
    chains = []
    for j, acc_ref in enumerate(acc_refs):
        lanes = slice(j * LANES, (j + 1) * LANES)
        q1, q2 = _row_half_masks(qt_ref[0, lanes, :])
        values = slice(j * (LANES + ONES_ROWS), (j + 1) * (LANES + ONES_ROWS))
        chains.append((jnp.concatenate([q1, q2], axis=1), lanes, values, acc_ref))
    outs = _attend_all(chains, k_ref, vt_ref, pl.program_id(2) > 0, lazy_ref[0] > 0)
    lv = lam_ref[...]
    lam = (jnp.exp(jnp.sum(lv[0:1] * lv[1:2], axis=-1, keepdims=True))
           - jnp.exp(jnp.sum(lv[2:3] * lv[3:4], axis=-1, keepdims=True)) + lam_init)
    for j, o in enumerate(outs):
        od = o[:, :TOK_TILE] - lam * o[:, TOK_TILE:]
        ms = jnp.mean(od * od, axis=0, keepdims=True)
        on = od * lax.rsqrt(ms + EPS) * subln_ref[...] * (1.0 - lam_init)
        o_ref[0, :, j * LANES:(j + 1) * LANES] = on.T.astype(o_ref.dtype)


def _diff_attn_call(lazy_ok, qt, k, vt, lam_vecs, subln, lam_init):
    b, s, w = k.shape
    wide = ATTN_CHAINS * LANES
    return pl.pallas_call(
        functools.partial(_diff_attn_kernel, lam_init),
        grid=(b, w // wide, s // TOK_TILE),
        in_specs=[
            pl.BlockSpec(memory_space=pltpu.SMEM),
            pl.BlockSpec((1, wide, TOK_TILE), lambda i, h, t: (i, h, t)),
            pl.BlockSpec((1, s, wide), lambda i, h, t: (i, 0, h)),
            pl.BlockSpec((1, ATTN_CHAINS * (LANES + ONES_ROWS), s), lambda i, h, t: (i, h, 0)),
            pl.BlockSpec((4, HEAD_DIM), lambda i, h, t: (0, 0)),
            pl.BlockSpec((LANES, 1), lambda i, h, t: (0, 0)),
        ],
        out_specs=pl.BlockSpec((1, TOK_TILE, wide), lambda i, h, t: (i, t, h)),
        out_shape=jax.ShapeDtypeStruct((b, s, w), BF16),
        scratch_shapes=[pltpu.VMEM((LANES + ONES_ROWS, 2 * TOK_TILE), F32)] * ATTN_CHAINS,
        compiler_params=_cparams(("parallel", "parallel", "parallel")),
        name="diff_attn",
    )(lazy_ok, qt, k, vt, lam_vecs, subln.reshape(LANES, 1))


def _gqa_attn_kernel(lazy_ref, qt_ref, k_ref, vt_ref, o_ref, *acc_refs):
    t = TOK_TILE
    chains = []
    for g, acc_ref in enumerate(acc_refs):
        qa = _row_half_masks(qt_ref[0, 2 * g * LANES:(2 * g + 1) * LANES, :])
        qb = _row_half_masks(qt_ref[0, (2 * g + 1) * LANES:(2 * g + 2) * LANES, :])
        qt = jnp.concatenate([qa[0], qa[1], qb[0], qb[1]], axis=1)
        values = slice(g * (HEAD_DIM + ONES_ROWS), (g + 1) * (HEAD_DIM + ONES_ROWS))
        chains.append((qt, slice(g * LANES, (g + 1) * LANES), values, acc_ref))
    outs = _attend_all(chains, k_ref, vt_ref, pl.program_id(1) > 0, lazy_ref[0] > 0)
    ot = jnp.concatenate([o[:, j * t:(j + 1) * t] for o in outs for j in range(GQA_REP)], axis=0)
    o_ref[0] = ot.T.astype(o_ref.dtype)


def _gqa_attn_call(lazy_ok, qt, k_dup, vt):
    b, w, s = qt.shape
    groups = w // (2 * LANES)
    return pl.pallas_call(
        _gqa_attn_kernel,
        grid=(b, s // TOK_TILE),
        in_specs=[pl.BlockSpec(memory_space=pltpu.SMEM),
                  pl.BlockSpec((1, w, TOK_TILE), lambda i, t: (i, 0, t)),
                  pl.BlockSpec((1, s, groups * LANES), lambda i, t: (i, 0, 0)),
                  pl.BlockSpec((1, groups * (HEAD_DIM + ONES_ROWS), s), lambda i, t: (i, 0, 0))],
        out_specs=pl.BlockSpec((1, TOK_TILE, w), lambda i, t: (i, t, 0)),
        out_shape=jax.ShapeDtypeStruct((b, s, w), BF16),
        scratch_shapes=[pltpu.VMEM((HEAD_DIM + ONES_ROWS, 4 * TOK_TILE), F32)] * groups,
        compiler_params=_cparams(("parallel", "parallel")),
        name="gqa_attn",
    )(lazy_ok, qt, k_dup, vt)


def _na_attn_kernel(n_rows, q_ref, k_ref, v_ref, bias_ref, o_ref):
    t = pl.program_id(2)

    @pl.when(t == 0)
    def _():
        o_ref[0] = jnp.zeros(o_ref.shape[1:], o_ref.dtype)

    @pl.when(t > 0)
    def _():
        kctx = k_ref[0, 0:TOK_TILE, :]
        vctx = v_ref[0, 0:TOK_TILE, :]
        win = NA_WIN_ROWS * GRID_W
        rows_per_tile = TOK_TILE // GRID_W
        lo = _lane_iota((GRID_W, LANES)) < HEAD_DIM
        for i in range(rows_per_tile):
            r = (t - 1) * rows_per_tile + i
            r0 = jnp.clip(r - NA_WIN_ROWS // 2, 0, n_rows - NA_WIN_ROWS)
            off = pl.multiple_of(TOK_TILE + r0 * GRID_W, GRID_W)
            q1, q2 = _half_masks(q_ref[0, i * GRID_W:(i + 1) * GRID_W, :])
            q = jnp.concatenate([q1, q2], axis=0)
            s_nb = _dot(q, k_ref[0, pl.ds(off, win), :], _NT) + bias_ref[0, r - r0]
            s_cx = _dot(q, kctx, _NT)
            m = jnp.maximum(jnp.max(s_nb, axis=-1, keepdims=True), jnp.max(s_cx, axis=-1, keepdims=True))
            p_nb = jnp.exp2(s_nb - m)
            p_cx = jnp.exp2(s_cx - m)
            den = jnp.sum(p_nb, axis=-1, keepdims=True) + jnp.sum(p_cx, axis=-1, keepdims=True)
            o = (_dot(p_nb.astype(BF16), v_ref[0, pl.ds(off, win), :]) + _dot(p_cx.astype(BF16), vctx)) / den
            o_ref[0, i * GRID_W:(i + 1) * GRID_W, :] = jnp.where(lo, o[:GRID_W], o[GRID_W:]).astype(o_ref.dtype)


def _na_bias_table(rpb):
    kr = NA_WIN_ROWS
    cols = jnp.arange(GRID_W, dtype=jnp.int32)
    c0 = jnp.clip(cols - NA_WIN_COLS // 2, 0, GRID_W - NA_WIN_COLS)
    kc = jnp.arange(GRID_W, dtype=jnp.int32)
    inside = (kc[None, :] >= c0[:, None]) & (kc[None, :] < c0[:, None] + NA_WIN_COLS)
    dc = kc[None, :] - cols[:, None] + (NA_WIN_COLS - 1)
    onehot = ((dc[:, :, None] == jnp.arange(2 * NA_WIN_COLS - 1)) & inside[:, :, None]).astype(F32)
    by_col = jnp.einsum("hrd,ckd->hrck", rpb.astype(F32), onehot, precision=lax.Precision.HIGHEST)
    by_col = jnp.where(inside[None, None], by_col * LOG2E, NEG)
    tab = jnp.stack([by_col[:, NA_WIN_ROWS - 1 - var:2 * NA_WIN_ROWS - 1 - var]
                     for var in range(NA_WIN_ROWS)], axis=1)
    h = rpb.shape[0]
    tab = tab.transpose(0, 1, 3, 2, 4).reshape(h // 2, 2, NA_WIN_ROWS, GRID_W, kr * GRID_W)
    return tab.transpose(0, 2, 1, 3, 4).reshape(h // 2, NA_WIN_ROWS, 2 * GRID_W, kr * GRID_W)


def _na_attn_call(q, k, v, bias):
    b, s, w = q.shape
    n_rows = (s - TOK_TILE) // GRID_W
    seq = pl.BlockSpec((1, s, LANES), lambda i, h, t: (i, 0, h))
    tile = pl.BlockSpec((1, TOK_TILE, LANES), lambda i, h, t: (i, t, h))
    return pl.pallas_call(
        functools.partial(_na_attn_kernel, n_rows),
        grid=(b, w // LANES, s // TOK_TILE),
        in_specs=[tile, seq, seq,
                  pl.BlockSpec((1,) + bias.shape[1:], lambda i, h, t: (h, 0, 0, 0))],
        out_specs=tile,
        out_shape=jax.ShapeDtypeStruct((b, s, w), BF16),
        compiler_params=_cparams(("parallel", "parallel", "parallel")),
        name="na_attn",
    )(q, k, v, bias)


def _pool_kernel(s_total, u_ref, w_ref, scale_ref, o_ref):
    t = pl.program_id(1)
    t0 = t * TOK_TILE
    seg_lo = jnp.where(t == 0, 0, TOK_TILE)
    seg_hi = jnp.where(t == 0, TOK_TILE, s_total)
    span = TOK_TILE + 2 * POOL_HALO
    start = pl.multiple_of(jnp.clip(t0 - POOL_HALO, 0, s_total - span), SUBLANES)
    hi, lo = _split(u_ref[0, pl.ds(start, span), :])
    own = u_ref[0, pl.ds(pl.multiple_of(t0, TOK_TILE), TOK_TILE), :]
    tok_q = t0 + lax.broadcasted_iota(jnp.int32, (TOK_TILE, span), 0)
    tok_k = start + lax.broadcasted_iota(jnp.int32, (TOK_TILE, span), 1)
    group = _lane_iota(own.shape) // HEAD_DIM
    mean = jnp.zeros(own.shape, F32)
    for g, win in enumerate(POOL_WINDOWS):
        lo_t = jnp.maximum(tok_q - win // 2, seg_lo)
        hi_t = jnp.minimum(tok_q + win // 2, seg_hi)
        band = jnp.where(tok_k >= lo_t, jnp.where(tok_k < hi_t, 1.0, 0.0), 0.0).astype(BF16)
        count = (hi_t - lo_t)[:, 0:1].astype(F32)
        total = _dot(band, hi) + _dot(band, lo)
        mean = jnp.where(group == g, total / count, mean)
    p = (mean - own).astype(BF16)
    o_ref[0] = (_dot(p, w_ref[...]) * scale_ref[...]).astype(o_ref.dtype)


def _pool_call(u, w_blockdiag, scale):
    b, s, w = u.shape
    return pl.pallas_call(
        functools.partial(_pool_kernel, s),
        grid=(b, s // TOK_TILE),
        in_specs=[pl.BlockSpec((1, s, w), lambda i, t: (i, 0, 0)),
                  pl.BlockSpec((w, w), lambda i, t: (0, 0)),
                  pl.BlockSpec((1, w), lambda i, t: (0, 0))],
        out_specs=pl.BlockSpec((1, TOK_TILE, w), lambda i, t: (i, t, 0)),
        out_shape=jax.ShapeDtypeStruct((b, s, w), BF16),
        compiler_params=_cparams(("parallel", "parallel")),
        name="pool_mix",
    )(u, w_blockdiag, scale.reshape(1, w))


def _outproj_kernel(n_parts, *refs):
    y_refs = refs[:n_parts]
    w_refs = refs[n_parts:2 * n_parts]
    x_ref, mod_ref, norm_ref, x1_ref, xm_ref = refs[2 * n_parts:]
    acc = _dot(y_refs[0][0], w_refs[0][...])
    for y_ref, w_ref in zip(y_refs[1:], w_refs[1:]):
        acc = acc + _dot(y_ref[0], w_ref[...])
    x1 = x_ref[0] + mod_ref[0, 0, 2:3, :] * acc
    x1_ref[0] = x1
    xm = _rms(x1, norm_ref[...]) * (1.0 + mod_ref[0, 0, 4:5, :]) + mod_ref[0, 0, 3:4, :]
    xm_ref[0] = xm.astype(xm_ref.dtype)


def _outproj_call(parts, weights, x, mods, norm_g):
    b, s, d = x.shape
    tok = lambda w: pl.BlockSpec((1, TOK_TILE, w), lambda i, t: (i, t, 0))
    specs = [tok(p.shape[-1]) for p in parts]
    specs += [pl.BlockSpec(w.shape, lambda i, t: (0, 0)) for w in weights]
    specs += [tok(d),
              pl.BlockSpec((1, 1, N_MOD, d), lambda i, t: (i, jnp.minimum(t, 1), 0, 0)),
              pl.BlockSpec((1, d), lambda i, t: (0, 0))]
    return pl.pallas_call(
        functools.partial(_outproj_kernel, len(parts)),
        grid=(b, s // TOK_TILE),
        in_specs=specs,
        out_specs=[tok(d), tok(d)],
        out_shape=[jax.ShapeDtypeStruct((b, s, d), F32), jax.ShapeDtypeStruct((b, s, d), BF16)],
        compiler_params=_cparams(("parallel", "parallel")),
        name="out_proj",
    )(*parts, *weights, x, mods, norm_g.reshape(1, d))


def _peer_fold_kernel(k_ref, wt_ref, kw_ref):
    k_hi, k_lo = _split(k_ref[0])
    w_hi, w_lo = _split(wt_ref[0])
    kw_ref[0] = _dot3(k_hi, k_lo, w_hi, w_lo).astype(kw_ref.dtype)


def _peer_fold_call(keys_ph, wq_t):
    n, nk, kd = keys_ph.shape
    d = wq_t.shape[-1]
    blk = pl.BlockSpec((1, nk, d), lambda i: (i, 0, 0))
    return pl.pallas_call(
        _peer_fold_kernel,
        grid=(n,),
        in_specs=[pl.BlockSpec((1, nk, kd), lambda i: (i, 0, 0)), pl.BlockSpec((1, kd, d), lambda i: (i, 0, 0))],
        out_specs=blk,
        out_shape=jax.ShapeDtypeStruct((n, nk, d), BF16),
        compiler_params=_cparams(("parallel",)),
        name="peer_fold",
    )(keys_ph, wq_t)


def _peer_scores_kernel(kw_ref, x_ref, st_ref):
    st_ref[...] = _dot(kw_ref[...], x_ref[...], _NT)


def _peer_scores_call(kw, x, tm):
    n, d = x.shape
    r = kw.shape[0]
    return pl.pallas_call(
        _peer_scores_kernel,
        grid=(n // tm,),
        in_specs=[pl.BlockSpec((r, d), lambda t: (0, 0)), pl.BlockSpec((tm, d), lambda t: (t, 0))],
        out_specs=pl.BlockSpec((r, tm), lambda t: (0, t)),
        out_shape=jax.ShapeDtypeStruct((r, n), F32),
        compiler_params=_cparams(("parallel",)),
        name="peer_scores",
    )(kw, x)


def _bitonic_merge_desc(v):
    n = len(v)
    if n == 1:
        return v
    half = n // 2
    top = [jnp.maximum(v[i], v[i + half]) for i in range(half)]
    bot = [jnp.minimum(v[i], v[i + half]) for i in range(half)]
    return _bitonic_merge_desc(top) + _bitonic_merge_desc(bot)


def _sort_desc(v):
    n = len(v)
    if n == 1:
        return v
    return _bitonic_merge_desc(_sort_desc(v[:n // 2]) + _sort_desc(v[n // 2:])[::-1])


def _merge_top(a, b):
    n = len(a)
    return _bitonic_merge_desc([jnp.maximum(a[i], b[n - 1 - i]) for i in range(n)])


def _peer_select_kernel(st_ref, th_ref, e1_ref, e2_ref):
    k = PEER_TOPK
    nk = PEER_N_KEYS
    half_rows = PEER_HEADS * PEER_N_KEYS
    tops = []
    for p in range(2):
        groups = []
        for g in range(PEER_N_KEYS // k):
            vals = [st_ref[pl.ds(p * half_rows + g * k + j, PEER_HEADS, stride=PEER_N_KEYS), :]
                    for j in range(k)]
            groups.append(_sort_desc(vals))
        while len(groups) > 1:
            groups = [_merge_top(groups[i], groups[i + 1]) for i in range(0, len(groups), 2)]
        tops.append(groups[0])
    t1, t2 = tops
    neg = jnp.full(t1[0].shape, NEG, F32)
    rows = [[t1[i] + t2[j] for j in range(k // (i + 1))] for i in range(k)]
    first = _merge_top(rows[0], rows[1] + [neg] * (k - len(rows[1])))
    rest = [c for row in rows[2:] for c in row]
    rest = _sort_desc(rest + [neg] * (2 * k - len(rest)))[:k]
    top = [jnp.maximum(first[i], rest[k - 1 - i]) for i in range(k)]
    tau = functools.reduce(jnp.minimum, top)
    m1, m2 = t1[0], t2[0]
    rz = 1.0 / functools.reduce(lambda a, c: a + c, [jnp.exp(c - rows[0][0]) for c in top])
    th_rank = []
    for i in range(k):
        th_i = jnp.full(tau.shape, -NEG, F32)
        for j in range(len(rows[i])):
            th_i = jnp.where(rows[i][j] >= tau, t2[j], th_i)
        th_rank.append(th_i)
    big = jnp.full((nk, LANES), -NEG, F32)
    for h in range(PEER_HEADS):
        hs = slice(h, h + 1)
        head = slice(h * nk, (h + 1) * nk)
        s1 = st_ref[head, :]
        s2 = st_ref[half_rows + h * nk:half_rows + (h + 1) * nk, :]
        th = big
        for i in range(k):
            th = jnp.where(s1 == t1[i][hs], th_rank[i][hs], th)
        th_ref[head, :] = th
        e1_ref[head, :] = jnp.exp(s1 - m1[hs]) * rz[hs]
        e2_ref[head, :] = jnp.exp(s2 - m2[hs])


def _peer_select_call(st):
    r, n = st.shape
    out = pl.BlockSpec((r // 2, LANES), lambda t: (0, t))
    return pl.pallas_call(
        _peer_select_kernel,
        grid=(n // LANES,),
        in_specs=[pl.BlockSpec((r, LANES), lambda t: (0, t))],
        out_specs=[out, out, out],
        out_shape=[jax.ShapeDtypeStruct((r // 2, n), F32)] * 3,
        compiler_params=_cparams(("parallel",)),
        name="peer_select",
    )(st)


def _peer_dense_kernel(x_ref, th_ref, e1_ref, s2_ref, e2_ref, u_ref, vt_ref, y_ref, acc_ref, w_ref):
    c = pl.program_id(1)
    tm = x_ref.shape[0]
    nk = PEER_N_KEYS
    half_rows = SUBLANES // PEER_KEY_PARTS
    n_sub = nk // PEER_SUB_KEYS

    @pl.when(c == 0)
    def _():
        acc_ref[...] = jnp.zeros(acc_ref.shape, F32)

    def block(i, carry):
        cols = pl.ds(pl.multiple_of((i // n_sub) * LANES, LANES), LANES)
        sub = (i % n_sub) * PEER_SUB_KEYS
        for part in range(PEER_KEY_PARTS):
            w = [jnp.zeros((PEER_SUB_KEYS, LANES), F32)] * half_rows
            for h in range(PEER_HEADS):
                first = pl.ds(pl.multiple_of(h * nk + c * SUBLANES, SUBLANES), SUBLANES)
                second = pl.ds(pl.multiple_of(h * nk + sub, PEER_SUB_KEYS), PEER_SUB_KEYS)
                th8, e18 = th_ref[first, cols], e1_ref[first, cols]
                s2, e2 = s2_ref[second, cols], e2_ref[second, cols]
                for q in range(half_rows):
                    r = part * half_rows + q
                    w[q] = w[q] + jnp.where(s2 >= th8[r:r + 1, :], e18[r:r + 1, :] * e2, 0.0)
            for q in range(half_rows):
                rows = pl.ds(pl.multiple_of((part * half_rows + q) * nk + sub, PEER_SUB_KEYS), PEER_SUB_KEYS)
                w_ref[rows, cols] = w[q]
        return carry

    lax.fori_loop(0, (tm // LANES) * n_sub, block, 0)
    hv = _dot(u_ref[...], x_ref[...], _NT)
    g = w_ref[...] * (0.5 * hv * (1.0 + lax.erf(hv * SQRT_HALF)))
    acc_ref[...] += _dot(vt_ref[...], g.astype(BF16))

    @pl.when(c == pl.num_programs(1) - 1)
    def _():
        y_ref[...] = acc_ref[...].T


def _peer_dense_call(x, st, th, e1, e2, u_bf, vt_bf, tm):
    n, d = x.shape
    n_exp = u_bf.shape[0]
    ne = SUBLANES * PEER_N_KEYS
    half = pl.BlockSpec((th.shape[0], tm), lambda t, c: (0, t))
    return pl.pallas_call(
        _peer_dense_kernel,
        grid=(n // tm, n_exp // ne),
        in_specs=[
            pl.BlockSpec((tm, d), lambda t, c: (t, 0)),
            half, half,
            pl.BlockSpec((th.shape[0], tm), lambda t, c: (1, t)),
            half,
            pl.BlockSpec((ne, d), lambda t, c: (c, 0)),
            pl.BlockSpec((d, ne), lambda t, c: (0, c)),
        ],
        out_specs=pl.BlockSpec((tm, d), lambda t, c: (t, 0)),
        out_shape=jax.ShapeDtypeStruct((n, d), F32),
        scratch_shapes=[pltpu.VMEM((d, tm), F32), pltpu.VMEM((ne, tm), F32)],
        compiler_params=_cparams(("parallel", "arbitrary")),
        name="peer_dense",
    )(x, th, e1, st, e2, u_bf, vt_bf)


def _peer(xm, wq, keys, u, v):
    b, s, d = xm.shape
    n = b * s
    heads, _, nk, kd = keys.shape
    keys_ph = keys.transpose(1, 0, 2, 3).reshape(2 * heads, nk, kd)
    wq_t = wq.T.reshape(heads, 2, kd, d).transpose(1, 0, 2, 3).reshape(2 * heads, kd, d)
    kw = _peer_fold_call(keys_ph, wq_t).reshape(2 * heads * nk, d)
    tm = PEER_TOK_TILE if n % PEER_TOK_TILE == 0 else TOK_TILE
    x = xm.reshape(n, d)
    st = _peer_scores_call(kw, x, tm)
    th, e1, e2 = _peer_select_call(st)
    y = _peer_dense_call(x, st, th, e1, e2, u.astype(BF16), v.T.astype(BF16), tm)
    return y.reshape(b, s, d)


def _final_kernel(x_ref, y_ref, mod_ref, o_ref):
    o_ref[0] = x_ref[0] + mod_ref[0, 0, 5:6, :] * y_ref[0]


def _final_call(x1, y, mods):
    b, s, d = x1.shape
    nt = s // TOK_TILE - 1
    src = pl.BlockSpec((1, TOK_TILE, d), lambda i, t: (i, t + 1, 0))
    return pl.pallas_call(
        _final_kernel,
        grid=(b, nt),
        in_specs=[src, src, pl.BlockSpec((1, 1, N_MOD, d), lambda i, t: (i, 1, 0, 0))],
        out_specs=pl.BlockSpec((1, TOK_TILE, d), lambda i, t: (i, t, 0)),
        out_shape=jax.ShapeDtypeStruct((b, nt * TOK_TILE, d), F32),
        compiler_params=_cparams(("parallel", "parallel")),
        name="final_residual",
    )(x1, y, mods)


def _gain2(g):
    return jnp.concatenate([g, g]).reshape(1, LANES).astype(F32)


def kernel(x, c, ctx, c_ctx, l0_ada_w, l0_ada_b, l0_norm1, l0_norm2, l0_w_in, l0_w_out, l0_pool_w, l0_pool_scale, l0_diff_qnorm, l0_diff_knorm, l0_lambda_q1, l0_lambda_k1, l0_lambda_q2, l0_lambda_k2, l0_diff_subln, l0_peer_wq, l0_peer_keys, l0_peer_u, l0_peer_v, l1_ada_w, l1_ada_b, l1_norm1, l1_norm2, l1_w_in, l1_w_out, l1_gqa_qnorm, l1_gqa_knorm, l1_na_qnorm, l1_na_knorm, l1_na_rpb, l1_peer_wq, l1_peer_keys, l1_peer_u, l1_peer_v):
    b, t_lat, d = x.shape
    ctx_len = ctx.shape[1]
    assert ctx_len == TOK_TILE and t_lat % TOK_TILE == 0 and t_lat // GRID_W >= NA_WIN_ROWS
    s = ctx_len + t_lat
    xs = jnp.concatenate([ctx, x], axis=1)
    cos, sin = _rope_tables(s, ctx_len)
    qk_scale = HEAD_DIM ** -0.5 * LOG2E

    mods0 = _modulation(c, c_ctx, l0_ada_w, l0_ada_b)
    w0 = l0_w_in.astype(BF16)
    pw = l0_pool_scale.shape[0]
    dw = DIFF_HEADS * 2 * HEAD_DIM
    roles0 = [("plain_f32", pw, 1.0, False, 0), ("norm_rope", dw, qk_scale, True, 0),
              ("norm_rope", dw, 1.0, False, 0), ("plain", dw, 1.0, True, LANES)]
    weights0 = [w0[:, :pw], w0[:, pw:pw + dw], w0[:, pw + dw:pw + 2 * dw], w0[:, pw + 2 * dw:]]
    gains0 = [None, _gain2(l0_diff_qnorm), _gain2(l0_diff_knorm), None]
    u0, q0, k0, v0 = _inproj_call(xs, mods0, l0_norm1, cos, sin, roles0, weights0, gains0)
    lam_init = 0.8 - 0.6 * math.exp(-0.3 * 0)
    lam_vecs = jnp.stack([l0_lambda_q1, l0_lambda_k1, l0_lambda_q2, l0_lambda_k2]).astype(F32)
    lazy0 = _lazy_softmax_ok(l0_diff_qnorm, l0_diff_knorm, qk_scale)
    o_diff = _diff_attn_call(lazy0, q0, k0, v0, lam_vecs, l0_diff_subln, lam_init)
    pool_bd = jax.scipy.linalg.block_diag(*[l0_pool_w[g] for g in range(l0_pool_w.shape[0])]).astype(BF16)
    y_pool = _pool_call(u0, pool_bd, l0_pool_scale)
    wo0 = l0_w_out.astype(BF16)
    x1, xm0 = _outproj_call([y_pool, o_diff], [wo0[:pw], wo0[pw:]], xs, mods0, l0_norm2)
    y_peer0 = _peer(xm0, l0_peer_wq, l0_peer_keys, l0_peer_u, l0_peer_v)

    mods1 = _modulation(c, c_ctx, l1_ada_w, l1_ada_b)
    w1 = l1_w_in.astype(BF16)
    n_q = 8 * HEAD_DIM
    n_kv = GQA_KV_HEADS * HEAD_DIM
    o_ck, o_cv, o_nq, o_nk, o_nv = n_q, n_q + n_kv, n_q + 2 * n_kv, 2 * n_q + 2 * n_kv, 3 * n_q + 2 * n_kv
    dup = lambda w: jnp.concatenate([w[:, :HEAD_DIM], w[:, :HEAD_DIM], w[:, HEAD_DIM:], w[:, HEAD_DIM:]], axis=1)
    roles1 = [("norm_rope", n_q, qk_scale, True, 0), ("norm_rope", 2 * n_kv, 1.0, False, 0),
              ("plain", n_kv, 1.0, True, HEAD_DIM), ("norm", n_q, qk_scale, False, 0), ("norm", n_q, 1.0, False, 0),
              ("plain", n_q, 1.0, False, 0)]
    weights1 = [w1[:, :o_ck], dup(w1[:, o_ck:o_cv]), w1[:, o_cv:o_nq],
                w1[:, o_nq:o_nk], w1[:, o_nk:o_nv], w1[:, o_nv:]]
    gains1 = [_gain2(l1_gqa_qnorm), _gain2(l1_gqa_knorm), None, _gain2(l1_na_qnorm), _gain2(l1_na_knorm), None]
    x2, cq, ckd, cvd, nq, nk_, nv = _inproj_call(x1, mods1, l1_norm1, cos, sin, roles1, weights1, gains1,
                                                 residual=(y_peer0, mods0))
    o_gqa = _gqa_attn_call(_lazy_softmax_ok(l1_gqa_qnorm, l1_gqa_knorm, qk_scale), cq, ckd, cvd)
    bias = _na_bias_table(l1_na_rpb)
    o_na = _na_attn_call(nq, nk_, nv, bias)
    wo1 = l1_w_out.astype(BF16)
    x3, xm1 = _outproj_call([o_gqa, o_na], [wo1[:n_q], wo1[n_q:]], x2, mods1, l1_norm2)
    y_peer1 = _peer(xm1, l1_peer_wq, l1_peer_keys, l1_peer_u, l1_peer_v)
    return _final_call(x3, y_peer1, mods1)
```

```python
import functools
import math

import jax
import jax.numpy as jnp
from jax import lax
from jax.experimental import pallas as pl
from jax.experimental.pallas import tpu as pltpu

F32 = jnp.float32
BF16 = jnp.bfloat16

LANES = 128
SUBLANES = 8
VMEM_LIMIT_BYTES = 56 * 1024 * 1024

HEAD_DIM = 64
GRID_W = 64
ROPE_THETA = 10000.0
EPS = 1e-6
N_MOD = 6
POOL_WINDOWS = (2, 4, 8, 16)
POOL_HALO = max(POOL_WINDOWS) // 2
DIFF_HEADS = 6
GQA_KV_HEADS = 2
GQA_REP = 4
ROPE_PAIR = HEAD_DIM // 4
NA_WIN_ROWS = 8
NA_WIN_COLS = 16
PEER_HEADS = 8
PEER_N_KEYS = 128
PEER_TOPK = 16
TOK_TILE = 256
PEER_TOK_TILE = 512
ADA_COLS = 768
KV_CHUNK = 4096
PEER_SUB_KEYS = 64
PEER_KEY_PARTS = 2
ATTN_CHAINS = 2
ONES_ROWS = 16
LOG2E = 1.4426950408889634
LAZY_EXP_LIMIT = 60.0
NORM_SLACK = 1.05
NEG = -1e30
SQRT_HALF = 0.7071067811865476


def _cparams(sem, vmem=VMEM_LIMIT_BYTES):
    return pltpu.CompilerParams(dimension_semantics=sem, vmem_limit_bytes=vmem)


def _split(x):
    hi = x.astype(BF16)
    lo = (x - hi.astype(F32)).astype(BF16)
    return hi, lo


_NN = (((1,), (0,)), ((), ()))
_NT = (((1,), (1,)), ((), ()))


def _dot(a, b, dims=_NN):
    return lax.dot_general(a, b, dims, preferred_element_type=F32)


def _dot3(a_hi, a_lo, b_hi, b_lo, dims=_NN):
    return _dot(a_hi, b_hi, dims) + _dot(a_hi, b_lo, dims) + _dot(a_lo, b_hi, dims)


def _lane_iota(shape):
    return lax.broadcasted_iota(jnp.int32, shape, len(shape) - 1)


def _rms(x, gain):
    ms = jnp.mean(x * x, axis=-1, keepdims=True)
    return x * lax.rsqrt(ms + EPS) * gain


def _ada_kernel(c_ref, w_ref, b_ref, o_ref):
    c = c_ref[...]
    a = c / (1.0 + jnp.exp(-c))
    a_hi, a_lo = _split(a)
    w_hi, w_lo = _split(w_ref[...])
    o_ref[...] = _dot3(a_hi, a_lo, w_hi, w_lo) + b_ref[...]


def _ada_call(cvec, ada_w, ada_b):
    rows, d = cvec.shape
    n = ada_w.shape[1]
    bn = ADA_COLS
    return pl.pallas_call(
        _ada_kernel,
        grid=(n // bn,),
        in_specs=[
            pl.BlockSpec((rows, d), lambda j: (0, 0)),
            pl.BlockSpec((d, bn), lambda j: (0, j)),
            pl.BlockSpec((1, bn), lambda j: (0, j)),
        ],
        out_specs=pl.BlockSpec((rows, bn), lambda j: (0, j)),
        out_shape=jax.ShapeDtypeStruct((rows, n), F32),
        compiler_params=_cparams(("parallel",)),
        name="ada_mod",
    )(cvec, ada_w, ada_b.reshape(1, n))


def _modulation(c, c_ctx, ada_w, ada_b):
    b, d = c.shape
    rows = -(-(b + 1) // 16) * 16
    cvec = jnp.zeros((rows, d), F32).at[0].set(c_ctx).at[1:b + 1].set(c)
    m = _ada_call(cvec, ada_w, ada_b).reshape(rows, N_MOD, d)
    return jnp.stack([jnp.broadcast_to(m[0], (b, N_MOD, d)), m[1:b + 1]], axis=1)


def _head_norm(y, gain):
    r = lax.broadcasted_iota(jnp.int32, (LANES, LANES), 0) // HEAD_DIM
    c = lax.broadcasted_iota(jnp.int32, (LANES, LANES), 1) // HEAD_DIM
    ones_bd = jnp.where(r == c, 1.0, 0.0).astype(BF16)
    hi, lo = _split(y * y)
    ss = _dot(hi, ones_bd) + _dot(lo, ones_bd)
    return y * lax.rsqrt(ss * (1.0 / HEAD_DIM) + EPS) * gain


def _rope(y, cos, sin):
    up = pltpu.roll(y, LANES - ROPE_PAIR, 1)
    down = pltpu.roll(y, ROPE_PAIR, 1)
    partner = jnp.where((_lane_iota(y.shape) & ROPE_PAIR) == 0, up, down)
    return y * cos + partner * sin


def _inproj_kernel(roles, fuse_residual, *refs):
    refs = list(refs)
    x_ref = refs.pop(0)
    if fuse_residual:
        y_ref = refs.pop(0)
        pmod_ref = refs.pop(0)
    mod_ref = refs.pop(0)
    norm_ref = refs.pop(0)
    cos_ref = refs.pop(0)
    sin_ref = refs.pop(0)
    w_refs, g_refs = [], []
    for kind, _, _, _, _ in roles:
        w_refs.append(refs.pop(0))
        g_refs.append(refs.pop(0) if kind in ("norm", "norm_rope") else None)
    if fuse_residual:
        xo_ref = refs.pop(0)
    out_refs = refs

    x = x_ref[0]
    if fuse_residual:
        x = x + pmod_ref[0, 0, 5:6, :] * y_ref[0]
        xo_ref[0] = x
    xm = _rms(x, norm_ref[...]) * (1.0 + mod_ref[0, 0, 1:2, :]) + mod_ref[0, 0, 0:1, :]
    xm = xm.astype(BF16)
    cos = cos_ref[...]
    sin = sin_ref[...]
    for (kind, width, scale, transposed, unit), w_ref, g_ref, o_ref in zip(roles, w_refs, g_refs, out_refs):
        acc = _dot(xm, w_ref[...])
        if kind in ("plain", "plain_f32") and not transposed:
            o_ref[0] = acc.astype(o_ref.dtype)
            continue
        if unit:
            ones = jnp.ones((ONES_ROWS, TOK_TILE), o_ref.dtype)
            for j in range(width // LANES):
                yt = acc[:, j * LANES:(j + 1) * LANES].T.astype(o_ref.dtype)
                for k in range(LANES // unit):
                    base = (j * (LANES // unit) + k) * (unit + ONES_ROWS)
                    o_ref[0, base:base + unit, :] = yt[k * unit:(k + 1) * unit]
                    o_ref[0, base + unit:base + unit + ONES_ROWS, :] = ones
            continue
        for j in range(width // LANES):
            y = acc[:, j * LANES:(j + 1) * LANES]
            if kind in ("norm", "norm_rope"):
                y = _head_norm(y, g_ref[...])
            if kind == "norm_rope":
                y = _rope(y, cos, sin)
            if scale != 1.0:
                y = y * scale
            if transposed:
                o_ref[0, j * LANES:(j + 1) * LANES, :] = y.T.astype(o_ref.dtype)
            else:
                o_ref[0, :, j * LANES:(j + 1) * LANES] = y.astype(o_ref.dtype)


def _inproj_call(x, mods, norm_g, cos, sin, roles, weights, gains, residual=None):
    b, s, d = x.shape
    nt = s // TOK_TILE
    tok = lambda w: pl.BlockSpec((1, TOK_TILE, w), lambda i, t: (i, t, 0))
    mod_spec = pl.BlockSpec((1, 1, N_MOD, d), lambda i, t: (i, jnp.minimum(t, 1), 0, 0))
    args, specs = [x], [tok(d)]
    if residual is not None:
        y, pmods = residual
        args += [y, pmods]
        specs += [tok(d), mod_spec]
    args += [mods, norm_g.reshape(1, d), cos, sin]
    specs += [mod_spec, pl.BlockSpec((1, d), lambda i, t: (0, 0)),
              pl.BlockSpec((TOK_TILE, LANES), lambda i, t: (t, 0)),
              pl.BlockSpec((TOK_TILE, LANES), lambda i, t: (t, 0))]
    for (kind, width, _, _, _), w, g in zip(roles, weights, gains):
        args.append(w)
        specs.append(pl.BlockSpec((d, width), lambda i, t: (0, 0)))
        if kind in ("norm", "norm_rope"):
            args.append(g)
            specs.append(pl.BlockSpec((1, LANES), lambda i, t: (0, 0)))
    out_shapes, out_specs = [], []
    if residual is not None:
        out_shapes.append(jax.ShapeDtypeStruct((b, s, d), F32))
        out_specs.append(tok(d))
    for kind, width, _, transposed, unit in roles:
        dt = F32 if kind == "plain_f32" else BF16
        if transposed:
            rows = width // unit * (unit + ONES_ROWS) if unit else width
            out_shapes.append(jax.ShapeDtypeStruct((b, rows, s), dt))
            out_specs.append(pl.BlockSpec((1, rows, TOK_TILE), lambda i, t: (i, 0, t)))
        else:
            out_shapes.append(jax.ShapeDtypeStruct((b, s, width), dt))
            out_specs.append(tok(width))
    return pl.pallas_call(
        functools.partial(_inproj_kernel, tuple(roles), residual is not None),
        grid=(b, nt),
        in_specs=specs,
        out_specs=out_specs,
        out_shape=out_shapes,
        compiler_params=_cparams(("parallel", "parallel")),
        name="in_proj",
    )(*args)


def _rope_tables(s_total, ctx_len):
    t = jnp.arange(s_total - ctx_len, dtype=jnp.int32)
    pos = jnp.stack([t // GRID_W, t % GRID_W], axis=-1).astype(F32)
    n_freq = HEAD_DIM // 4
    inv_freq = ROPE_THETA ** (-jnp.arange(n_freq, dtype=F32) / n_freq)
    ang = pos[:, :, None] * inv_freq
    cos, sin = jnp.cos(ang), jnp.sin(ang)
    cos64 = jnp.concatenate([cos[:, 0], cos[:, 0], cos[:, 1], cos[:, 1]], axis=-1)
    sin64 = jnp.concatenate([-sin[:, 0], sin[:, 0], -sin[:, 1], sin[:, 1]], axis=-1)
    cos128 = jnp.concatenate([jnp.ones((ctx_len, LANES), F32), jnp.tile(cos64, (1, 2))], axis=0)
    sin128 = jnp.concatenate([jnp.zeros((ctx_len, LANES), F32), jnp.tile(sin64, (1, 2))], axis=0)
    return cos128, sin128


def _attend_all(chains, k_ref, vt_ref, is_latent, lazy_ref):
    t_lat = k_ref.shape[1] - TOK_TILE
    chunk = math.gcd(t_lat, KV_CHUNK)

    def step(keys, carry):
        out = []
        for (qt, key_lanes, value_rows, acc_ref), m_prev in zip(chains, carry):
            s = _dot(k_ref[0, keys, key_lanes], qt)
            m_new = jnp.maximum(m_prev, jnp.max(s, axis=0, keepdims=True))
            p = jnp.exp2(s - m_new)
            acc_ref[...] = jnp.exp2(m_prev - m_new) * acc_ref[...] + _dot(vt_ref[0, value_rows, keys], p.astype(BF16))
            out.append(m_new)
        return tuple(out)

    for _, _, _, acc_ref in chains:
        acc_ref[...] = jnp.zeros(acc_ref.shape, F32)

    def lazy_step(keys, carry):
        out = []
        for (qt, key_lanes, value_rows, acc_ref), m_prev in zip(chains, carry):
            s = _dot(k_ref[0, keys, key_lanes], qt)
            p = jnp.exp2(s - m_prev)
            m_new = jnp.maximum(m_prev, jnp.max(s, axis=0, keepdims=True))
            acc_ref[...] = jnp.exp2(m_prev - m_new) * (acc_ref[...] + _dot(vt_ref[0, value_rows, keys], p.astype(BF16)))
            out.append(m_new)
        return tuple(out)

    def all_chunks(step_fn, m_init):
        carry = step_fn(slice(0, TOK_TILE), tuple(jnp.full((1, c[0].shape[1]), m_init, F32) for c in chains))

        def body(j, carry):
            return step_fn(pl.ds(pl.multiple_of(TOK_TILE + j * chunk, TOK_TILE), chunk), carry)
        lax.fori_loop(0, jnp.where(is_latent, t_lat // chunk, 0), body, carry)

    lazy_ok = lazy_ref[0] > 0.0
    pl.when(lazy_ok)(lambda: all_chunks(lazy_step, -lazy_ref[1]))
    pl.when(jnp.logical_not(lazy_ok))(lambda: all_chunks(step, NEG))
    outs = []
    for _, _, _, acc_ref in chains:
        dv = acc_ref.shape[0] - ONES_ROWS
        outs.append(acc_ref[0:dv, :] / acc_ref[dv:dv + 1, :])
    return outs


def _lazy_softmax_ok(q_gain, k_gain, q_scale):
    bound = NORM_SLACK * HEAD_DIM * jnp.max(jnp.abs(q_gain)) * jnp.max(jnp.abs(k_gain)) * q_scale
    return jnp.stack([(2.0 * bound <= LAZY_EXP_LIMIT).astype(F32), bound.astype(F32)])


def _half_masks(q):
    lo = _lane_iota(q.shape) < HEAD_DIM
    zero = jnp.zeros_like(q)
    return jnp.where(lo, q, zero), jnp.where(lo, zero, q)


def _row_half_masks(qt):
    top = lax.broadcasted_iota(jnp.int32, qt.shape, 0) < HEAD_DIM
    zero = jnp.zeros_like(qt)
    return jnp.where(top, qt, zero), jnp.where(top, zero, qt)


def _diff_attn_kernel(lam_init, lazy_ref, qt_ref, k_ref, vt_ref, lam_ref, subln_ref, o_ref, *acc_refs):
    chains = []
    for j, acc_ref in enumerate(acc_refs):
        lanes = slice(j * LANES, (j + 1) * LANES)
        q1, q2 = _row_half_masks(qt_ref[0, lanes, :])
        values = slice(j * (LANES + ONES_ROWS), (j + 1) * (LANES + ONES_ROWS))
        chains.append((jnp.concatenate([q1, q2], axis=1), lanes, values, acc_ref))
    outs = _attend_all(chains, k_ref, vt_ref, pl.program_id(2) > 0, lazy_ref)
    lv = lam_ref[...]
    lam = (jnp.exp(jnp.sum(lv[0:1] * lv[1:2], axis=-1, keepdims=True))
           - jnp.exp(jnp.sum(lv[2:3] * lv[3:4], axis=-1, keepdims=True)) + lam_init)
    for j, o in enumerate(outs):
        od = o[:, :TOK_TILE] - lam * o[:, TOK_TILE:]
        ms = jnp.mean(od * od, axis=0, keepdims=True)
        on = od * lax.rsqrt(ms + EPS) * subln_ref[...] * (1.0 - lam_init)
        o_ref[0, :, j * LANES:(j + 1) * LANES] = on.T.astype(o_ref.dtype)


def _diff_attn_call(lazy_ok, qt, k, vt, lam_vecs, subln, lam_init):
    b, s, w = k.shape
    wide = ATTN_CHAINS * LANES
    return pl.pallas_call(
        functools.partial(_diff_attn_kernel, lam_init),
        grid=(b, w // wide, s // TOK_TILE),
        in_specs=[
            pl.BlockSpec(memory_space=pltpu.SMEM),
            pl.BlockSpec((1, wide, TOK_TILE), lambda i, h, t: (i, h, t)),
            pl.BlockSpec((1, s, wide), lambda i, h, t: (i, 0, h)),
            pl.BlockSpec((1, ATTN_CHAINS * (LANES + ONES_ROWS), s), lambda i, h, t: (i, h, 0)),
            pl.BlockSpec((4, HEAD_DIM), lambda i, h, t: (0, 0)),
            pl.BlockSpec((LANES, 1), lambda i, h, t: (0, 0)),
        ],
        out_specs=pl.BlockSpec((1, TOK_TILE, wide), lambda i, h, t: (i, t, h)),
        out_shape=jax.ShapeDtypeStruct((b, s, w), BF16),
        scratch_shapes=[pltpu.VMEM((LANES + ONES_ROWS, 2 * TOK_TILE), F32)] * ATTN_CHAINS,
        compiler_params=_cparams(("parallel", "parallel", "parallel")),
        name="diff_attn",
    )(lazy_ok, qt, k, vt, lam_vecs, subln.reshape(LANES, 1))


def _gqa_attn_kernel(lazy_ref, qt_ref, k_ref, vt_ref, o_ref, *acc_refs):
    t = TOK_TILE
    chains = []
    for g, acc_ref in enumerate(acc_refs):
        qa = _row_half_masks(qt_ref[0, 2 * g * LANES:(2 * g + 1) * LANES, :])
        qb = _row_half_masks(qt_ref[0, (2 * g + 1) * LANES:(2 * g + 2) * LANES, :])
        qt = jnp.concatenate([qa[0], qa[1], qb[0], qb[1]], axis=1)
        values = slice(g * (HEAD_DIM + ONES_ROWS), (g + 1) * (HEAD_DIM + ONES_ROWS))
        chains.append((qt, slice(g * LANES, (g + 1) * LANES), values, acc_ref))
    outs = _attend_all(chains, k_ref, vt_ref, pl.program_id(1) > 0, lazy_ref)
    ot = jnp.concatenate([o[:, j * t:(j + 1) * t] for o in outs for j in range(GQA_REP)], axis=0)
    o_ref[0] = ot.T.astype(o_ref.dtype)


def _gqa_attn_call(lazy_ok, qt, k_dup, vt):
    b, w, s = qt.shape
    groups = w // (2 * LANES)
    return pl.pallas_call(
        _gqa_attn_kernel,
        grid=(b, s // TOK_TILE),
        in_specs=[pl.BlockSpec(memory_space=pltpu.SMEM),
                  pl.BlockSpec((1, w, TOK_TILE), lambda i, t: (i, 0, t)),
                  pl.BlockSpec((1, s, groups * LANES), lambda i, t: (i, 0, 0)),
                  pl.BlockSpec((1, groups * (HEAD_DIM + ONES_ROWS), s), lambda i, t: (i, 0, 0))],
        out_specs=pl.BlockSpec((1, TOK_TILE, w), lambda i, t: (i, t, 0)),
        out_shape=jax.ShapeDtypeStruct((b, s, w), BF16),
        scratch_shapes=[pltpu.VMEM((HEAD_DIM + ONES_ROWS, 4 * TOK_TILE), F32)] * groups,
        compiler_params=_cparams(("parallel", "parallel")),
        name="gqa_attn",
    )(lazy_ok, qt, k_dup, vt)


def _na_attn_kernel(n_rows, q_ref, k_ref, v_ref, bias_ref, o_ref):
    t = pl.program_id(2)

    @pl.when(t == 0)
    def _():
        o_ref[0] = jnp.zeros(o_ref.shape[1:], o_ref.dtype)

    @pl.when(t > 0)
    def _():
        kctx = k_ref[0, 0:TOK_TILE, :]
        vctx = v_ref[0, 0:TOK_TILE, :]
        win = NA_WIN_ROWS * GRID_W
        rows_per_tile = TOK_TILE // GRID_W
        lo = _lane_iota((GRID_W, LANES)) < HEAD_DIM
        for i in range(rows_per_tile):
            r = (t - 1) * rows_per_tile + i
            r0 = jnp.clip(r - NA_WIN_ROWS // 2, 0, n_rows - NA_WIN_ROWS)
            off = pl.multiple_of(TOK_TILE + r0 * GRID_W, GRID_W)
            q1, q2 = _half_masks(q_ref[0, i * GRID_W:(i + 1) * GRID_W, :])
            q = jnp.concatenate([q1, q2], axis=0)
            s_nb = _dot(q, k_ref[0, pl.ds(off, win), :], _NT) + bias_ref[0, r - r0]
            s_cx = _dot(q, kctx, _NT)
            m = jnp.maximum(jnp.max(s_nb, axis=-1, keepdims=True), jnp.max(s_cx, axis=-1, keepdims=True))
            p_nb = jnp.exp2(s_nb - m)
            p_cx = jnp.exp2(s_cx - m)
            den = jnp.sum(p_nb, axis=-1, keepdims=True) + jnp.sum(p_cx, axis=-1, keepdims=True)
            o = (_dot(p_nb.astype(BF16), v_ref[0, pl.ds(off, win), :]) + _dot(p_cx.astype(BF16), vctx)) / den
            o_ref[0, i * GRID_W:(i + 1) * GRID_W, :] = jnp.where(lo, o[:GRID_W], o[GRID_W:]).astype(o_ref.dtype)


def _na_bias_table(rpb):
    kr = NA_WIN_ROWS
    cols = jnp.arange(GRID_W, dtype=jnp.int32)
    c0 = jnp.clip(cols - NA_WIN_COLS // 2, 0, GRID_W - NA_WIN_COLS)
    kc = jnp.arange(GRID_W, dtype=jnp.int32)
    inside = (kc[None, :] >= c0[:, None]) & (kc[None, :] < c0[:, None] + NA_WIN_COLS)
    dc = kc[None, :] - cols[:, None] + (NA_WIN_COLS - 1)
    onehot = ((dc[:, :, None] == jnp.arange(2 * NA_WIN_COLS - 1)) & inside[:, :, None]).astype(F32)
    by_col = jnp.einsum("hrd,ckd->hrck", rpb.astype(F32), onehot, precision=lax.Precision.HIGHEST)
    by_col = jnp.where(inside[None, None], by_col * LOG2E, NEG)
    tab = jnp.stack([by_col[:, NA_WIN_ROWS - 1 - var:2 * NA_WIN_ROWS - 1 - var]
                     for var in range(NA_WIN_ROWS)], axis=1)
    h = rpb.shape[0]
    tab = tab.transpose(0, 1, 3, 2, 4).reshape(h // 2, 2, NA_WIN_ROWS, GRID_W, kr * GRID_W)
    return tab.transpose(0, 2, 1, 3, 4).reshape(h // 2, NA_WIN_ROWS, 2 * GRID_W, kr * GRID_W)


def _na_attn_call(q, k, v, bias):
    b, s, w = q.shape
    n_rows = (s - TOK_TILE) // GRID_W
    seq = pl.BlockSpec((1, s, LANES), lambda i, h, t: (i, 0, h))
    tile = pl.BlockSpec((1, TOK_TILE, LANES), lambda i, h, t: (i, t, h))
    return pl.pallas_call(
        functools.partial(_na_attn_kernel, n_rows),
        grid=(b, w // LANES, s // TOK_TILE),
        in_specs=[tile, seq, seq,
                  pl.BlockSpec((1,) + bias.shape[1:], lambda i, h, t: (h, 0, 0, 0))],
        out_specs=tile,
        out_shape=jax.ShapeDtypeStruct((b, s, w), BF16),
        compiler_params=_cparams(("parallel", "parallel", "parallel")),
        name="na_attn",
    )(q, k, v, bias)


def _pool_kernel(s_total, u_ref, w_ref, scale_ref, o_ref):
    t = pl.program_id(1)
    t0 = t * TOK_TILE
    seg_lo = jnp.where(t == 0, 0, TOK_TILE)
    seg_hi = jnp.where(t == 0, TOK_TILE, s_total)
    span = TOK_TILE + 2 * POOL_HALO
    start = pl.multiple_of(jnp.clip(t0 - POOL_HALO, 0, s_total - span), SUBLANES)
    hi, lo = _split(u_ref[0, pl.ds(start, span), :])
    own = u_ref[0, pl.ds(pl.multiple_of(t0, TOK_TILE), TOK_TILE), :]
    tok_q = t0 + lax.broadcasted_iota(jnp.int32, (TOK_TILE, span), 0)
    tok_k = start + lax.broadcasted_iota(jnp.int32, (TOK_TILE, span), 1)
    group = _lane_iota(own.shape) // HEAD_DIM
    mean = jnp.zeros(own.shape, F32)
    for g, win in enumerate(POOL_WINDOWS):
        lo_t = jnp.maximum(tok_q - win // 2, seg_lo)
        hi_t = jnp.minimum(tok_q + win // 2, seg_hi)
        band = jnp.where(tok_k >= lo_t, jnp.where(tok_k < hi_t, 1.0, 0.0), 0.0).astype(BF16)
        count = (hi_t - lo_t)[:, 0:1].astype(F32)
        total = _dot(band, hi) + _dot(band, lo)
        mean = jnp.where(group == g, total / count, mean)
    p = (mean - own).astype(BF16)
    o_ref[0] = (_dot(p, w_ref[...]) * scale_ref[...]).astype(o_ref.dtype)


def _pool_call(u, w_blockdiag, scale):
    b, s, w = u.shape
    return pl.pallas_call(
        functools.partial(_pool_kernel, s),
        grid=(b, s // TOK_TILE),
        in_specs=[pl.BlockSpec((1, s, w), lambda i, t: (i, 0, 0)),
                  pl.BlockSpec((w, w), lambda i, t: (0, 0)),
                  pl.BlockSpec((1, w), lambda i, t: (0, 0))],
        out_specs=pl.BlockSpec((1, TOK_TILE, w), lambda i, t: (i, t, 0)),
        out_shape=jax.ShapeDtypeStruct((b, s, w), BF16),
        compiler_params=_cparams(("parallel", "parallel")),
        name="pool_mix",
    )(u, w_blockdiag, scale.reshape(1, w))


def _outproj_kernel(n_parts, *refs):
    y_refs = refs[:n_parts]
    w_refs = refs[n_parts:2 * n_parts]
    x_ref, mod_ref, norm_ref, x1_ref, xm_ref = refs[2 * n_parts:]
    acc = _dot(y_refs[0][0], w_refs[0][...])
    for y_ref, w_ref in zip(y_refs[1:], w_refs[1:]):
        acc = acc + _dot(y_ref[0], w_ref[...])
    x1 = x_ref[0] + mod_ref[0, 0, 2:3, :] * acc
    x1_ref[0] = x1
    xm = _rms(x1, norm_ref[...]) * (1.0 + mod_ref[0, 0, 4:5, :]) + mod_ref[0, 0, 3:4, :]
    xm_ref[0] = xm.astype(xm_ref.dtype)


def _outproj_call(parts, weights, x, mods, norm_g):
    b, s, d = x.shape
    tok = lambda w: pl.BlockSpec((1, TOK_TILE, w), lambda i, t: (i, t, 0))
    specs = [tok(p.shape[-1]) for p in parts]
    specs += [pl.BlockSpec(w.shape, lambda i, t: (0, 0)) for w in weights]
    specs += [tok(d),
              pl.BlockSpec((1, 1, N_MOD, d), lambda i, t: (i, jnp.minimum(t, 1), 0, 0)),
              pl.BlockSpec((1, d), lambda i, t: (0, 0))]
    return pl.pallas_call(
        functools.partial(_outproj_kernel, len(parts)),
        grid=(b, s // TOK_TILE),
        in_specs=specs,
        out_specs=[tok(d), tok(d)],
        out_shape=[jax.ShapeDtypeStruct((b, s, d), F32), jax.ShapeDtypeStruct((b, s, d), BF16)],
        compiler_params=_cparams(("parallel", "parallel")),
        name="out_proj",
    )(*parts, *weights, x, mods, norm_g.reshape(1, d))


def _peer_fold_kernel(k_ref, wt_ref, kw_ref):
    k_hi, k_lo = _split(k_ref[0])
    w_hi, w_lo = _split(wt_ref[0])
    kw_ref[0] = _dot3(k_hi, k_lo, w_hi, w_lo).astype(kw_ref.dtype)


def _peer_fold_call(keys_ph, wq_t):
    n, nk, kd = keys_ph.shape
    d = wq_t.shape[-1]
    blk = pl.BlockSpec((1, nk, d), lambda i: (i, 0, 0))
    return pl.pallas_call(
        _peer_fold_kernel,
        grid=(n,),
        in_specs=[pl.BlockSpec((1, nk, kd), lambda i: (i, 0, 0)), pl.BlockSpec((1, kd, d), lambda i: (i, 0, 0))],
        out_specs=blk,
        out_shape=jax.ShapeDtypeStruct((n, nk, d), BF16),
        compiler_params=_cparams(("parallel",)),
        name="peer_fold",
    )(keys_ph, wq_t)


def _peer_scores_kernel(kw_ref, x_ref, st_ref):
    st_ref[...] = _dot(kw_ref[...], x_ref[...], _NT)


def _peer_scores_call(kw, x, tm):
    n, d = x.shape
    r = kw.shape[0]
    return pl.pallas_call(
        _peer_scores_kernel,
        grid=(n // tm,),
        in_specs=[pl.BlockSpec((r, d), lambda t: (0, 0)), pl.BlockSpec((tm, d), lambda t: (t, 0))],
        out_specs=pl.BlockSpec((r, tm), lambda t: (0, t)),
        out_shape=jax.ShapeDtypeStruct((r, n), F32),
        compiler_params=_cparams(("parallel",)),
        name="peer_scores",
    )(kw, x)


def _bitonic_merge_desc(v):
    n = len(v)
    if n == 1:
        return v
    half = n // 2
    top = [jnp.maximum(v[i], v[i + half]) for i in range(half)]
    bot = [jnp.minimum(v[i], v[i + half]) for i in range(half)]
    return _bitonic_merge_desc(top) + _bitonic_merge_desc(bot)


def _sort_desc(v):
    n = len(v)
    if n == 1:
        return v
    return _bitonic_merge_desc(_sort_desc(v[:n // 2]) + _sort_desc(v[n // 2:])[::-1])


def _merge_top(a, b):
    n = len(a)
    return _bitonic_merge_desc([jnp.maximum(a[i], b[n - 1 - i]) for i in range(n)])


def _peer_select_kernel(st_ref, th_ref, e1_ref, e2_ref):
    k = PEER_TOPK
    nk = PEER_N_KEYS
    half_rows = PEER_HEADS * PEER_N_KEYS
    tops = []
    for p in range(2):
        groups = []
        for g in range(PEER_N_KEYS // k):
            vals = [st_ref[pl.ds(p * half_rows + g * k + j, PEER_HEADS, stride=PEER_N_KEYS), :]
                    for j in range(k)]
            groups.append(_sort_desc(vals))
        while len(groups) > 1:
            groups = [_merge_top(groups[i], groups[i + 1]) for i in range(0, len(groups), 2)]
        tops.append(groups[0])
    t1, t2 = tops
    neg = jnp.full(t1[0].shape, NEG, F32)
    rows = [[t1[i] + t2[j] for j in range(k // (i + 1))] for i in range(k)]
    first = _merge_top(rows[0], rows[1] + [neg] * (k - len(rows[1])))
    rest = [c for row in rows[2:] for c in row]
    rest = _sort_desc(rest + [neg] * (2 * k - len(rest)))[:k]
    top = [jnp.maximum(first[i], rest[k - 1 - i]) for i in range(k)]
    tau = functools.reduce(jnp.minimum, top)
    m1, m2 = t1[0], t2[0]
    rz = 1.0 / functools.reduce(lambda a, c: a + c, [jnp.exp(c - rows[0][0]) for c in top])
    th_rank = []
    for i in range(k):
        th_i = jnp.full(tau.shape, -NEG, F32)
        for j in range(len(rows[i])):
            th_i = jnp.where(rows[i][j] >= tau, t2[j], th_i)
        th_rank.append(th_i)
    big = jnp.full((nk, LANES), -NEG, F32)
    for h in range(PEER_HEADS):
        hs = slice(h, h + 1)
        head = slice(h * nk, (h + 1) * nk)
        s1 = st_ref[head, :]
        s2 = st_ref[half_rows + h * nk:half_rows + (h + 1) * nk, :]
        th = big
        for i in range(k):
            th = jnp.where(s1 == t1[i][hs], th_rank[i][hs], th)
        th_ref[head, :] = th
        e1_ref[head, :] = jnp.exp(s1 - m1[hs]) * rz[hs]
        e2_ref[head, :] = jnp.exp(s2 - m2[hs])


def _peer_select_call(st):
    r, n = st.shape
    out = pl.BlockSpec((r // 2, LANES), lambda t: (0, t))
    return pl.pallas_call(
        _peer_select_kernel,
        grid=(n // LANES,),
        in_specs=[pl.BlockSpec((r, LANES), lambda t: (0, t))],
        out_specs=[out, out, out],
        out_shape=[jax.ShapeDtypeStruct((r // 2, n), F32)] * 3,
        compiler_params=_cparams(("parallel",)),
        name="peer_select",
    )(st)


def _peer_dense_kernel(x_ref, th_ref, e1_ref, s2_ref, e2_ref, u_ref, vt_ref, y_ref, acc_ref, w_ref):
    c = pl.program_id(1)
    tm = x_ref.shape[0]
    nk = PEER_N_KEYS
    part_keys = SUBLANES // PEER_KEY_PARTS
    n_sub = nk // PEER_SUB_KEYS

    @pl.when(c == 0)
    def _():
        acc_ref[...] = jnp.zeros(acc_ref.shape, F32)

    def block(i, carry):
        cols = pl.ds(pl.multiple_of((i // n_sub) * LANES, LANES), LANES)
        sub = (i % n_sub) * PEER_SUB_KEYS
        for part in range(PEER_KEY_PARTS):
            w = [None] * part_keys
            for h in range(PEER_HEADS):
                first = pl.ds(pl.multiple_of(h * nk + c * SUBLANES, SUBLANES), SUBLANES)
                second = pl.ds(pl.multiple_of(h * nk + sub, PEER_SUB_KEYS), PEER_SUB_KEYS)
                th8, e18 = th_ref[first, cols], e1_ref[first, cols]
                s2, e2 = s2_ref[second, cols], e2_ref[second, cols]
                for q in range(part_keys):
                    r = part * part_keys + q
                    gate = jnp.where(s2 >= th8[r:r + 1, :], e18[r:r + 1, :] * e2, 0.0)
                    w[q] = gate if h == 0 else w[q] + gate
            for q in range(part_keys):
                rows = pl.ds(pl.multiple_of((part * part_keys + q) * nk + sub, PEER_SUB_KEYS), PEER_SUB_KEYS)
                w_ref[rows, cols] = w[q]
        return carry

    lax.fori_loop(0, (tm // LANES) * n_sub, block, 0)
    hv = _dot(u_ref[...], x_ref[...], _NT)
    g = w_ref[...] * (0.5 * hv * (1.0 + lax.erf(hv * SQRT_HALF)))
    acc_ref[...] += _dot(vt_ref[...], g.astype(BF16))

    @pl.when(c == pl.num_programs(1) - 1)
    def _():
        y_ref[...] = acc_ref[...].T


def _peer_dense_call(x, st, th, e1, e2, u_bf, vt_bf, tm):
    n, d = x.shape
    n_exp = u_bf.shape[0]
    ne = SUBLANES * PEER_N_KEYS
    half = pl.BlockSpec((th.shape[0], tm), lambda t, c: (0, t))
    return pl.pallas_call(
        _peer_dense_kernel,
        grid=(n // tm, n_exp // ne),
        in_specs=[
            pl.BlockSpec((tm, d), lambda t, c: (t, 0)),
            half, half,
            pl.BlockSpec((th.shape[0], tm), lambda t, c: (1, t)),
            half,
            pl.BlockSpec((ne, d), lambda t, c: (c, 0)),
            pl.BlockSpec((d, ne), lambda t, c: (0, c)),
        ],
        out_specs=pl.BlockSpec((tm, d), lambda t, c: (t, 0)),
        out_shape=jax.ShapeDtypeStruct((n, d), F32),
        scratch_shapes=[pltpu.VMEM((d, tm), F32), pltpu.VMEM((ne, tm), F32)],
        compiler_params=_cparams(("parallel", "arbitrary")),
        name="peer_dense",
    )(x, th, e1, st, e2, u_bf, vt_bf)


def _peer(xm, wq, keys, u, v):
    b, s, d = xm.shape
    n = b * s
    heads, _, nk, kd = keys.shape
    keys_ph = keys.transpose(1, 0, 2, 3).reshape(2 * heads, nk, kd)
    wq_t = wq.T.reshape(heads, 2, kd, d).transpose(1, 0, 2, 3).reshape(2 * heads, kd, d)
    kw = _peer_fold_call(keys_ph, wq_t).reshape(2 * heads * nk, d)
    tm = PEER_TOK_TILE if n % PEER_TOK_TILE == 0 else TOK_TILE
    x = xm.reshape(n, d)
    st = _peer_scores_call(kw, x, tm)
    th, e1, e2 = _peer_select_call(st)
    y = _peer_dense_call(x, st, th, e1, e2, u.astype(BF16), v.T.astype(BF16), tm)
    return y.reshape(b, s, d)


def _final_kernel(x_ref, y_ref, mod_ref, o_ref):
    o_ref[0] = x_ref[0] + mod_ref[0, 0, 5:6, :] * y_ref[0]


def _final_call(x1, y, mods):
    b, s, d = x1.shape
    nt = s // TOK_TILE - 1
    src = pl.BlockSpec((1, TOK_TILE, d), lambda i, t: (i, t + 1, 0))
    return pl.pallas_call(
        _final_kernel,
        grid=(b, nt),
        in_specs=[src, src, pl.BlockSpec((1, 1, N_MOD, d), lambda i, t: (i, 1, 0, 0))],
        out_specs=pl.BlockSpec((1, TOK_TILE, d), lambda i, t: (i, t, 0)),
        out_shape=jax.ShapeDtypeStruct((b, nt * TOK_TILE, d), F32),
        compiler_params=_cparams(("parallel", "parallel")),
        name="final_residual",
    )(x1, y, mods)


def _gain2(g):
    return jnp.concatenate([g, g]).reshape(1, LANES).astype(F32)


def kernel(x, c, ctx, c_ctx, l0_ada_w, l0_ada_b, l0_norm1, l0_norm2, l0_w_in, l0_w_out, l0_pool_w, l0_pool_scale, l0_diff_qnorm, l0_diff_knorm, l0_lambda_q1, l0_lambda_k1, l0_lambda_q2, l0_lambda_k2, l0_diff_subln, l0_peer_wq, l0_peer_keys, l0_peer_u, l0_peer_v, l1_ada_w, l1_ada_b, l1_norm1, l1_norm2, l1_w_in, l1_w_out, l1_gqa_qnorm, l1_gqa_knorm, l1_na_qnorm, l1_na_knorm, l1_na_rpb, l1_peer_wq, l1_peer_keys, l1_peer_u, l1_peer_v):
    b, t_lat, d = x.shape
    ctx_len = ctx.shape[1]
    assert ctx_len == TOK_TILE and t_lat % TOK_TILE == 0 and t_lat // GRID_W >= NA_WIN_ROWS
    s = ctx_len + t_lat
    xs = jnp.concatenate([ctx, x], axis=1)
    cos, sin = _rope_tables(s, ctx_len)
    qk_scale = HEAD_DIM ** -0.5 * LOG2E

    mods0 = _modulation(c, c_ctx, l0_ada_w, l0_ada_b)
    w0 = l0_w_in.astype(BF16)
    pw = l0_pool_scale.shape[0]
    dw = DIFF_HEADS * 2 * HEAD_DIM
    roles0 = [("plain_f32", pw, 1.0, False, 0), ("norm_rope", dw, qk_scale, True, 0),
              ("norm_rope", dw, 1.0, False, 0), ("plain", dw, 1.0, True, LANES)]
    weights0 = [w0[:, :pw], w0[:, pw:pw + dw], w0[:, pw + dw:pw + 2 * dw], w0[:, pw + 2 * dw:]]
    gains0 = [None, _gain2(l0_diff_qnorm), _gain2(l0_diff_knorm), None]
    u0, q0, k0, v0 = _inproj_call(xs, mods0, l0_norm1, cos, sin, roles0, weights0, gains0)
    lam_init = 0.8 - 0.6 * math.exp(-0.3 * 0)
    lam_vecs = jnp.stack([l0_lambda_q1, l0_lambda_k1, l0_lambda_q2, l0_lambda_k2]).astype(F32)
    lazy0 = _lazy_softmax_ok(l0_diff_qnorm, l0_diff_knorm, qk_scale)
    o_diff = _diff_attn_call(lazy0, q0, k0, v0, lam_vecs, l0_diff_subln, lam_init)
    pool_bd = jax.scipy.linalg.block_diag(*[l0_pool_w[g] for g in range(l0_pool_w.shape[0])]).astype(BF16)
    y_pool = _pool_call(u0, pool_bd, l0_pool_scale)
    wo0 = l0_w_out.astype(BF16)
    x1, xm0 = _outproj_call([y_pool, o_diff], [wo0[:pw], wo0[pw:]], xs, mods0, l0_norm2)
    y_peer0 = _peer(xm0, l0_peer_wq, l0_peer_keys, l0_peer_u, l0_peer_v)

    mods1 = _modulation(c, c_ctx, l1_ada_w, l1_ada_b)
    w1 = l1_w_in.astype(BF16)
    n_q = 8 * HEAD_DIM
    n_kv = GQA_KV_HEADS * HEAD_DIM
    o_ck, o_cv, o_nq, o_nk, o_nv = n_q, n_q + n_kv, n_q + 2 * n_kv, 2 * n_q + 2 * n_kv, 3 * n_q + 2 * n_kv
    dup = lambda w: jnp.concatenate([w[:, :HEAD_DIM], w[:, :HEAD_DIM], w[:, HEAD_DIM:], w[:, HEAD_DIM:]], axis=1)
    roles1 = [("norm_rope", n_q, qk_scale, True, 0), ("norm_rope", 2 * n_kv, 1.0, False, 0),
              ("plain", n_kv, 1.0, True, HEAD_DIM), ("norm", n_q, qk_scale, False, 0), ("norm", n_q, 1.0, False, 0),
              ("plain", n_q, 1.0, False, 0)]
    weights1 = [w1[:, :o_ck], dup(w1[:, o_ck:o_cv]), w1[:, o_cv:o_nq],
                w1[:, o_nq:o_nk], w1[:, o_nk:o_nv], w1[:, o_nv:]]
    gains1 = [_gain2(l1_gqa_qnorm), _gain2(l1_gqa_knorm), None, _gain2(l1_na_qnorm), _gain2(l1_na_knorm), None]
    x2, cq, ckd, cvd, nq, nk_, nv = _inproj_call(x1, mods1, l1_norm1, cos, sin, roles1, weights1, gains1,
                                                 residual=(y_peer0, mods0))
    o_gqa = _gqa_attn_call(_lazy_softmax_ok(l1_gqa_qnorm, l1_gqa_knorm, qk_scale), cq, ckd, cvd)
    bias = _na_bias_table(l1_na_rpb)
    o_na = _na_attn_call(nq, nk_, nv, bias)
    wo1 = l1_w_out.astype(BF16)
    x3, xm1 = _outproj_call([o_gqa, o_na], [wo1[:n_q], wo1[n_q:]], x2, mods1, l1_norm2)
    y_peer1 = _peer(xm1, l1_peer_wq, l1_peer_keys, l1_peer_u, l1_peer_v)
    return _final_call(x3, y_peer1, mods1)
```

```python
import functools
import math

import jax
import jax.numpy as jnp
from jax import lax
from jax.experimental import pallas as pl
from jax.experimental.pallas import tpu as pltpu

F32 = jnp.float32
BF16 = jnp.bfloat16

LANES = 128
SUBLANES = 8
VMEM_LIMIT_BYTES = 56 * 1024 * 1024

HEAD_DIM = 64
GRID_W = 64
ROPE_THETA = 10000.0
EPS = 1e-6
N_MOD = 6
POOL_WINDOWS = (2, 4, 8, 16)
POOL_HALO = max(POOL_WINDOWS) // 2
DIFF_HEADS = 6
GQA_KV_HEADS = 2
GQA_REP = 4
ROPE_PAIR = HEAD_DIM // 4
NA_WIN_ROWS = 8
NA_WIN_COLS = 16
PEER_HEADS = 8
PEER_N_KEYS = 128
PEER_TOPK = 16
TOK_TILE = 256
PEER_TOK_TILE = 512
ADA_COLS = 768
KV_CHUNK = 4096
PEER_SUB_KEYS = 64
PEER_KEY_PARTS = 2
ATTN_CHAINS = 2
NA_PAIRS = 2
ONES_ROWS = 16
LOG2E = 1.4426950408889634
LAZY_EXP_LIMIT = 60.0
NORM_SLACK = 1.05
NEG = -1e30
SQRT_HALF = 0.7071067811865476


def _cparams(sem, vmem=VMEM_LIMIT_BYTES):
    return pltpu.CompilerParams(dimension_semantics=sem, vmem_limit_bytes=vmem)


def _split(x):
    hi = x.astype(BF16)
    lo = (x - hi.astype(F32)).astype(BF16)
    return hi, lo


_NN = (((1,), (0,)), ((), ()))
_NT = (((1,), (1,)), ((), ()))


def _dot(a, b, dims=_NN):
    return lax.dot_general(a, b, dims, preferred_element_type=F32)


def _dot3(a_hi, a_lo, b_hi, b_lo, dims=_NN):
    return _dot(a_hi, b_hi, dims) + _dot(a_hi, b_lo, dims) + _dot(a_lo, b_hi, dims)


def _lane_iota(shape):
    return lax.broadcasted_iota(jnp.int32, shape, len(shape) - 1)


def _rms(x, gain):
    ms = jnp.mean(x * x, axis=-1, keepdims=True)
    return x * lax.rsqrt(ms + EPS) * gain


def _ada_kernel(c_ref, w_ref, b_ref, o_ref):
    c = c_ref[...]
    a = c / (1.0 + jnp.exp(-c))
    a_hi, a_lo = _split(a)
    w_hi, w_lo = _split(w_ref[...])
    o_ref[...] = _dot3(a_hi, a_lo, w_hi, w_lo) + b_ref[...]


def _ada_call(cvec, ada_w, ada_b):
    rows, d = cvec.shape
    n = ada_w.shape[1]
    bn = ADA_COLS
    return pl.pallas_call(
        _ada_kernel,
        grid=(n // bn,),
        in_specs=[
            pl.BlockSpec((rows, d), lambda j: (0, 0)),
            pl.BlockSpec((d, bn), lambda j: (0, j)),
            pl.BlockSpec((1, bn), lambda j: (0, j)),
        ],
        out_specs=pl.BlockSpec((rows, bn), lambda j: (0, j)),
        out_shape=jax.ShapeDtypeStruct((rows, n), F32),
        compiler_params=_cparams(("parallel",)),
        name="ada_mod",
    )(cvec, ada_w, ada_b.reshape(1, n))


def _modulation(c, c_ctx, ada_w, ada_b):
    b, d = c.shape
    rows = -(-(b + 1) // 16) * 16
    cvec = jnp.zeros((rows, d), F32).at[0].set(c_ctx).at[1:b + 1].set(c)
    m = _ada_call(cvec, ada_w, ada_b).reshape(rows, N_MOD, d)
    return jnp.stack([jnp.broadcast_to(m[0], (b, N_MOD, d)), m[1:b + 1]], axis=1)


def _head_norm(y, gain):
    r = lax.broadcasted_iota(jnp.int32, (LANES, LANES), 0) // HEAD_DIM
    c = lax.broadcasted_iota(jnp.int32, (LANES, LANES), 1) // HEAD_DIM
    ones_bd = jnp.where(r == c, 1.0, 0.0).astype(BF16)
    hi, lo = _split(y * y)
    ss = _dot(hi, ones_bd) + _dot(lo, ones_bd)
    return y * lax.rsqrt(ss * (1.0 / HEAD_DIM) + EPS) * gain


def _rope(y, cos, sin):
    up = pltpu.roll(y, LANES - ROPE_PAIR, 1)
    down = pltpu.roll(y, ROPE_PAIR, 1)
    partner = jnp.where((_lane_iota(y.shape) & ROPE_PAIR) == 0, up, down)
    return y * cos + partner * sin


def _inproj_kernel(roles, fuse_residual, *refs):
    refs = list(refs)
    x_ref = refs.pop(0)
    if fuse_residual:
        y_ref = refs.pop(0)
        pmod_ref = refs.pop(0)
    mod_ref = refs.pop(0)
    norm_ref = refs.pop(0)
    cos_ref = refs.pop(0)
    sin_ref = refs.pop(0)
    w_refs, g_refs = [], []
    for kind, _, _, _, _ in roles:
        w_refs.append(refs.pop(0))
        g_refs.append(refs.pop(0) if kind in ("norm", "norm_rope") else None)
    if fuse_residual:
        xo_ref = refs.pop(0)
    out_refs = refs

    x = x_ref[0]
    if fuse_residual:
        x = x + pmod_ref[0, 0, 5:6, :] * y_ref[0]
        xo_ref[0] = x
    xm = _rms(x, norm_ref[...]) * (1.0 + mod_ref[0, 0, 1:2, :]) + mod_ref[0, 0, 0:1, :]
    xm = xm.astype(BF16)
    cos = cos_ref[...]
    sin = sin_ref[...]
    for (kind, width, scale, transposed, unit), w_ref, g_ref, o_ref in zip(roles, w_refs, g_refs, out_refs):
        acc = _dot(xm, w_ref[...])
        if kind in ("plain", "plain_f32") and not transposed:
            o_ref[0] = acc.astype(o_ref.dtype)
            continue
        if unit:
            ones = jnp.ones((ONES_ROWS, TOK_TILE), o_ref.dtype)
            for j in range(width // LANES):
                yt = acc[:, j * LANES:(j + 1) * LANES].T.astype(o_ref.dtype)
                for k in range(LANES // unit):
                    base = (j * (LANES // unit) + k) * (unit + ONES_ROWS)
                    o_ref[0, base:base + unit, :] = yt[k * unit:(k + 1) * unit]
                    o_ref[0, base + unit:base + unit + ONES_ROWS, :] = ones
            continue
        for j in range(width // LANES):
            y = acc[:, j * LANES:(j + 1) * LANES]
            if kind in ("norm", "norm_rope"):
                y = _head_norm(y, g_ref[...])
            if kind == "norm_rope":
                y = _rope(y, cos, sin)
            if scale != 1.0:
                y = y * scale
            if transposed:
                o_ref[0, j * LANES:(j + 1) * LANES, :] = y.T.astype(o_ref.dtype)
            else:
                o_ref[0, :, j * LANES:(j + 1) * LANES] = y.astype(o_ref.dtype)


def _inproj_call(x, mods, norm_g, cos, sin, roles, weights, gains, residual=None):
    b, s, d = x.shape
    nt = s // TOK_TILE
    tok = lambda w: pl.BlockSpec((1, TOK_TILE, w), lambda i, t: (i, t, 0))
    mod_spec = pl.BlockSpec((1, 1, N_MOD, d), lambda i, t: (i, jnp.minimum(t, 1), 0, 0))
    args, specs = [x], [tok(d)]
    if residual is not None:
        y, pmods = residual
        args += [y, pmods]
        specs += [tok(d), mod_spec]
    args += [mods, norm_g.reshape(1, d), cos, sin]
    specs += [mod_spec, pl.BlockSpec((1, d), lambda i, t: (0, 0)),
              pl.BlockSpec((TOK_TILE, LANES), lambda i, t: (t, 0)),
              pl.BlockSpec((TOK_TILE, LANES), lambda i, t: (t, 0))]
    for (kind, width, _, _, _), w, g in zip(roles, weights, gains):
        args.append(w)
        specs.append(pl.BlockSpec((d, width), lambda i, t: (0, 0)))
        if kind in ("norm", "norm_rope"):
            args.append(g)
            specs.append(pl.BlockSpec((1, LANES), lambda i, t: (0, 0)))
    out_shapes, out_specs = [], []
    if residual is not None:
        out_shapes.append(jax.ShapeDtypeStruct((b, s, d), F32))
        out_specs.append(tok(d))
    for kind, width, _, transposed, unit in roles:
        dt = F32 if kind == "plain_f32" else BF16
        if transposed:
            rows = width // unit * (unit + ONES_ROWS) if unit else width
            out_shapes.append(jax.ShapeDtypeStruct((b, rows, s), dt))
            out_specs.append(pl.BlockSpec((1, rows, TOK_TILE), lambda i, t: (i, 0, t)))
        else:
            out_shapes.append(jax.ShapeDtypeStruct((b, s, width), dt))
            out_specs.append(tok(width))
    return pl.pallas_call(
        functools.partial(_inproj_kernel, tuple(roles), residual is not None),
        grid=(b, nt),
        in_specs=specs,
        out_specs=out_specs,
        out_shape=out_shapes,
        compiler_params=_cparams(("parallel", "parallel")),
        name="in_proj",
    )(*args)


def _rope_tables(s_total, ctx_len):
    t = jnp.arange(s_total - ctx_len, dtype=jnp.int32)
    pos = jnp.stack([t // GRID_W, t % GRID_W], axis=-1).astype(F32)
    n_freq = HEAD_DIM // 4
    inv_freq = ROPE_THETA ** (-jnp.arange(n_freq, dtype=F32) / n_freq)
    ang = pos[:, :, None] * inv_freq
    cos, sin = jnp.cos(ang), jnp.sin(ang)
    cos64 = jnp.concatenate([cos[:, 0], cos[:, 0], cos[:, 1], cos[:, 1]], axis=-1)
    sin64 = jnp.concatenate([-sin[:, 0], sin[:, 0], -sin[:, 1], sin[:, 1]], axis=-1)
    cos128 = jnp.concatenate([jnp.ones((ctx_len, LANES), F32), jnp.tile(cos64, (1, 2))], axis=0)
    sin128 = jnp.concatenate([jnp.zeros((ctx_len, LANES), F32), jnp.tile(sin64, (1, 2))], axis=0)
    return cos128, sin128


def _attend_all(chains, k_ref, vt_ref, is_latent, lazy_ref):
    t_lat = k_ref.shape[1] - TOK_TILE
    chunk = math.gcd(t_lat, KV_CHUNK)

    def step(keys, carry):
        out = []
        for (qt, key_lanes, value_rows, acc_ref), m_prev in zip(chains, carry):
            s = _dot(k_ref[0, keys, key_lanes], qt)
            m_new = jnp.maximum(m_prev, jnp.max(s, axis=0, keepdims=True))
            p = jnp.exp2(s - m_new)
            acc_ref[...] = jnp.exp2(m_prev - m_new) * acc_ref[...] + _dot(vt_ref[0, value_rows, keys], p.astype(BF16))
            out.append(m_new)
        return tuple(out)

    for _, _, _, acc_ref in chains:
        acc_ref[...] = jnp.zeros(acc_ref.shape, F32)

    def lazy_step(keys, carry):
        out = []
        for (qt, key_lanes, value_rows, acc_ref), m_prev in zip(chains, carry):
            s = _dot(k_ref[0, keys, key_lanes], qt)
            p = jnp.exp2(s - m_prev)
            m_new = jnp.maximum(m_prev, jnp.max(s, axis=0, keepdims=True))
            acc_ref[...] = jnp.exp2(m_prev - m_new) * (acc_ref[...] + _dot(vt_ref[0, value_rows, keys], p.astype(BF16)))
            out.append(m_new)
        return tuple(out)

    def all_chunks(step_fn, m_init):
        carry = step_fn(slice(0, TOK_TILE), tuple(jnp.full((1, c[0].shape[1]), m_init, F32) for c in chains))

        def body(j, carry):
            return step_fn(pl.ds(pl.multiple_of(TOK_TILE + j * chunk, TOK_TILE), chunk), carry)
        lax.fori_loop(0, jnp.where(is_latent, t_lat // chunk, 0), body, carry)

    lazy_ok = lazy_ref[0] > 0.0
    pl.when(lazy_ok)(lambda: all_chunks(lazy_step, -lazy_ref[1]))
    pl.when(jnp.logical_not(lazy_ok))(lambda: all_chunks(step, NEG))
    outs = []
    for _, _, _, acc_ref in chains:
        dv = acc_ref.shape[0] - ONES_ROWS
        outs.append(acc_ref[0:dv, :] / acc_ref[dv:dv + 1, :])
    return outs


def _lazy_softmax_ok(q_gain, k_gain, q_scale):
    bound = NORM_SLACK * HEAD_DIM * jnp.max(jnp.abs(q_gain)) * jnp.max(jnp.abs(k_gain)) * q_scale
    return jnp.stack([(2.0 * bound <= LAZY_EXP_LIMIT).astype(F32), bound.astype(F32)])


def _half_masks(q):
    lo = _lane_iota(q.shape) < HEAD_DIM
    zero = jnp.zeros_like(q)
    return jnp.where(lo, q, zero), jnp.where(lo, zero, q)


def _row_half_masks(qt):
    top = lax.broadcasted_iota(jnp.int32, qt.shape, 0) < HEAD_DIM
    zero = jnp.zeros_like(qt)
    return jnp.where(top, qt, zero), jnp.where(top, zero, qt)


def _diff_attn_kernel(lam_init, lazy_ref, qt_ref, k_ref, vt_ref, lam_ref, subln_ref, o_ref, *acc_refs):
    chains = []
    for j, acc_ref in enumerate(acc_refs):
        lanes = slice(j * LANES, (j + 1) * LANES)
        q1, q2 = _row_half_masks(qt_ref[0, lanes, :])
        values = slice(j * (LANES + ONES_ROWS), (j + 1) * (LANES + ONES_ROWS))
        chains.append((jnp.concatenate([q1, q2], axis=1), lanes, values, acc_ref))
    outs = _attend_all(chains, k_ref, vt_ref, pl.program_id(2) > 0, lazy_ref)
    lv = lam_ref[...]
    lam = (jnp.exp(jnp.sum(lv[0:1] * lv[1:2], axis=-1, keepdims=True))
           - jnp.exp(jnp.sum(lv[2:3] * lv[3:4], axis=-1, keepdims=True)) + lam_init)
    for j, o in enumerate(outs):
        od = o[:, :TOK_TILE] - lam * o[:, TOK_TILE:]
        ms = jnp.mean(od * od, axis=0, keepdims=True)
        on = od * lax.rsqrt(ms + EPS) * subln_ref[...] * (1.0 - lam_init)
        o_ref[0, :, j * LANES:(j + 1) * LANES] = on.T.astype(o_ref.dtype)


def _diff_attn_call(lazy_ok, qt, k, vt, lam_vecs, subln, lam_init):
    b, s, w = k.shape
    wide = ATTN_CHAINS * LANES
    return pl.pallas_call(
        functools.partial(_diff_attn_kernel, lam_init),
        grid=(b, w // wide, s // TOK_TILE),
        in_specs=[
            pl.BlockSpec(memory_space=pltpu.SMEM),
            pl.BlockSpec((1, wide, TOK_TILE), lambda i, h, t: (i, h, t)),
            pl.BlockSpec((1, s, wide), lambda i, h, t: (i, 0, h)),
            pl.BlockSpec((1, ATTN_CHAINS * (LANES + ONES_ROWS), s), lambda i, h, t: (i, h, 0)),
            pl.BlockSpec((4, HEAD_DIM), lambda i, h, t: (0, 0)),
            pl.BlockSpec((LANES, 1), lambda i, h, t: (0, 0)),
        ],
        out_specs=pl.BlockSpec((1, TOK_TILE, wide), lambda i, h, t: (i, t, h)),
        out_shape=jax.ShapeDtypeStruct((b, s, w), BF16),
        scratch_shapes=[pltpu.VMEM((LANES + ONES_ROWS, 2 * TOK_TILE), F32)] * ATTN_CHAINS,
        compiler_params=_cparams(("parallel", "parallel", "parallel")),
        name="diff_attn",
    )(lazy_ok, qt, k, vt, lam_vecs, subln.reshape(LANES, 1))


def _gqa_attn_kernel(lazy_ref, qt_ref, k_ref, vt_ref, o_ref, *acc_refs):
    t = TOK_TILE
    chains = []
    for g, acc_ref in enumerate(acc_refs):
        qa = _row_half_masks(qt_ref[0, 2 * g * LANES:(2 * g + 1) * LANES, :])
        qb = _row_half_masks(qt_ref[0, (2 * g + 1) * LANES:(2 * g + 2) * LANES, :])
        qt = jnp.concatenate([qa[0], qa[1], qb[0], qb[1]], axis=1)
        values = slice(g * (HEAD_DIM + ONES_ROWS), (g + 1) * (HEAD_DIM + ONES_ROWS))
        chains.append((qt, slice(g * LANES, (g + 1) * LANES), values, acc_ref))
    outs = _attend_all(chains, k_ref, vt_ref, pl.program_id(1) > 0, lazy_ref)
    ot = jnp.concatenate([o[:, j * t:(j + 1) * t] for o in outs for j in range(GQA_REP)], axis=0)
    o_ref[0] = ot.T.astype(o_ref.dtype)


def _gqa_attn_call(lazy_ok, qt, k_dup, vt):
    b, w, s = qt.shape
    groups = w // (2 * LANES)
    return pl.pallas_call(
        _gqa_attn_kernel,
        grid=(b, s // TOK_TILE),
        in_specs=[pl.BlockSpec(memory_space=pltpu.SMEM),
                  pl.BlockSpec((1, w, TOK_TILE), lambda i, t: (i, 0, t)),
                  pl.BlockSpec((1, s, groups * LANES), lambda i, t: (i, 0, 0)),
                  pl.BlockSpec((1, groups * (HEAD_DIM + ONES_ROWS), s), lambda i, t: (i, 0, 0))],
        out_specs=pl.BlockSpec((1, TOK_TILE, w), lambda i, t: (i, t, 0)),
        out_shape=jax.ShapeDtypeStruct((b, s, w), BF16),
        scratch_shapes=[pltpu.VMEM((HEAD_DIM + ONES_ROWS, 4 * TOK_TILE), F32)] * groups,
        compiler_params=_cparams(("parallel", "parallel")),
        name="gqa_attn",
    )(lazy_ok, qt, k_dup, vt)


def _na_attn_kernel(n_rows, q_ref, k_ref, v_ref, bias_ref, o_ref):
    t = pl.program_id(2)

    @pl.when(t == 0)
    def _():
        o_ref[0] = jnp.zeros(o_ref.shape[1:], o_ref.dtype)

    @pl.when(t > 0)
    def _():
        win = NA_WIN_ROWS * GRID_W
        rows_per_tile = TOK_TILE // GRID_W
        lo = _lane_iota((GRID_W, LANES)) < HEAD_DIM
        for pair in range(NA_PAIRS):
            lanes = slice(pair * LANES, (pair + 1) * LANES)
            kctx = k_ref[0, 0:TOK_TILE, lanes]
            vctx = v_ref[0, 0:TOK_TILE, lanes]
            for i in range(rows_per_tile):
                r = (t - 1) * rows_per_tile + i
                r0 = jnp.clip(r - NA_WIN_ROWS // 2, 0, n_rows - NA_WIN_ROWS)
                off = pl.multiple_of(TOK_TILE + r0 * GRID_W, GRID_W)
                q1, q2 = _half_masks(q_ref[0, i * GRID_W:(i + 1) * GRID_W, lanes])
                q = jnp.concatenate([q1, q2], axis=0)
                s_nb = _dot(q, k_ref[0, pl.ds(off, win), lanes], _NT) + bias_ref[pair, r - r0]
                s_cx = _dot(q, kctx, _NT)
                m = jnp.maximum(jnp.max(s_nb, axis=-1, keepdims=True), jnp.max(s_cx, axis=-1, keepdims=True))
                p_nb = jnp.exp2(s_nb - m)
                p_cx = jnp.exp2(s_cx - m)
                den = jnp.sum(p_nb, axis=-1, keepdims=True) + jnp.sum(p_cx, axis=-1, keepdims=True)
                o = (_dot(p_nb.astype(BF16), v_ref[0, pl.ds(off, win), lanes]) + _dot(p_cx.astype(BF16), vctx)) / den
                o_ref[0, i * GRID_W:(i + 1) * GRID_W, lanes] = jnp.where(lo, o[:GRID_W], o[GRID_W:]).astype(o_ref.dtype)


def _na_bias_table(rpb):
    kr = NA_WIN_ROWS
    cols = jnp.arange(GRID_W, dtype=jnp.int32)
    c0 = jnp.clip(cols - NA_WIN_COLS // 2, 0, GRID_W - NA_WIN_COLS)
    kc = jnp.arange(GRID_W, dtype=jnp.int32)
    inside = (kc[None, :] >= c0[:, None]) & (kc[None, :] < c0[:, None] + NA_WIN_COLS)
    dc = kc[None, :] - cols[:, None] + (NA_WIN_COLS - 1)
    onehot = ((dc[:, :, None] == jnp.arange(2 * NA_WIN_COLS - 1)) & inside[:, :, None]).astype(F32)
    by_col = jnp.einsum("hrd,ckd->hrck", rpb.astype(F32), onehot, precision=lax.Precision.HIGHEST)
    by_col = jnp.where(inside[None, None], by_col * LOG2E, NEG)
    tab = jnp.stack([by_col[:, NA_WIN_ROWS - 1 - var:2 * NA_WIN_ROWS - 1 - var]
                     for var in range(NA_WIN_ROWS)], axis=1)
    h = rpb.shape[0]
    tab = tab.transpose(0, 1, 3, 2, 4).reshape(h // 2, 2, NA_WIN_ROWS, GRID_W, kr * GRID_W)
    return tab.transpose(0, 2, 1, 3, 4).reshape(h // 2, NA_WIN_ROWS, 2 * GRID_W, kr * GRID_W)


def _na_attn_call(q, k, v, bias):
    b, s, w = q.shape
    n_rows = (s - TOK_TILE) // GRID_W
    wide = NA_PAIRS * LANES
    seq = pl.BlockSpec((1, s, wide), lambda i, h, t: (i, 0, h))
    tile = pl.BlockSpec((1, TOK_TILE, wide), lambda i, h, t: (i, t, h))
    return pl.pallas_call(
        functools.partial(_na_attn_kernel, n_rows),
        grid=(b, w // wide, s // TOK_TILE),
        in_specs=[tile, seq, seq,
                  pl.BlockSpec((NA_PAIRS,) + bias.shape[1:], lambda i, h, t: (h, 0, 0, 0))],
        out_specs=tile,
        out_shape=jax.ShapeDtypeStruct((b, s, w), BF16),
        compiler_params=_cparams(("parallel", "parallel", "parallel")),
        name="na_attn",
    )(q, k, v, bias)


def _pool_kernel(s_total, u_ref, w_ref, scale_ref, o_ref):
    t = pl.program_id(1)
    t0 = t * TOK_TILE
    seg_lo = jnp.where(t == 0, 0, TOK_TILE)
    seg_hi = jnp.where(t == 0, TOK_TILE, s_total)
    span = TOK_TILE + 2 * POOL_HALO
    start = pl.multiple_of(jnp.clip(t0 - POOL_HALO, 0, s_total - span), SUBLANES)
    hi, lo = _split(u_ref[0, pl.ds(start, span), :])
    own = u_ref[0, pl.ds(pl.multiple_of(t0, TOK_TILE), TOK_TILE), :]
    tok_q = t0 + lax.broadcasted_iota(jnp.int32, (TOK_TILE, span), 0)
    tok_k = start + lax.broadcasted_iota(jnp.int32, (TOK_TILE, span), 1)
    group = _lane_iota(own.shape) // HEAD_DIM
    mean = jnp.zeros(own.shape, F32)
    for g, win in enumerate(POOL_WINDOWS):
        lo_t = jnp.maximum(tok_q - win // 2, seg_lo)
        hi_t = jnp.minimum(tok_q + win // 2, seg_hi)
        band = jnp.where(tok_k >= lo_t, jnp.where(tok_k < hi_t, 1.0, 0.0), 0.0).astype(BF16)
        count = (hi_t - lo_t)[:, 0:1].astype(F32)
        total = _dot(band, hi) + _dot(band, lo)
        mean = jnp.where(group == g, total / count, mean)
    p = (mean - own).astype(BF16)
    o_ref[0] = (_dot(p, w_ref[...]) * scale_ref[...]).astype(o_ref.dtype)


def _pool_call(u, w_blockdiag, scale):
    b, s, w = u.shape
    return pl.pallas_call(
        functools.partial(_pool_kernel, s),
        grid=(b, s // TOK_TILE),
        in_specs=[pl.BlockSpec((1, s, w), lambda i, t: (i, 0, 0)),
                  pl.BlockSpec((w, w), lambda i, t: (0, 0)),
                  pl.BlockSpec((1, w), lambda i, t: (0, 0))],
        out_specs=pl.BlockSpec((1, TOK_TILE, w), lambda i, t: (i, t, 0)),
        out_shape=jax.ShapeDtypeStruct((b, s, w), BF16),
        compiler_params=_cparams(("parallel", "parallel")),
        name="pool_mix",
    )(u, w_blockdiag, scale.reshape(1, w))


def _outproj_kernel(n_parts, *refs):
    y_refs = refs[:n_parts]
    w_refs = refs[n_parts:2 * n_parts]
    x_ref, mod_ref, norm_ref, x1_ref, xm_ref = refs[2 * n_parts:]
    acc = _dot(y_refs[0][0], w_refs[0][...])
    for y_ref, w_ref in zip(y_refs[1:], w_refs[1:]):
        acc = acc + _dot(y_ref[0], w_ref[...])
    x1 = x_ref[0] + mod_ref[0, 0, 2:3, :] * acc
    x1_ref[0] = x1
    xm = _rms(x1, norm_ref[...]) * (1.0 + mod_ref[0, 0, 4:5, :]) + mod_ref[0, 0, 3:4, :]
    xm_ref[0] = xm.astype(xm_ref.dtype)


def _outproj_call(parts, weights, x, mods, norm_g):
    b, s, d = x.shape
    tok = lambda w: pl.BlockSpec((1, TOK_TILE, w), lambda i, t: (i, t, 0))
    specs = [tok(p.shape[-1]) for p in parts]
    specs += [pl.BlockSpec(w.shape, lambda i, t: (0, 0)) for w in weights]
    specs += [tok(d),
              pl.BlockSpec((1, 1, N_MOD, d), lambda i, t: (i, jnp.minimum(t, 1), 0, 0)),
              pl.BlockSpec((1, d), lambda i, t: (0, 0))]
    return pl.pallas_call(
        functools.partial(_outproj_kernel, len(parts)),
        grid=(b, s // TOK_TILE),
        in_specs=specs,
        out_specs=[tok(d), tok(d)],
        out_shape=[jax.ShapeDtypeStruct((b, s, d), F32), jax.ShapeDtypeStruct((b, s, d), BF16)],
        compiler_params=_cparams(("parallel", "parallel")),
        name="out_proj",
    )(*parts, *weights, x, mods, norm_g.reshape(1, d))


def _peer_fold_kernel(k_ref, wt_ref, kw_ref):
    k_hi, k_lo = _split(k_ref[0])
    w_hi, w_lo = _split(wt_ref[0])
    kw_ref[0] = _dot3(k_hi, k_lo, w_hi, w_lo).astype(kw_ref.dtype)


def _peer_fold_call(keys_ph, wq_t):
    n, nk, kd = keys_ph.shape
    d = wq_t.shape[-1]
    blk = pl.BlockSpec((1, nk, d), lambda i: (i, 0, 0))
    return pl.pallas_call(
        _peer_fold_kernel,
        grid=(n,),
        in_specs=[pl.BlockSpec((1, nk, kd), lambda i: (i, 0, 0)), pl.BlockSpec((1, kd, d), lambda i: (i, 0, 0))],
        out_specs=blk,
        out_shape=jax.ShapeDtypeStruct((n, nk, d), BF16),
        compiler_params=_cparams(("parallel",)),
        name="peer_fold",
    )(keys_ph, wq_t)


def _peer_scores_kernel(kw_ref, x_ref, st_ref):
    st_ref[...] = _dot(kw_ref[...], x_ref[...], _NT)


def _peer_scores_call(kw, x, tm):
    n, d = x.shape
    r = kw.shape[0]
    return pl.pallas_call(
        _peer_scores_kernel,
        grid=(n // tm,),
        in_specs=[pl.BlockSpec((r, d), lambda t: (0, 0)), pl.BlockSpec((tm, d), lambda t: (t, 0))],
        out_specs=pl.BlockSpec((r, tm), lambda t: (0, t)),
        out_shape=jax.ShapeDtypeStruct((r, n), F32),
        compiler_params=_cparams(("parallel",)),
        name="peer_scores",
    )(kw, x)


def _bitonic_merge_desc(v):
    n = len(v)
    if n == 1:
        return v
    half = n // 2
    top = [jnp.maximum(v[i], v[i + half]) for i in range(half)]
    bot = [jnp.minimum(v[i], v[i + half]) for i in range(half)]
    return _bitonic_merge_desc(top) + _bitonic_merge_desc(bot)


def _sort_desc(v):
    n = len(v)
    if n == 1:
        return v
    return _bitonic_merge_desc(_sort_desc(v[:n // 2]) + _sort_desc(v[n // 2:])[::-1])


def _merge_top(a, b):
    n = len(a)
    return _bitonic_merge_desc([jnp.maximum(a[i], b[n - 1 - i]) for i in range(n)])


def _peer_select_kernel(st_ref, th_ref, e1_ref, e2_ref):
    k = PEER_TOPK
    nk = PEER_N_KEYS
    half_rows = PEER_HEADS * PEER_N_KEYS
    tops = []
    for p in range(2):
        groups = []
        for g in range(PEER_N_KEYS // k):
            vals = [st_ref[pl.ds(p * half_rows + g * k + j, PEER_HEADS, stride=PEER_N_KEYS), :]
                    for j in range(k)]
            groups.append(_sort_desc(vals))
        while len(groups) > 1:
            groups = [_merge_top(groups[i], groups[i + 1]) for i in range(0, len(groups), 2)]
        tops.append(groups[0])
    t1, t2 = tops
    neg = jnp.full(t1[0].shape, NEG, F32)
    rows = [[t1[i] + t2[j] for j in range(k // (i + 1))] for i in range(k)]
    first = _merge_top(rows[0], rows[1] + [neg] * (k - len(rows[1])))
    rest = [c for row in rows[2:] for c in row]
    rest = _sort_desc(rest + [neg] * (2 * k - len(rest)))[:k]
    top = [jnp.maximum(first[i], rest[k - 1 - i]) for i in range(k)]
    tau = functools.reduce(jnp.minimum, top)
    m1, m2 = t1[0], t2[0]
    rz = 1.0 / functools.reduce(lambda a, c: a + c, [jnp.exp(c - rows[0][0]) for c in top])
    th_rank = []
    for i in range(k):
        th_i = jnp.full(tau.shape, -NEG, F32)
        for j in range(len(rows[i])):
            th_i = jnp.where(rows[i][j] >= tau, t2[j], th_i)
        th_rank.append(th_i)
    big = jnp.full((nk, LANES), -NEG, F32)
    for h in range(PEER_HEADS):
        hs = slice(h, h + 1)
        head = slice(h * nk, (h + 1) * nk)
        s1 = st_ref[head, :]
        s2 = st_ref[half_rows + h * nk:half_rows + (h + 1) * nk, :]
        th = big
        for i in range(k):
            th = jnp.where(s1 == t1[i][hs], th_rank[i][hs], th)
        th_ref[head, :] = th
        e1_ref[head, :] = jnp.exp(s1 - m1[hs]) * rz[hs]
        e2_ref[head, :] = jnp.exp(s2 - m2[hs])


def _peer_select_call(st):
    r, n = st.shape
    out = pl.BlockSpec((r // 2, LANES), lambda t: (0, t))
    return pl.pallas_call(
        _peer_select_kernel,
        grid=(n // LANES,),
        in_specs=[pl.BlockSpec((r, LANES), lambda t: (0, t))],
        out_specs=[out, out, out],
        out_shape=[jax.ShapeDtypeStruct((r // 2, n), F32)] * 3,
        compiler_params=_cparams(("parallel",)),
        name="peer_select",
    )(st)


def _peer_dense_kernel(x_ref, th_ref, e1_ref, s2_ref, e2_ref, u_ref, vt_ref, y_ref, acc_ref, w_ref):
    c = pl.program_id(1)
    tm = x_ref.shape[0]
    nk = PEER_N_KEYS
    part_keys = SUBLANES // PEER_KEY_PARTS
    n_sub = nk // PEER_SUB_KEYS

    @pl.when(c == 0)
    def _():
        acc_ref[...] = jnp.zeros(acc_ref.shape, F32)

    def block(i, carry):
        cols = pl.ds(pl.multiple_of((i // n_sub) * LANES, LANES), LANES)
        sub = (i % n_sub) * PEER_SUB_KEYS
        for part in range(PEER_KEY_PARTS):
            w = [None] * part_keys
            for h in range(PEER_HEADS):
                first = pl.ds(pl.multiple_of(h * nk + c * SUBLANES, SUBLANES), SUBLANES)
                second = pl.ds(pl.multiple_of(h * nk + sub, PEER_SUB_KEYS), PEER_SUB_KEYS)
                th8, e18 = th_ref[first, cols], e1_ref[first, cols]
                s2, e2 = s2_ref[second, cols], e2_ref[second, cols]
                for q in range(part_keys):
                    r = part * part_keys + q
                    gate = jnp.where(s2 >= th8[r:r + 1, :], e18[r:r + 1, :] * e2, 0.0)
                    w[q] = gate if h == 0 else w[q] + gate
            for q in range(part_keys):
                rows = pl.ds(pl.multiple_of((part * part_keys + q) * nk + sub, PEER_SUB_KEYS), PEER_SUB_KEYS)
                w_ref[rows, cols] = w[q]
        return carry

    lax.fori_loop(0, (tm // LANES) * n_sub, block, 0)
    hv = _dot(u_ref[...], x_ref[...], _NT)
    g = w_ref[...] * (0.5 * hv * (1.0 + lax.erf(hv * SQRT_HALF)))
    acc_ref[...] += _dot(vt_ref[...], g.astype(BF16))

    @pl.when(c == pl.num_programs(1) - 1)
    def _():
        y_ref[...] = acc_ref[...].T


def _peer_dense_call(x, st, th, e1, e2, u_bf, vt_bf, tm):
    n, d = x.shape
    n_exp = u_bf.shape[0]
    ne = SUBLANES * PEER_N_KEYS
    half = pl.BlockSpec((th.shape[0], tm), lambda t, c: (0, t))
    return pl.pallas_call(
        _peer_dense_kernel,
        grid=(n // tm, n_exp // ne),
        in_specs=[
            pl.BlockSpec((tm, d), lambda t, c: (t, 0)),
            half, half,
            pl.BlockSpec((th.shape[0], tm), lambda t, c: (1, t)),
            half,
            pl.BlockSpec((ne, d), lambda t, c: (c, 0)),
            pl.BlockSpec((d, ne), lambda t, c: (0, c)),
        ],
        out_specs=pl.BlockSpec((tm, d), lambda t, c: (t, 0)),
        out_shape=jax.ShapeDtypeStruct((n, d), F32),
        scratch_shapes=[pltpu.VMEM((d, tm), F32), pltpu.VMEM((ne, tm), F32)],
        compiler_params=_cparams(("parallel", "arbitrary")),
        name="peer_dense",
    )(x, th, e1, st, e2, u_bf, vt_bf)


def _peer(xm, wq, keys, u, v):
    b, s, d = xm.shape
    n = b * s
    heads, _, nk, kd = keys.shape
    keys_ph = keys.transpose(1, 0, 2, 3).reshape(2 * heads, nk, kd)
    wq_t = wq.T.reshape(heads, 2, kd, d).transpose(1, 0, 2, 3).reshape(2 * heads, kd, d)
    kw = _peer_fold_call(keys_ph, wq_t).reshape(2 * heads * nk, d)
    tm = PEER_TOK_TILE if n % PEER_TOK_TILE == 0 else TOK_TILE
    x = xm.reshape(n, d)
    st = _peer_scores_call(kw, x, tm)
    th, e1, e2 = _peer_select_call(st)
    y = _peer_dense_call(x, st, th, e1, e2, u.astype(BF16), v.T.astype(BF16), tm)
    return y.reshape(b, s, d)


def _final_kernel(x_ref, y_ref, mod_ref, o_ref):
    o_ref[0] = x_ref[0] + mod_ref[0, 0, 5:6, :] * y_ref[0]


def _final_call(x1, y, mods):
    b, s, d = x1.shape
    nt = s // TOK_TILE - 1
    src = pl.BlockSpec((1, TOK_TILE, d), lambda i, t: (i, t + 1, 0))
    return pl.pallas_call(
        _final_kernel,
        grid=(b, nt),
        in_specs=[src, src, pl.BlockSpec((1, 1, N_MOD, d), lambda i, t: (i, 1, 0, 0))],
        out_specs=pl.BlockSpec((1, TOK_TILE, d), lambda i, t: (i, t, 0)),
        out_shape=jax.ShapeDtypeStruct((b, nt * TOK_TILE, d), F32),
        compiler_params=_cparams(("parallel", "parallel")),
        name="final_residual",
    )(x1, y, mods)


def _gain2(g):
    return jnp.concatenate([g, g]).reshape(1, LANES).astype(F32)


def kernel(x, c, ctx, c_ctx, l0_ada_w, l0_ada_b, l0_norm1, l0_norm2, l0_w_in, l0_w_out, l0_pool_w, l0_pool_scale, l0_diff_qnorm, l0_diff_knorm, l0_lambda_q1, l0_lambda_k1, l0_lambda_q2, l0_lambda_k2, l0_diff_subln, l0_peer_wq, l0_peer_keys, l0_peer_u, l0_peer_v, l1_ada_w, l1_ada_b, l1_norm1, l1_norm2, l1_w_in, l1_w_out, l1_gqa_qnorm, l1_gqa_knorm, l1_na_qnorm, l1_na_knorm, l1_na_rpb, l1_peer_wq, l1_peer_keys, l1_peer_u, l1_peer_v):
    b, t_lat, d = x.shape
    ctx_len = ctx.shape[1]
    assert ctx_len == TOK_TILE and t_lat % TOK_TILE == 0 and t_lat // GRID_W >= NA_WIN_ROWS
    s = ctx_len + t_lat
    xs = jnp.concatenate([ctx, x], axis=1)
    cos, sin = _rope_tables(s, ctx_len)
    qk_scale = HEAD_DIM ** -0.5 * LOG2E

    mods0 = _modulation(c, c_ctx, l0_ada_w, l0_ada_b)
    w0 = l0_w_in.astype(BF16)
    pw = l0_pool_scale.shape[0]
    dw = DIFF_HEADS * 2 * HEAD_DIM
    roles0 = [("plain_f32", pw, 1.0, False, 0), ("norm_rope", dw, qk_scale, True, 0),
              ("norm_rope", dw, 1.0, False, 0), ("plain", dw, 1.0, True, LANES)]
    weights0 = [w0[:, :pw], w0[:, pw:pw + dw], w0[:, pw + dw:pw + 2 * dw], w0[:, pw + 2 * dw:]]
    gains0 = [None, _gain2(l0_diff_qnorm), _gain2(l0_diff_knorm), None]
    u0, q0, k0, v0 = _inproj_call(xs, mods0, l0_norm1, cos, sin, roles0, weights0, gains0)
    lam_init = 0.8 - 0.6 * math.exp(-0.3 * 0)
    lam_vecs = jnp.stack([l0_lambda_q1, l0_lambda_k1, l0_lambda_q2, l0_lambda_k2]).astype(F32)
    lazy0 = _lazy_softmax_ok(l0_diff_qnorm, l0_diff_knorm, qk_scale)
    o_diff = _diff_attn_call(lazy0, q0, k0, v0, lam_vecs, l0_diff_subln, lam_init)
    pool_bd = jax.scipy.linalg.block_diag(*[l0_pool_w[g] for g in range(l0_pool_w.shape[0])]).astype(BF16)
    y_pool = _pool_call(u0, pool_bd, l0_pool_scale)
    wo0 = l0_w_out.astype(BF16)
    x1, xm0 = _outproj_call([y_pool, o_diff], [wo0[:pw], wo0[pw:]], xs, mods0, l0_norm2)
    y_peer0 = _peer(xm0, l0_peer_wq, l0_peer_keys, l0_peer_u, l0_peer_v)

    mods1 = _modulation(c, c_ctx, l1_ada_w, l1_ada_b)
    w1 = l1_w_in.astype(BF16)
    n_q = 8 * HEAD_DIM
    n_kv = GQA_KV_HEADS * HEAD_DIM
    o_ck, o_cv, o_nq, o_nk, o_nv = n_q, n_q + n_kv, n_q + 2 * n_kv, 2 * n_q + 2 * n_kv, 3 * n_q + 2 * n_kv
    dup = lambda w: jnp.concatenate([w[:, :HEAD_DIM], w[:, :HEAD_DIM], w[:, HEAD_DIM:], w[:, HEAD_DIM:]], axis=1)
    roles1 = [("norm_rope", n_q, qk_scale, True, 0), ("norm_rope", 2 * n_kv, 1.0, False, 0),
              ("plain", n_kv, 1.0, True, HEAD_DIM), ("norm", n_q, qk_scale, False, 0), ("norm", n_q, 1.0, False, 0),
              ("plain", n_q, 1.0, False, 0)]
    weights1 = [w1[:, :o_ck], dup(w1[:, o_ck:o_cv]), w1[:, o_cv:o_nq],
                w1[:, o_nq:o_nk], w1[:, o_nk:o_nv], w1[:, o_nv:]]
    gains1 = [_gain2(l1_gqa_qnorm), _gain2(l1_gqa_knorm), None, _gain2(l1_na_qnorm), _gain2(l1_na_knorm), None]
    x2, cq, ckd, cvd, nq, nk_, nv = _inproj_call(x1, mods1, l1_norm1, cos, sin, roles1, weights1, gains1,
                                                 residual=(y_peer0, mods0))
    o_gqa = _gqa_attn_call(_lazy_softmax_ok(l1_gqa_qnorm, l1_gqa_knorm, qk_scale), cq, ckd, cvd)
    bias = _na_bias_table(l1_na_rpb)
    o_na = _na_attn_call(nq, nk_, nv, bias)
    wo1 = l1_w_out.astype(BF16)
    x3, xm1 = _outproj_call([o_gqa, o_na], [wo1[:n_q], wo1[n_q:]], x2, mods1, l1_norm2)
    y_peer1 = _peer(xm1, l1_peer_wq, l1_peer_keys, l1_peer_u, l1_peer_v)
    return _final_call(x3, y_peer1, mods1)
```

```python
import functools
import math

import jax
import jax.numpy as jnp
from jax import lax
from jax.experimental import pallas as pl
from jax.experimental.pallas import tpu as pltpu

F32 = jnp.float32
BF16 = jnp.bfloat16

LANES = 128
SUBLANES = 8
VMEM_LIMIT_BYTES = 56 * 1024 * 1024

HEAD_DIM = 64
GRID_W = 64
ROPE_THETA = 10000.0
EPS = 1e-6
N_MOD = 6
POOL_WINDOWS = (2, 4, 8, 16)
POOL_HALO = max(POOL_WINDOWS) // 2
DIFF_HEADS = 6
GQA_KV_HEADS = 2
GQA_REP = 4
ROPE_PAIR = HEAD_DIM // 4
NA_WIN_ROWS = 8
NA_WIN_COLS = 16
PEER_HEADS = 8
PEER_N_KEYS = 128
PEER_TOPK = 16
TOK_TILE = 256
PEER_TOK_TILE = 1024
ADA_COLS = 768
KV_CHUNK = 4096
PEER_SUB_KEYS = 64
PEER_KEY_PARTS = 2
ATTN_CHAINS = 2
NA_PAIRS = 2
ONES_ROWS = 16
LOG2E = 1.4426950408889634
LAZY_EXP_LIMIT = 60.0
NORM_SLACK = 1.05
NEG = -1e30
SQRT_HALF = 0.7071067811865476


def _cparams(sem, vmem=VMEM_LIMIT_BYTES):
    return pltpu.CompilerParams(dimension_semantics=sem, vmem_limit_bytes=vmem)


def _split(x):
    hi = x.astype(BF16)
    lo = (x - hi.astype(F32)).astype(BF16)
    return hi, lo


_NN = (((1,), (0,)), ((), ()))
_NT = (((1,), (1,)), ((), ()))


def _dot(a, b, dims=_NN):
    return lax.dot_general(a, b, dims, preferred_element_type=F32)


def _dot3(a_hi, a_lo, b_hi, b_lo, dims=_NN):
    return _dot(a_hi, b_hi, dims) + _dot(a_hi, b_lo, dims) + _dot(a_lo, b_hi, dims)


def _lane_iota(shape):
    return lax.broadcasted_iota(jnp.int32, shape, len(shape) - 1)


def _rms(x, gain):
    ms = jnp.mean(x * x, axis=-1, keepdims=True)
    return x * lax.rsqrt(ms + EPS) * gain


def _ada_kernel(c_ref, w_ref, b_ref, o_ref):
    c = c_ref[...]
    a = c / (1.0 + jnp.exp(-c))
    a_hi, a_lo = _split(a)
    w_hi, w_lo = _split(w_ref[...])
    o_ref[...] = _dot3(a_hi, a_lo, w_hi, w_lo) + b_ref[...]


def _ada_call(cvec, ada_w, ada_b):
    rows, d = cvec.shape
    n = ada_w.shape[1]
    bn = ADA_COLS
    return pl.pallas_call(
        _ada_kernel,
        grid=(n // bn,),
        in_specs=[
            pl.BlockSpec((rows, d), lambda j: (0, 0)),
            pl.BlockSpec((d, bn), lambda j: (0, j)),
            pl.BlockSpec((1, bn), lambda j: (0, j)),
        ],
        out_specs=pl.BlockSpec((rows, bn), lambda j: (0, j)),
        out_shape=jax.ShapeDtypeStruct((rows, n), F32),
        compiler_params=_cparams(("parallel",)),
        name="ada_mod",
    )(cvec, ada_w, ada_b.reshape(1, n))


def _modulation(c, c_ctx, ada_w, ada_b):
    b, d = c.shape
    rows = -(-(b + 1) // 16) * 16
    cvec = jnp.zeros((rows, d), F32).at[0].set(c_ctx).at[1:b + 1].set(c)
    m = _ada_call(cvec, ada_w, ada_b).reshape(rows, N_MOD, d)
    return jnp.stack([jnp.broadcast_to(m[0], (b, N_MOD, d)), m[1:b + 1]], axis=1)


def _head_norm(y, gain):
    r = lax.broadcasted_iota(jnp.int32, (LANES, LANES), 0) // HEAD_DIM
    c = lax.broadcasted_iota(jnp.int32, (LANES, LANES), 1) // HEAD_DIM
    ones_bd = jnp.where(r == c, 1.0, 0.0).astype(BF16)
    hi, lo = _split(y * y)
    ss = _dot(hi, ones_bd) + _dot(lo, ones_bd)
    return y * lax.rsqrt(ss * (1.0 / HEAD_DIM) + EPS) * gain


def _rope(y, cos, sin):
    up = pltpu.roll(y, LANES - ROPE_PAIR, 1)
    down = pltpu.roll(y, ROPE_PAIR, 1)
    partner = jnp.where((_lane_iota(y.shape) & ROPE_PAIR) == 0, up, down)
    return y * cos + partner * sin


def _inproj_kernel(roles, fuse_residual, *refs):
    refs = list(refs)
    x_ref = refs.pop(0)
    if fuse_residual:
        y_ref = refs.pop(0)
        pmod_ref = refs.pop(0)
    mod_ref = refs.pop(0)
    norm_ref = refs.pop(0)
    cos_ref = refs.pop(0)
    sin_ref = refs.pop(0)
    w_refs, g_refs = [], []
    for kind, _, _, _, _ in roles:
        w_refs.append(refs.pop(0))
        g_refs.append(refs.pop(0) if kind in ("norm", "norm_rope") else None)
    if fuse_residual:
        xo_ref = refs.pop(0)
    out_refs = refs

    x = x_ref[0]
    if fuse_residual:
        x = x + pmod_ref[0, 0, 5:6, :] * y_ref[0]
        xo_ref[0] = x
    xm = _rms(x, norm_ref[...]) * (1.0 + mod_ref[0, 0, 1:2, :]) + mod_ref[0, 0, 0:1, :]
    xm = xm.astype(BF16)
    cos = cos_ref[...]
    sin = sin_ref[...]
    for (kind, width, scale, transposed, unit), w_ref, g_ref, o_ref in zip(roles, w_refs, g_refs, out_refs):
        acc = _dot(xm, w_ref[...])
        if kind in ("plain", "plain_f32") and not transposed:
            o_ref[0] = acc.astype(o_ref.dtype)
            continue
        if unit:
            ones = jnp.ones((ONES_ROWS, TOK_TILE), o_ref.dtype)
            for j in range(width // LANES):
                yt = acc[:, j * LANES:(j + 1) * LANES].T.astype(o_ref.dtype)
                for k in range(LANES // unit):
                    base = (j * (LANES // unit) + k) * (unit + ONES_ROWS)
                    o_ref[0, base:base + unit, :] = yt[k * unit:(k + 1) * unit]
                    o_ref[0, base + unit:base + unit + ONES_ROWS, :] = ones
            continue
        for j in range(width // LANES):
            y = acc[:, j * LANES:(j + 1) * LANES]
            if kind in ("norm", "norm_rope"):
                y = _head_norm(y, g_ref[...])
            if kind == "norm_rope":
                y = _rope(y, cos, sin)
            if scale != 1.0:
                y = y * scale
            if transposed:
                o_ref[0, j * LANES:(j + 1) * LANES, :] = y.T.astype(o_ref.dtype)
            else:
                o_ref[0, :, j * LANES:(j + 1) * LANES] = y.astype(o_ref.dtype)


def _inproj_call(x, mods, norm_g, cos, sin, roles, weights, gains, residual=None):
    b, s, d = x.shape
    nt = s // TOK_TILE
    tok = lambda w: pl.BlockSpec((1, TOK_TILE, w), lambda i, t: (i, t, 0))
    mod_spec = pl.BlockSpec((1, 1, N_MOD, d), lambda i, t: (i, jnp.minimum(t, 1), 0, 0))
    args, specs = [x], [tok(d)]
    if residual is not None:
        y, pmods = residual
        args += [y, pmods]
        specs += [tok(d), mod_spec]
    args += [mods, norm_g.reshape(1, d), cos, sin]
    specs += [mod_spec, pl.BlockSpec((1, d), lambda i, t: (0, 0)),
              pl.BlockSpec((TOK_TILE, LANES), lambda i, t: (t, 0)),
              pl.BlockSpec((TOK_TILE, LANES), lambda i, t: (t, 0))]
    for (kind, width, _, _, _), w, g in zip(roles, weights, gains):
        args.append(w)
        specs.append(pl.BlockSpec((d, width), lambda i, t: (0, 0)))
        if kind in ("norm", "norm_rope"):
            args.append(g)
            specs.append(pl.BlockSpec((1, LANES), lambda i, t: (0, 0)))
    out_shapes, out_specs = [], []
    if residual is not None:
        out_shapes.append(jax.ShapeDtypeStruct((b, s, d), F32))
        out_specs.append(tok(d))
    for kind, width, _, transposed, unit in roles:
        dt = F32 if kind == "plain_f32" else BF16
        if transposed:
            rows = width // unit * (unit + ONES_ROWS) if unit else width
            out_shapes.append(jax.ShapeDtypeStruct((b, rows, s), dt))
            out_specs.append(pl.BlockSpec((1, rows, TOK_TILE), lambda i, t: (i, 0, t)))
        else:
            out_shapes.append(jax.ShapeDtypeStruct((b, s, width), dt))
            out_specs.append(tok(width))
    return pl.pallas_call(
        functools.partial(_inproj_kernel, tuple(roles), residual is not None),
        grid=(b, nt),
        in_specs=specs,
        out_specs=out_specs,
        out_shape=out_shapes,
        compiler_params=_cparams(("parallel", "parallel")),
        name="in_proj",
    )(*args)


def _rope_tables(s_total, ctx_len):
    t = jnp.arange(s_total - ctx_len, dtype=jnp.int32)
    pos = jnp.stack([t // GRID_W, t % GRID_W], axis=-1).astype(F32)
    n_freq = HEAD_DIM // 4
    inv_freq = ROPE_THETA ** (-jnp.arange(n_freq, dtype=F32) / n_freq)
    ang = pos[:, :, None] * inv_freq
    cos, sin = jnp.cos(ang), jnp.sin(ang)
    cos64 = jnp.concatenate([cos[:, 0], cos[:, 0], cos[:, 1], cos[:, 1]], axis=-1)
    sin64 = jnp.concatenate([-sin[:, 0], sin[:, 0], -sin[:, 1], sin[:, 1]], axis=-1)
    cos128 = jnp.concatenate([jnp.ones((ctx_len, LANES), F32), jnp.tile(cos64, (1, 2))], axis=0)
    sin128 = jnp.concatenate([jnp.zeros((ctx_len, LANES), F32), jnp.tile(sin64, (1, 2))], axis=0)
    return cos128, sin128


def _attend_all(chains, k_ref, vt_ref, is_latent, lazy_ref):
    t_lat = k_ref.shape[1] - TOK_TILE
    chunk = math.gcd(t_lat, KV_CHUNK)

    def step(keys, carry):
        out = []
        for (qt, key_lanes, value_rows, acc_ref), m_prev in zip(chains, carry):
            s = _dot(k_ref[0, keys, key_lanes], qt)
            m_new = jnp.maximum(m_prev, jnp.max(s, axis=0, keepdims=True))
            p = jnp.exp2(s - m_new)
            acc_ref[...] = jnp.exp2(m_prev - m_new) * acc_ref[...] + _dot(vt_ref[0, value_rows, keys], p.astype(BF16))
            out.append(m_new)
        return tuple(out)

    for _, _, _, acc_ref in chains:
        acc_ref[...] = jnp.zeros(acc_ref.shape, F32)

    def lazy_step(keys, carry):
        out = []
        for (qt, key_lanes, value_rows, acc_ref), m_prev in zip(chains, carry):
            s = _dot(k_ref[0, keys, key_lanes], qt)
            p = jnp.exp2(s - m_prev)
            m_new = jnp.maximum(m_prev, jnp.max(s, axis=0, keepdims=True))
            acc_ref[...] = jnp.exp2(m_prev - m_new) * (acc_ref[...] + _dot(vt_ref[0, value_rows, keys], p.astype(BF16)))
            out.append(m_new)
        return tuple(out)

    def all_chunks(step_fn, m_init):
        carry = step_fn(slice(0, TOK_TILE), tuple(jnp.full((1, c[0].shape[1]), m_init, F32) for c in chains))

        def body(j, carry):
            return step_fn(pl.ds(pl.multiple_of(TOK_TILE + j * chunk, TOK_TILE), chunk), carry)
        lax.fori_loop(0, jnp.where(is_latent, t_lat // chunk, 0), body, carry)

    lazy_ok = lazy_ref[0] > 0.0
    pl.when(lazy_ok)(lambda: all_chunks(lazy_step, -lazy_ref[1]))
    pl.when(jnp.logical_not(lazy_ok))(lambda: all_chunks(step, NEG))
    outs = []
    for _, _, _, acc_ref in chains:
        dv = acc_ref.shape[0] - ONES_ROWS
        outs.append(acc_ref[0:dv, :] / acc_ref[dv:dv + 1, :])
    return outs


def _lazy_softmax_ok(q_gain, k_gain, q_scale):
    bound = NORM_SLACK * HEAD_DIM * jnp.max(jnp.abs(q_gain)) * jnp.max(jnp.abs(k_gain)) * q_scale
    return jnp.stack([(2.0 * bound <= LAZY_EXP_LIMIT).astype(F32), bound.astype(F32)])


def _half_masks(q):
    lo = _lane_iota(q.shape) < HEAD_DIM
    zero = jnp.zeros_like(q)
    return jnp.where(lo, q, zero), jnp.where(lo, zero, q)


def _row_half_masks(qt):
    top = lax.broadcasted_iota(jnp.int32, qt.shape, 0) < HEAD_DIM
    zero = jnp.zeros_like(qt)
    return jnp.where(top, qt, zero), jnp.where(top, zero, qt)


def _diff_attn_kernel(lam_init, lazy_ref, qt_ref, k_ref, vt_ref, lam_ref, subln_ref, o_ref, *acc_refs):
    chains = []
    for j, acc_ref in enumerate(acc_refs):
        lanes = slice(j * LANES, (j + 1) * LANES)
        q1, q2 = _row_half_masks(qt_ref[0, lanes, :])
        values = slice(j * (LANES + ONES_ROWS), (j + 1) * (LANES + ONES_ROWS))
        chains.append((jnp.concatenate([q1, q2], axis=1), lanes, values, acc_ref))
    outs = _attend_all(chains, k_ref, vt_ref, pl.program_id(2) > 0, lazy_ref)
    lv = lam_ref[...]
    lam = (jnp.exp(jnp.sum(lv[0:1] * lv[1:2], axis=-1, keepdims=True))
           - jnp.exp(jnp.sum(lv[2:3] * lv[3:4], axis=-1, keepdims=True)) + lam_init)
    for j, o in enumerate(outs):
        od = o[:, :TOK_TILE] - lam * o[:, TOK_TILE:]
        ms = jnp.mean(od * od, axis=0, keepdims=True)
        on = od * lax.rsqrt(ms + EPS) * subln_ref[...] * (1.0 - lam_init)
        o_ref[0, :, j * LANES:(j + 1) * LANES] = on.T.astype(o_ref.dtype)


def _diff_attn_call(lazy_ok, qt, k, vt, lam_vecs, subln, lam_init):
    b, s, w = k.shape
    wide = ATTN_CHAINS * LANES
    return pl.pallas_call(
        functools.partial(_diff_attn_kernel, lam_init),
        grid=(b, w // wide, s // TOK_TILE),
        in_specs=[
            pl.BlockSpec(memory_space=pltpu.SMEM),
            pl.BlockSpec((1, wide, TOK_TILE), lambda i, h, t: (i, h, t)),
            pl.BlockSpec((1, s, wide), lambda i, h, t: (i, 0, h)),
            pl.BlockSpec((1, ATTN_CHAINS * (LANES + ONES_ROWS), s), lambda i, h, t: (i, h, 0)),
            pl.BlockSpec((4, HEAD_DIM), lambda i, h, t: (0, 0)),
            pl.BlockSpec((LANES, 1), lambda i, h, t: (0, 0)),
        ],
        out_specs=pl.BlockSpec((1, TOK_TILE, wide), lambda i, h, t: (i, t, h)),
        out_shape=jax.ShapeDtypeStruct((b, s, w), BF16),
        scratch_shapes=[pltpu.VMEM((LANES + ONES_ROWS, 2 * TOK_TILE), F32)] * ATTN_CHAINS,
        compiler_params=_cparams(("parallel", "parallel", "parallel")),
        name="diff_attn",
    )(lazy_ok, qt, k, vt, lam_vecs, subln.reshape(LANES, 1))


def _gqa_attn_kernel(lazy_ref, qt_ref, k_ref, vt_ref, o_ref, *acc_refs):
    t = TOK_TILE
    chains = []
    for g, acc_ref in enumerate(acc_refs):
        qa = _row_half_masks(qt_ref[0, 2 * g * LANES:(2 * g + 1) * LANES, :])
        qb = _row_half_masks(qt_ref[0, (2 * g + 1) * LANES:(2 * g + 2) * LANES, :])
        qt = jnp.concatenate([qa[0], qa[1], qb[0], qb[1]], axis=1)
        values = slice(g * (HEAD_DIM + ONES_ROWS), (g + 1) * (HEAD_DIM + ONES_ROWS))
        chains.append((qt, slice(g * LANES, (g + 1) * LANES), values, acc_ref))
    outs = _attend_all(chains, k_ref, vt_ref, pl.program_id(1) > 0, lazy_ref)
    ot = jnp.concatenate([o[:, j * t:(j + 1) * t] for o in outs for j in range(GQA_REP)], axis=0)
    o_ref[0] = ot.T.astype(o_ref.dtype)


def _gqa_attn_call(lazy_ok, qt, k_dup, vt):
    b, w, s = qt.shape
    groups = w // (2 * LANES)
    return pl.pallas_call(
        _gqa_attn_kernel,
        grid=(b, s // TOK_TILE),
        in_specs=[pl.BlockSpec(memory_space=pltpu.SMEM),
                  pl.BlockSpec((1, w, TOK_TILE), lambda i, t: (i, 0, t)),
                  pl.BlockSpec((1, s, groups * LANES), lambda i, t: (i, 0, 0)),
                  pl.BlockSpec((1, groups * (HEAD_DIM + ONES_ROWS), s), lambda i, t: (i, 0, 0))],
        out_specs=pl.BlockSpec((1, TOK_TILE, w), lambda i, t: (i, t, 0)),
        out_shape=jax.ShapeDtypeStruct((b, s, w), BF16),
        scratch_shapes=[pltpu.VMEM((HEAD_DIM + ONES_ROWS, 4 * TOK_TILE), F32)] * groups,
        compiler_params=_cparams(("parallel", "parallel")),
        name="gqa_attn",
    )(lazy_ok, qt, k_dup, vt)


def _na_attn_kernel(n_rows, q_ref, k_ref, v_ref, bias_ref, o_ref):
    t = pl.program_id(2)

    @pl.when(t == 0)
    def _():
        o_ref[0] = jnp.zeros(o_ref.shape[1:], o_ref.dtype)

    @pl.when(t > 0)
    def _():
        win = NA_WIN_ROWS * GRID_W
        rows_per_tile = TOK_TILE // GRID_W
        lo = _lane_iota((GRID_W, LANES)) < HEAD_DIM
        for pair in range(NA_PAIRS):
            lanes = slice(pair * LANES, (pair + 1) * LANES)
            kctx = k_ref[0, 0:TOK_TILE, lanes]
            vctx = v_ref[0, 0:TOK_TILE, lanes]
            for i in range(rows_per_tile):
                r = (t - 1) * rows_per_tile + i
                r0 = jnp.clip(r - NA_WIN_ROWS // 2, 0, n_rows - NA_WIN_ROWS)
                off = pl.multiple_of(TOK_TILE + r0 * GRID_W, GRID_W)
                q1, q2 = _half_masks(q_ref[0, i * GRID_W:(i + 1) * GRID_W, lanes])
                q = jnp.concatenate([q1, q2], axis=0)
                s_nb = _dot(q, k_ref[0, pl.ds(off, win), lanes], _NT) + bias_ref[pair, r - r0]
                s_cx = _dot(q, kctx, _NT)
                m = jnp.maximum(jnp.max(s_nb, axis=-1, keepdims=True), jnp.max(s_cx, axis=-1, keepdims=True))
                p_nb = jnp.exp2(s_nb - m)
                p_cx = jnp.exp2(s_cx - m)
                den = jnp.sum(p_nb, axis=-1, keepdims=True) + jnp.sum(p_cx, axis=-1, keepdims=True)
                o = (_dot(p_nb.astype(BF16), v_ref[0, pl.ds(off, win), lanes]) + _dot(p_cx.astype(BF16), vctx)) / den
                o_ref[0, i * GRID_W:(i + 1) * GRID_W, lanes] = jnp.where(lo, o[:GRID_W], o[GRID_W:]).astype(o_ref.dtype)


def _na_bias_table(rpb):
    kr = NA_WIN_ROWS
    cols = jnp.arange(GRID_W, dtype=jnp.int32)
    c0 = jnp.clip(cols - NA_WIN_COLS // 2, 0, GRID_W - NA_WIN_COLS)
    kc = jnp.arange(GRID_W, dtype=jnp.int32)
    inside = (kc[None, :] >= c0[:, None]) & (kc[None, :] < c0[:, None] + NA_WIN_COLS)
    dc = kc[None, :] - cols[:, None] + (NA_WIN_COLS - 1)
    onehot = ((dc[:, :, None] == jnp.arange(2 * NA_WIN_COLS - 1)) & inside[:, :, None]).astype(F32)
    by_col = jnp.einsum("hrd,ckd->hrck", rpb.astype(F32), onehot, precision=lax.Precision.HIGHEST)
    by_col = jnp.where(inside[None, None], by_col * LOG2E, NEG)
    tab = jnp.stack([by_col[:, NA_WIN_ROWS - 1 - var:2 * NA_WIN_ROWS - 1 - var]
                     for var in range(NA_WIN_ROWS)], axis=1)
    h = rpb.shape[0]
    tab = tab.transpose(0, 1, 3, 2, 4).reshape(h // 2, 2, NA_WIN_ROWS, GRID_W, kr * GRID_W)
    return tab.transpose(0, 2, 1, 3, 4).reshape(h // 2, NA_WIN_ROWS, 2 * GRID_W, kr * GRID_W)


def _na_attn_call(q, k, v, bias):
    b, s, w = q.shape
    n_rows = (s - TOK_TILE) // GRID_W
    wide = NA_PAIRS * LANES
    seq = pl.BlockSpec((1, s, wide), lambda i, h, t: (i, 0, h))
    tile = pl.BlockSpec((1, TOK_TILE, wide), lambda i, h, t: (i, t, h))
    return pl.pallas_call(
        functools.partial(_na_attn_kernel, n_rows),
        grid=(b, w // wide, s // TOK_TILE),
        in_specs=[tile, seq, seq,
                  pl.BlockSpec((NA_PAIRS,) + bias.shape[1:], lambda i, h, t: (h, 0, 0, 0))],
        out_specs=tile,
        out_shape=jax.ShapeDtypeStruct((b, s, w), BF16),
        compiler_params=_cparams(("parallel", "parallel", "parallel")),
        name="na_attn",
    )(q, k, v, bias)


def _pool_kernel(s_total, u_ref, w_ref, scale_ref, o_ref):
    t = pl.program_id(1)
    t0 = t * TOK_TILE
    seg_lo = jnp.where(t == 0, 0, TOK_TILE)
    seg_hi = jnp.where(t == 0, TOK_TILE, s_total)
    span = TOK_TILE + 2 * POOL_HALO
    start = pl.multiple_of(jnp.clip(t0 - POOL_HALO, 0, s_total - span), SUBLANES)
    hi, lo = _split(u_ref[0, pl.ds(start, span), :])
    own = u_ref[0, pl.ds(pl.multiple_of(t0, TOK_TILE), TOK_TILE), :]
    tok_q = t0 + lax.broadcasted_iota(jnp.int32, (TOK_TILE, span), 0)
    tok_k = start + lax.broadcasted_iota(jnp.int32, (TOK_TILE, span), 1)
    group = _lane_iota(own.shape) // HEAD_DIM
    mean = jnp.zeros(own.shape, F32)
    for g, win in enumerate(POOL_WINDOWS):
        lo_t = jnp.maximum(tok_q - win // 2, seg_lo)
        hi_t = jnp.minimum(tok_q + win // 2, seg_hi)
        band = jnp.where(tok_k >= lo_t, jnp.where(tok_k < hi_t, 1.0, 0.0), 0.0).astype(BF16)
        count = (hi_t - lo_t)[:, 0:1].astype(F32)
        total = _dot(band, hi) + _dot(band, lo)
        mean = jnp.where(group == g, total / count, mean)
    p = (mean - own).astype(BF16)
    o_ref[0] = (_dot(p, w_ref[...]) * scale_ref[...]).astype(o_ref.dtype)


def _pool_call(u, w_blockdiag, scale):
    b, s, w = u.shape
    return pl.pallas_call(
        functools.partial(_pool_kernel, s),
        grid=(b, s // TOK_TILE),
        in_specs=[pl.BlockSpec((1, s, w), lambda i, t: (i, 0, 0)),
                  pl.BlockSpec((w, w), lambda i, t: (0, 0)),
                  pl.BlockSpec((1, w), lambda i, t: (0, 0))],
        out_specs=pl.BlockSpec((1, TOK_TILE, w), lambda i, t: (i, t, 0)),
        out_shape=jax.ShapeDtypeStruct((b, s, w), BF16),
        compiler_params=_cparams(("parallel", "parallel")),
        name="pool_mix",
    )(u, w_blockdiag, scale.reshape(1, w))


def _outproj_kernel(n_parts, *refs):
    y_refs = refs[:n_parts]
    w_refs = refs[n_parts:2 * n_parts]
    x_ref, mod_ref, norm_ref, x1_ref, xm_ref = refs[2 * n_parts:]
    acc = _dot(y_refs[0][0], w_refs[0][...])
    for y_ref, w_ref in zip(y_refs[1:], w_refs[1:]):
        acc = acc + _dot(y_ref[0], w_ref[...])
    x1 = x_ref[0] + mod_ref[0, 0, 2:3, :] * acc
    x1_ref[0] = x1
    xm = _rms(x1, norm_ref[...]) * (1.0 + mod_ref[0, 0, 4:5, :]) + mod_ref[0, 0, 3:4, :]
    xm_ref[0] = xm.astype(xm_ref.dtype)


def _outproj_call(parts, weights, x, mods, norm_g):
    b, s, d = x.shape
    tok = lambda w: pl.BlockSpec((1, TOK_TILE, w), lambda i, t: (i, t, 0))
    specs = [tok(p.shape[-1]) for p in parts]
    specs += [pl.BlockSpec(w.shape, lambda i, t: (0, 0)) for w in weights]
    specs += [tok(d),
              pl.BlockSpec((1, 1, N_MOD, d), lambda i, t: (i, jnp.minimum(t, 1), 0, 0)),
              pl.BlockSpec((1, d), lambda i, t: (0, 0))]
    return pl.pallas_call(
        functools.partial(_outproj_kernel, len(parts)),
        grid=(b, s // TOK_TILE),
        in_specs=specs,
        out_specs=[tok(d), tok(d)],
        out_shape=[jax.ShapeDtypeStruct((b, s, d), F32), jax.ShapeDtypeStruct((b, s, d), BF16)],
        compiler_params=_cparams(("parallel", "parallel")),
        name="out_proj",
    )(*parts, *weights, x, mods, norm_g.reshape(1, d))


def _peer_fold_kernel(k_ref, wt_ref, kw_ref):
    k_hi, k_lo = _split(k_ref[0])
    w_hi, w_lo = _split(wt_ref[0])
    kw_ref[0] = _dot3(k_hi, k_lo, w_hi, w_lo).astype(kw_ref.dtype)


def _peer_fold_call(keys_ph, wq_t):
    n, nk, kd = keys_ph.shape
    d = wq_t.shape[-1]
    blk = pl.BlockSpec((1, nk, d), lambda i: (i, 0, 0))
    return pl.pallas_call(
        _peer_fold_kernel,
        grid=(n,),
        in_specs=[pl.BlockSpec((1, nk, kd), lambda i: (i, 0, 0)), pl.BlockSpec((1, kd, d), lambda i: (i, 0, 0))],
        out_specs=blk,
        out_shape=jax.ShapeDtypeStruct((n, nk, d), BF16),
        compiler_params=_cparams(("parallel",)),
        name="peer_fold",
    )(keys_ph, wq_t)


def _peer_scores_kernel(kw_ref, x_ref, st_ref):
    st_ref[...] = _dot(kw_ref[...], x_ref[...], _NT)


def _peer_scores_call(kw, x, tm):
    n, d = x.shape
    r = kw.shape[0]
    return pl.pallas_call(
        _peer_scores_kernel,
        grid=(n // tm,),
        in_specs=[pl.BlockSpec((r, d), lambda t: (0, 0)), pl.BlockSpec((tm, d), lambda t: (t, 0))],
        out_specs=pl.BlockSpec((r, tm), lambda t: (0, t)),
        out_shape=jax.ShapeDtypeStruct((r, n), F32),
        compiler_params=_cparams(("parallel",)),
        name="peer_scores",
    )(kw, x)


def _bitonic_merge_desc(v):
    n = len(v)
    if n == 1:
        return v
    half = n // 2
    top = [jnp.maximum(v[i], v[i + half]) for i in range(half)]
    bot = [jnp.minimum(v[i], v[i + half]) for i in range(half)]
    return _bitonic_merge_desc(top) + _bitonic_merge_desc(bot)


def _sort_desc(v):
    n = len(v)
    if n == 1:
        return v
    return _bitonic_merge_desc(_sort_desc(v[:n // 2]) + _sort_desc(v[n // 2:])[::-1])


def _merge_top(a, b):
    n = len(a)
    return _bitonic_merge_desc([jnp.maximum(a[i], b[n - 1 - i]) for i in range(n)])


def _peer_select_kernel(st_ref, th_ref, e1_ref, e2_ref):
    k = PEER_TOPK
    nk = PEER_N_KEYS
    half_rows = PEER_HEADS * PEER_N_KEYS
    tops = []
    for p in range(2):
        groups = []
        for g in range(PEER_N_KEYS // k):
            vals = [st_ref[pl.ds(p * half_rows + g * k + j, PEER_HEADS, stride=PEER_N_KEYS), :]
                    for j in range(k)]
            groups.append(_sort_desc(vals))
        while len(groups) > 1:
            groups = [_merge_top(groups[i], groups[i + 1]) for i in range(0, len(groups), 2)]
        tops.append(groups[0])
    t1, t2 = tops
    neg = jnp.full(t1[0].shape, NEG, F32)
    rows = [[t1[i] + t2[j] for j in range(k // (i + 1))] for i in range(k)]
    first = _merge_top(rows[0], rows[1] + [neg] * (k - len(rows[1])))
    rest = [c for row in rows[2:] for c in row]
    rest = _sort_desc(rest + [neg] * (2 * k - len(rest)))[:k]
    top = [jnp.maximum(first[i], rest[k - 1 - i]) for i in range(k)]
    tau = functools.reduce(jnp.minimum, top)
    m1, m2 = t1[0], t2[0]
    rz = 1.0 / functools.reduce(lambda a, c: a + c, [jnp.exp(c - rows[0][0]) for c in top])
    th_rank = []
    for i in range(k):
        th_i = jnp.full(tau.shape, -NEG, F32)
        for j in range(len(rows[i])):
            th_i = jnp.where(rows[i][j] >= tau, t2[j], th_i)
        th_rank.append(th_i)
    big = jnp.full((nk, LANES), -NEG, F32)
    for h in range(PEER_HEADS):
        hs = slice(h, h + 1)
        head = slice(h * nk, (h + 1) * nk)
        s1 = st_ref[head, :]
        s2 = st_ref[half_rows + h * nk:half_rows + (h + 1) * nk, :]
        th = big
        for i in range(k):
            th = jnp.where(s1 == t1[i][hs], th_rank[i][hs], th)
        th_ref[head, :] = th
        e1_ref[head, :] = jnp.exp(s1 - m1[hs]) * rz[hs]
        e2_ref[head, :] = jnp.exp(s2 - m2[hs])


def _peer_select_call(st):
    r, n = st.shape
    out = pl.BlockSpec((r // 2, LANES), lambda t: (0, t))
    return pl.pallas_call(
        _peer_select_kernel,
        grid=(n // LANES,),
        in_specs=[pl.BlockSpec((r, LANES), lambda t: (0, t))],
        out_specs=[out, out, out],
        out_shape=[jax.ShapeDtypeStruct((r // 2, n), F32)] * 3,
        compiler_params=_cparams(("parallel",)),
        name="peer_select",
    )(st)


def _peer_dense_kernel(x_ref, th_ref, e1_ref, s2_ref, e2_ref, u_ref, vt_ref, y_ref, acc_ref, w_ref):
    c = pl.program_id(1)
    tm = x_ref.shape[0]
    nk = PEER_N_KEYS
    part_keys = SUBLANES // PEER_KEY_PARTS
    n_sub = nk // PEER_SUB_KEYS

    @pl.when(c == 0)
    def _():
        acc_ref[...] = jnp.zeros(acc_ref.shape, F32)

    def block(i, carry):
        cols = pl.ds(pl.multiple_of((i // n_sub) * LANES, LANES), LANES)
        sub = (i % n_sub) * PEER_SUB_KEYS
        for part in range(PEER_KEY_PARTS):
            w = [None] * part_keys
            for h in range(PEER_HEADS):
                first = pl.ds(pl.multiple_of(h * nk + c * SUBLANES, SUBLANES), SUBLANES)
                second = pl.ds(pl.multiple_of(h * nk + sub, PEER_SUB_KEYS), PEER_SUB_KEYS)
                th8, e18 = th_ref[first, cols], e1_ref[first, cols]
                s2, e2 = s2_ref[second, cols], e2_ref[second, cols]
                for q in range(part_keys):
                    r = part * part_keys + q
                    gate = jnp.where(s2 >= th8[r:r + 1, :], e18[r:r + 1, :] * e2, 0.0)
                    w[q] = gate if h == 0 else w[q] + gate
            for q in range(part_keys):
                rows = pl.ds(pl.multiple_of((part * part_keys + q) * nk + sub, PEER_SUB_KEYS), PEER_SUB_KEYS)
                w_ref[rows, cols] = w[q]
        return carry

    lax.fori_loop(0, (tm // LANES) * n_sub, block, 0)
    hv = _dot(u_ref[...], x_ref[...], _NT)
    g = w_ref[...] * (0.5 * hv * (1.0 + lax.erf(hv * SQRT_HALF)))
    acc_ref[...] += _dot(vt_ref[...], g.astype(BF16))

    @pl.when(c == pl.num_programs(1) - 1)
    def _():
        y_ref[...] = acc_ref[...].T


def _peer_dense_call(x, st, th, e1, e2, u_bf, vt_bf, tm):
    n, d = x.shape
    n_exp = u_bf.shape[0]
    ne = SUBLANES * PEER_N_KEYS
    once = pl.Buffered(1)
    half = pl.BlockSpec((th.shape[0], tm), lambda t, c: (0, t), pipeline_mode=once)
    return pl.pallas_call(
        _peer_dense_kernel,
        grid=(n // tm, n_exp // ne),
        in_specs=[
            pl.BlockSpec((tm, d), lambda t, c: (t, 0), pipeline_mode=once),
            half, half,
            pl.BlockSpec((th.shape[0], tm), lambda t, c: (1, t), pipeline_mode=once),
            half,
            pl.BlockSpec((ne, d), lambda t, c: (c, 0)),
            pl.BlockSpec((d, ne), lambda t, c: (0, c)),
        ],
        out_specs=pl.BlockSpec((tm, d), lambda t, c: (t, 0)),
        out_shape=jax.ShapeDtypeStruct((n, d), F32),
        scratch_shapes=[pltpu.VMEM((d, tm), F32), pltpu.VMEM((ne, tm), F32)],
        compiler_params=_cparams(("parallel", "arbitrary")),
        name="peer_dense",
    )(x, th, e1, st, e2, u_bf, vt_bf)


def _peer(xm, wq, keys, u, v):
    b, s, d = xm.shape
    n = b * s
    heads, _, nk, kd = keys.shape
    keys_ph = keys.transpose(1, 0, 2, 3).reshape(2 * heads, nk, kd)
    wq_t = wq.T.reshape(heads, 2, kd, d).transpose(1, 0, 2, 3).reshape(2 * heads, kd, d)
    kw = _peer_fold_call(keys_ph, wq_t).reshape(2 * heads * nk, d)
    tm = PEER_TOK_TILE if n % PEER_TOK_TILE == 0 else TOK_TILE
    x = xm.reshape(n, d)
    st = _peer_scores_call(kw, x, tm)
    th, e1, e2 = _peer_select_call(st)
    y = _peer_dense_call(x, st, th, e1, e2, u.astype(BF16), v.T.astype(BF16), tm)
    return y.reshape(b, s, d)


def _final_kernel(x_ref, y_ref, mod_ref, o_ref):
    o_ref[0] = x_ref[0] + mod_ref[0, 0, 5:6, :] * y_ref[0]


def _final_call(x1, y, mods):
    b, s, d = x1.shape
    nt = s // TOK_TILE - 1
    src = pl.BlockSpec((1, TOK_TILE, d), lambda i, t: (i, t + 1, 0))
    return pl.pallas_call(
        _final_kernel,
        grid=(b, nt),
        in_specs=[src, src, pl.BlockSpec((1, 1, N_MOD, d), lambda i, t: (i, 1, 0, 0))],
        out_specs=pl.BlockSpec((1, TOK_TILE, d), lambda i, t: (i, t, 0)),
        out_shape=jax.ShapeDtypeStruct((b, nt * TOK_TILE, d), F32),
        compiler_params=_cparams(("parallel", "parallel")),
        name="final_residual",
    )(x1, y, mods)


def _gain2(g):
    return jnp.concatenate([g, g]).reshape(1, LANES).astype(F32)


def kernel(x, c, ctx, c_ctx, l0_ada_w, l0_ada_b, l0_norm1, l0_norm2, l0_w_in, l0_w_out, l0_pool_w, l0_pool_scale, l0_diff_qnorm, l0_diff_knorm, l0_lambda_q1, l0_lambda_k1, l0_lambda_q2, l0_lambda_k2, l0_diff_subln, l0_peer_wq, l0_peer_keys, l0_peer_u, l0_peer_v, l1_ada_w, l1_ada_b, l1_norm1, l1_norm2, l1_w_in, l1_w_out, l1_gqa_qnorm, l1_gqa_knorm, l1_na_qnorm, l1_na_knorm, l1_na_rpb, l1_peer_wq, l1_peer_keys, l1_peer_u, l1_peer_v):
    b, t_lat, d = x.shape
    ctx_len = ctx.shape[1]
    assert ctx_len == TOK_TILE and t_lat % TOK_TILE == 0 and t_lat // GRID_W >= NA_WIN_ROWS
    s = ctx_len + t_lat
    xs = jnp.concatenate([ctx, x], axis=1)
    cos, sin = _rope_tables(s, ctx_len)
    qk_scale = HEAD_DIM ** -0.5 * LOG2E

    mods0 = _modulation(c, c_ctx, l0_ada_w, l0_ada_b)
    w0 = l0_w_in.astype(BF16)
    pw = l0_pool_scale.shape[0]
    dw = DIFF_HEADS * 2 * HEAD_DIM
    roles0 = [("plain_f32", pw, 1.0, False, 0), ("norm_rope", dw, qk_scale, True, 0),
              ("norm_rope", dw, 1.0, False, 0), ("plain", dw, 1.0, True, LANES)]
    weights0 = [w0[:, :pw], w0[:, pw:pw + dw], w0[:, pw + dw:pw + 2 * dw], w0[:, pw + 2 * dw:]]
    gains0 = [None, _gain2(l0_diff_qnorm), _gain2(l0_diff_knorm), None]
    u0, q0, k0, v0 = _inproj_call(xs, mods0, l0_norm1, cos, sin, roles0, weights0, gains0)
    lam_init = 0.8 - 0.6 * math.exp(-0.3 * 0)
    lam_vecs = jnp.stack([l0_lambda_q1, l0_lambda_k1, l0_lambda_q2, l0_lambda_k2]).astype(F32)
    lazy0 = _lazy_softmax_ok(l0_diff_qnorm, l0_diff_knorm, qk_scale)
    o_diff = _diff_attn_call(lazy0, q0, k0, v0, lam_vecs, l0_diff_subln, lam_init)
    pool_bd = jax.scipy.linalg.block_diag(*[l0_pool_w[g] for g in range(l0_pool_w.shape[0])]).astype(BF16)
    y_pool = _pool_call(u0, pool_bd, l0_pool_scale)
    wo0 = l0_w_out.astype(BF16)
    x1, xm0 = _outproj_call([y_pool, o_diff], [wo0[:pw], wo0[pw:]], xs, mods0, l0_norm2)
    y_peer0 = _peer(xm0, l0_peer_wq, l0_peer_keys, l0_peer_u, l0_peer_v)

    mods1 = _modulation(c, c_ctx, l1_ada_w, l1_ada_b)
    w1 = l1_w_in.astype(BF16)
    n_q = 8 * HEAD_DIM
    n_kv = GQA_KV_HEADS * HEAD_DIM
    o_ck, o_cv, o_nq, o_nk, o_nv = n_q, n_q + n_kv, n_q + 2 * n_kv, 2 * n_q + 2 * n_kv, 3 * n_q + 2 * n_kv
    dup = lambda w: jnp.concatenate([w[:, :HEAD_DIM], w[:, :HEAD_DIM], w[:, HEAD_DIM:], w[:, HEAD_DIM:]], axis=1)
    roles1 = [("norm_rope", n_q, qk_scale, True, 0), ("norm_rope", 2 * n_kv, 1.0, False, 0),
              ("plain", n_kv, 1.0, True, HEAD_DIM), ("norm", n_q, qk_scale, False, 0), ("norm", n_q, 1.0, False, 0),
              ("plain", n_q, 1.0, False, 0)]
    weights1 = [w1[:, :o_ck], dup(w1[:, o_ck:o_cv]), w1[:, o_cv:o_nq],
                w1[:, o_nq:o_nk], w1[:, o_nk:o_nv], w1[:, o_nv:]]
    gains1 = [_gain2(l1_gqa_qnorm), _gain2(l1_gqa_knorm), None, _gain2(l1_na_qnorm), _gain2(l1_na_knorm), None]
    x2, cq, ckd, cvd, nq, nk_, nv = _inproj_call(x1, mods1, l1_norm1, cos, sin, roles1, weights1, gains1,
                                                 residual=(y_peer0, mods0))
    o_gqa = _gqa_attn_call(_lazy_softmax_ok(l1_gqa_qnorm, l1_gqa_knorm, qk_scale), cq, ckd, cvd)
    bias = _na_bias_table(l1_na_rpb)
    o_na = _na_attn_call(nq, nk_, nv, bias)
    wo1 = l1_w_out.astype(BF16)
    x3, xm1 = _outproj_call([o_gqa, o_na], [wo1[:n_q], wo1[n_q:]], x2, mods1, l1_norm2)
    y_peer1 = _peer(xm1, l1_peer_wq, l1_peer_keys, l1_peer_u, l1_peer_v)
    return _final_call(x3, y_peer1, mods1)
```
